```python
import math
import jax, jax.numpy as jnp
from jax import lax
import numpy as np

D_MODEL = 1024
BATCH = 32
SEQ = 2048
DEPTH = 2

N_MIXERS = 2
ROPE_THETA = 10000.0
EPS = 1e-6
Q_BLOCK = 128
NEG_INF = -1e30
MAX_POS_OFFSET = 1024
DA_HEAD_DIM = 64
DA_HEADS = D_MODEL // (2 * DA_HEAD_DIM)
MLA_HEADS = 16
MLA_NOPE = 64
MLA_ROPE = 32
MLA_V = 64
MLA_Q_RANK = 384
MLA_KV_RANK = 256
N_GROUPS = 4
EXPERTS_PER_GROUP = 8
TOP_K = 2
EXPERT_FF = 256

kernel_name = "hybrid_diffattn_mla_hmoe_adaln"


def rms_norm(x, g):
    xf = x.astype(jnp.float32)
    y = xf * lax.rsqrt(jnp.mean(xf * xf, axis=-1, keepdims=True) + EPS)
    return (y * g.astype(jnp.float32)).astype(x.dtype)


def rope_tables(positions, dim):
    inv_freq = ROPE_THETA ** (-jnp.arange(0, dim, 2, dtype=jnp.float32) / dim)
    ang = positions.astype(jnp.float32)[..., None] * inv_freq
    return jnp.cos(ang), jnp.sin(ang)


def apply_rope(x, cos, sin):
    extra = x.ndim - 3
    shp = cos.shape[:2] + (1,) * extra + cos.shape[-1:]
    cos = cos.reshape(shp)
    sin = sin.reshape(shp)
    xf = x.astype(jnp.float32)
    x1, x2 = jnp.split(xf, 2, axis=-1)
    return jnp.concatenate([x1 * cos - x2 * sin, x2 * cos + x1 * sin], axis=-1).astype(x.dtype)


def modulate(x, shift, scale):
    return x * (1.0 + scale[:, None, :]) + shift[:, None, :]


def causal_mask(start, q_len, k_len):
    qpos = start + jnp.arange(q_len)
    kpos = jnp.arange(k_len)
    return kpos[None, :] <= qpos[:, None]


def diff_attention(h, cos, sin, w_qkv, q_norm, k_norm, lq1, lk1, lq2, lk2, subln, w_o, lambda_init):
    B, S, _ = h.shape
    H, dh = DA_HEADS, DA_HEAD_DIM
    q, k, v = jnp.split(h @ w_qkv, 3, axis=-1)
    q = q.reshape(B, S, H, 2, dh)
    k = k.reshape(B, S, H, 2, dh)
    v = v.reshape(B, S, H, 2 * dh)
    q = apply_rope(rms_norm(q, q_norm), cos, sin)
    k = apply_rope(rms_norm(k, k_norm), cos, sin)
    lam = (jnp.exp(jnp.sum(lq1.astype(jnp.float32) * lk1.astype(jnp.float32)))
           - jnp.exp(jnp.sum(lq2.astype(jnp.float32) * lk2.astype(jnp.float32))) + lambda_init)
    scale = dh ** -0.5
    outs = []
    for j in range(S // Q_BLOCK):
        s0 = j * Q_BLOCK
        L = s0 + Q_BLOCK
        s = jnp.einsum('bqhcd,bkhcd->bhcqk', q[:, s0:L], k[:, :L],
                       preferred_element_type=jnp.float32) * scale
        p = jax.nn.softmax(jnp.where(causal_mask(s0, Q_BLOCK, L), s, NEG_INF), axis=-1)
        a = p[:, :, 0] - lam * p[:, :, 1]
        outs.append(jnp.einsum('bhqk,bkhd->bqhd', a.astype(v.dtype), v[:, :L]))
    o = jnp.concatenate(outs, axis=1)
    o = rms_norm(o, subln) * (1.0 - lambda_init)
    return o.reshape(B, S, H * 2 * dh) @ w_o


def mla_attention(h, cos, sin, w_in, q_a_norm, kv_a_norm, w_uq, w_ukv,
                  q_nope_norm, q_pe_norm, k_nope_norm, k_pe_norm, w_o):
    B, S, _ = h.shape
    H = MLA_HEADS
    lat = h @ w_in
    cq = lat[..., :MLA_Q_RANK]
    ckv = lat[..., MLA_Q_RANK:MLA_Q_RANK + MLA_KV_RANK]
    k_pe = lat[..., MLA_Q_RANK + MLA_KV_RANK:]
    q = (rms_norm(cq, q_a_norm) @ w_uq).reshape(B, S, H, MLA_NOPE + MLA_ROPE)
    kv = (rms_norm(ckv, kv_a_norm) @ w_ukv).reshape(B, S, H, MLA_NOPE + MLA_V)
    q_nope = rms_norm(q[..., :MLA_NOPE], q_nope_norm)
    q_pe = apply_rope(rms_norm(q[..., MLA_NOPE:], q_pe_norm), cos, sin)
    k_nope = rms_norm(kv[..., :MLA_NOPE], k_nope_norm)
    v = kv[..., MLA_NOPE:]
    k_pe = apply_rope(rms_norm(k_pe, k_pe_norm), cos, sin)
    scale = (MLA_NOPE + MLA_ROPE) ** -0.5
    outs = []
    for j in range(S // Q_BLOCK):
        s0 = j * Q_BLOCK
        L = s0 + Q_BLOCK
        s = (jnp.einsum('bqhd,bkhd->bhqk', q_nope[:, s0:L], k_nope[:, :L], preferred_element_type=jnp.float32)
             + jnp.einsum('bqhr,bkr->bhqk', q_pe[:, s0:L], k_pe[:, :L], preferred_element_type=jnp.float32)) * scale
        p = jax.nn.softmax(jnp.where(causal_mask(s0, Q_BLOCK, L), s, NEG_INF), axis=-1)
        outs.append(jnp.einsum('bhqk,bkhd->bqhd', p.astype(v.dtype), v[:, :L]))
    o = jnp.concatenate(outs, axis=1)
    return o.reshape(B, S, H * MLA_V) @ w_o


def hier_moe(h, w_group, b_group, w_router, b_router, w1, w3, w2):
    B, S, D = h.shape
    G, E = N_GROUPS, EXPERTS_PER_GROUP
    t = h.reshape(B * S, D)
    g_prob = jax.nn.softmax((t @ w_group).astype(jnp.float32), axis=-1)
    g_idx = jnp.argmax(g_prob + b_group.astype(jnp.float32), axis=-1)
    g_gate = jnp.take_along_axis(g_prob, g_idx[:, None], axis=1)
    e_logit = (t @ w_router).astype(jnp.float32).reshape(-1, G, E)
    e_logit = jnp.take_along_axis(e_logit, g_idx[:, None, None], axis=1)[:, 0]
    e_prob = jax.nn.softmax(e_logit, axis=-1)
    _, e_idx = lax.top_k(e_prob + b_router.astype(jnp.float32)[g_idx], TOP_K)
    e_w = jnp.take_along_axis(e_prob, e_idx, axis=1)
    e_w = e_w / jnp.sum(e_w, axis=-1, keepdims=True) * g_gate
    within = jnp.einsum('nk,nke->ne', e_w, jax.nn.one_hot(e_idx, E, dtype=jnp.float32))
    combine = (jax.nn.one_hot(g_idx, G, dtype=jnp.float32)[:, :, None] * within[:, None, :]).astype(t.dtype)
    out = jnp.zeros_like(t)
    for g in range(G):
        a = jnp.einsum('nd,edf->nef', t, w1[g])
        b = jnp.einsum('nd,edf->nef', t, w3[g])
        hid = jax.nn.silu(a) * b * combine[:, g, :, None]
        out = out + jnp.einsum('nef,efd->nd', hid, w2[g])
    return out.reshape(B, S, D)


def setup_inputs(seed: int = 0) -> dict:
    key = jax.random.key(seed)
    ks = iter(jax.random.split(key, 40))
    f32 = jnp.float32
    D = D_MODEL
    n_da = (DEPTH + 1) // 2
    n_mla = DEPTH // 2
    G, E, F = N_GROUPS, EXPERTS_PER_GROUP, EXPERT_FF

    def nrm(shape, scale):
        return jax.random.normal(next(ks), shape, f32) * scale

    def gain(shape):
        return 1.0 + 0.02 * jax.random.normal(next(ks), shape, f32)

    x = nrm((BATCH, SEQ, D), 1.0)
    c = nrm((BATCH, D), 1.0)
    offset = jax.random.randint(next(ks), (BATCH, 1), 0, MAX_POS_OFFSET, dtype=jnp.int32)
    positions = (offset + jnp.arange(SEQ, dtype=jnp.int32)[None, :]).astype(jnp.int32)
    qk_mla = MLA_NOPE + MLA_ROPE
    return {
        "x": x,
        "c": c,
        "positions": positions,
        "ada_w": nrm((DEPTH, D, 6 * D), 0.5 * D ** -0.5),
        "ada_b": nrm((DEPTH, 6 * D), 0.02),
        "norm_mix": gain((DEPTH, D)),
        "norm_ffn": gain((DEPTH, D)),
        "da_w_qkv": nrm((n_da, D, 3 * D), D ** -0.5),
        "da_q_norm": gain((n_da, DA_HEAD_DIM)),
        "da_k_norm": gain((n_da, DA_HEAD_DIM)),
        "da_lambda_q1": nrm((n_da, DA_HEAD_DIM), 0.1),
        "da_lambda_k1": nrm((n_da, DA_HEAD_DIM), 0.1),
        "da_lambda_q2": nrm((n_da, DA_HEAD_DIM), 0.1),
        "da_lambda_k2": nrm((n_da, DA_HEAD_DIM), 0.1),
        "da_subln": gain((n_da, 2 * DA_HEAD_DIM)),
        "da_w_o": nrm((n_da, D, D), D ** -0.5),
        "mla_w_in": nrm((n_mla, D, MLA_Q_RANK + MLA_KV_RANK + MLA_ROPE), D ** -0.5),
        "mla_q_a_norm": gain((n_mla, MLA_Q_RANK)),
        "mla_kv_a_norm": gain((n_mla, MLA_KV_RANK)),
        "mla_w_uq": nrm((n_mla, MLA_Q_RANK, MLA_HEADS * qk_mla), MLA_Q_RANK ** -0.5),
        "mla_w_ukv": nrm((n_mla, MLA_KV_RANK, MLA_HEADS * (MLA_NOPE + MLA_V)), MLA_KV_RANK ** -0.5),
        "mla_q_nope_norm": gain((n_mla, MLA_NOPE)),
        "mla_q_pe_norm": gain((n_mla, MLA_ROPE)),
        "mla_k_nope_norm": gain((n_mla, MLA_NOPE)),
        "mla_k_pe_norm": gain((n_mla, MLA_ROPE)),
        "mla_w_o": nrm((n_mla, MLA_HEADS * MLA_V, D), (MLA_HEADS * MLA_V) ** -0.5),
        "moe_w_group": nrm((DEPTH, D, G), D ** -0.5),
        "moe_b_group": nrm((DEPTH, G), 0.01),
        "moe_w_router": nrm((DEPTH, D, G * E), D ** -0.5),
        "moe_b_router": nrm((DEPTH, G, E), 0.01),
        "moe_w1": nrm((DEPTH, G, E, D, F), D ** -0.5),
        "moe_w3": nrm((DEPTH, G, E, D, F), D ** -0.5),
        "moe_w2": nrm((DEPTH, G, E, F, D), F ** -0.5),
    }


def reference(x, c, positions, ada_w, ada_b, norm_mix, norm_ffn,
              da_w_qkv, da_q_norm, da_k_norm, da_lambda_q1, da_lambda_k1, da_lambda_q2, da_lambda_k2,
              da_subln, da_w_o,
              mla_w_in, mla_q_a_norm, mla_kv_a_norm, mla_w_uq, mla_w_ukv,
              mla_q_nope_norm, mla_q_pe_norm, mla_k_nope_norm, mla_k_pe_norm, mla_w_o,
              moe_w_group, moe_b_group, moe_w_router, moe_b_router, moe_w1, moe_w3, moe_w2):
    cos_da, sin_da = rope_tables(positions, DA_HEAD_DIM)
    cos_mla, sin_mla = rope_tables(positions, MLA_ROPE)
    cond = jax.nn.silu(c)
    for i in range(DEPTH):
        mod = cond @ ada_w[i] + ada_b[i]
        sh_m, sc_m, g_m, sh_f, sc_f, g_f = jnp.split(mod, 6, axis=-1)
        h = modulate(rms_norm(x, norm_mix[i]), sh_m, sc_m)
        j = i // N_MIXERS
        if i % N_MIXERS == 0:
            lambda_init = 0.8 - 0.6 * math.exp(-0.3 * i)
            y = diff_attention(h, cos_da, sin_da, da_w_qkv[j], da_q_norm[j], da_k_norm[j],
                               da_lambda_q1[j], da_lambda_k1[j], da_lambda_q2[j], da_lambda_k2[j],
                               da_subln[j], da_w_o[j], lambda_init)
        else:
            y = mla_attention(h, cos_mla, sin_mla, mla_w_in[j], mla_q_a_norm[j], mla_kv_a_norm[j],
                              mla_w_uq[j], mla_w_ukv[j], mla_q_nope_norm[j], mla_q_pe_norm[j],
                              mla_k_nope_norm[j], mla_k_pe_norm[j], mla_w_o[j])
        x = x + g_m[:, None, :] * y
        h = modulate(rms_norm(x, norm_ffn[i]), sh_f, sc_f)
        x = x + g_f[:, None, :] * hier_moe(h, moe_w_group[i], moe_b_group[i], moe_w_router[i],
                                           moe_b_router[i], moe_w1[i], moe_w3[i], moe_w2[i])
    return x
```

```python
import functools
import math

import numpy as np
import jax
import jax.numpy as jnp
from jax import lax
from jax.experimental import pallas as pl
from jax.experimental.pallas import tpu as pltpu

F32 = jnp.float32
BF16 = jnp.bfloat16

ROPE_THETA = 10000.0
EPS = 1e-6
NEG_INF = -1e30
DA_HEAD_DIM = 64
MLA_HEADS = 16
MLA_NOPE = 64
MLA_ROPE = 32
MLA_V = 64
MLA_Q_RANK = 384
MLA_KV_RANK = 256
N_GROUPS = 4
EXPERTS_PER_GROUP = 8
EXPERT_FF = 256

LANES = 128
VMEM_LIMIT = 56 * 1024 * 1024
TOK_TILE = 512
ATT_TILE = 256
MOE_TILE = 256
ROW_TILE = 1024
MLA_IN_PAD = 768
EXT_W = 128

N_PAIRS = EXPERTS_PER_GROUP * (EXPERTS_PER_GROUP - 1) // 2
N_COMBOS = N_GROUPS * N_PAIRS


def _combo_tables():
    lo_t, hi_t = [], []
    for g in range(N_GROUPS):
        for lo in range(EXPERTS_PER_GROUP):
            for hi in range(lo + 1, EXPERTS_PER_GROUP):
                lo_t.append(g * EXPERTS_PER_GROUP + lo)
                hi_t.append(g * EXPERTS_PER_GROUP + hi)
    return np.asarray(lo_t, np.int32), np.asarray(hi_t, np.int32)


_COMBO_LO, _COMBO_HI = _combo_tables()


def _cparams(sem):
    return pltpu.CompilerParams(dimension_semantics=sem, vmem_limit_bytes=VMEM_LIMIT)


def _split_bf16(x):
    hi = x.astype(BF16)
    lo = (x - hi.astype(F32)).astype(BF16)
    return jnp.concatenate([hi, lo], axis=1)


def _seg_rsqrt_bcast(t, seg_ref, exp_ref):
    ss = jnp.dot((t * t).astype(BF16), seg_ref[...], preferred_element_type=F32)
    r = lax.rsqrt(ss + EPS)
    return jnp.dot(_split_bf16(r), exp_ref[...], preferred_element_type=F32)


def _rope(t, cosf, sinf, low_mask, half, width):
    partner = jnp.where(low_mask, pltpu.roll(t, width - half, 1), pltpu.roll(t, half, 1))
    return t * cosf + partner * sinf


def _adaln_h(x, mod_ref, g_ref, shift_row, scale_row):
    ms = jnp.mean(x * x, axis=-1, keepdims=True)
    h = x * lax.rsqrt(ms + EPS) * g_ref[...]
    return h * (1.0 + mod_ref[0, scale_row:scale_row + 1, :]) + mod_ref[0, shift_row:shift_row + 1, :]


def _mod_kernel(c_ref, w_ref, b_ref, o_ref):
    c = c_ref[...]
    cond = (c * jax.nn.sigmoid(c)).astype(BF16)
    o_ref[0] = jnp.dot(cond, w_ref[0].astype(BF16), preferred_element_type=F32) + b_ref[0]


def _modulation(c, ada_w, ada_b):
    depth, d, n6 = ada_w.shape
    b = c.shape[0]
    tn = 1536
    return pl.pallas_call(
        _mod_kernel,
        grid=(depth, n6 // tn),
        in_specs=[pl.BlockSpec((b, d), lambda i, j: (0, 0)),
                  pl.BlockSpec((1, d, tn), lambda i, j: (i, 0, j)),
                  pl.BlockSpec((1, 1, tn), lambda i, j: (i, 0, j))],
        out_specs=pl.BlockSpec((1, b, tn), lambda i, j: (i, 0, j)),
        out_shape=jax.ShapeDtypeStruct((depth, b, n6), F32),
        compiler_params=_cparams(("arbitrary", "arbitrary")),
    )(c, ada_w, ada_b.reshape(depth, 1, n6))


def _da_pre_kernel(x_ref, mod_ref, pos_ref, g_ref, w_ref, qg_ref, kg_ref, invf_ref, sgn_ref,
                   seg_ref, exp_ref, q_out, k_out, v_out, *, d, scale):
    h = _adaln_h(x_ref[0], mod_ref, g_ref, 0, 1)
    qkv = jnp.dot(h.astype(BF16), w_ref[...], preferred_element_type=F32)
    q = qkv[:, :d]
    k = qkv[:, d:2 * d]
    v_out[0] = qkv[:, 2 * d:].astype(BF16)

    ang = pos_ref[0] * invf_ref[...]
    reps = d // LANES
    cosf = jnp.tile(jnp.cos(ang), (1, reps))
    sinf = jnp.tile(jnp.sin(ang) * sgn_ref[...], (1, reps))
    lane = lax.broadcasted_iota(jnp.int32, (1, d), 1)
    half = DA_HEAD_DIM // 2
    low = (lane & (DA_HEAD_DIM - 1)) < half

    qn = q * _seg_rsqrt_bcast(q, seg_ref, exp_ref) * qg_ref[...]
    q_out[0] = (_rope(qn, cosf, sinf, low, half, d) * scale).astype(BF16)
    kn = k * _seg_rsqrt_bcast(k, seg_ref, exp_ref) * kg_ref[...]
    k_out[0] = _rope(kn, cosf, sinf, low, half, d).astype(BF16)


def _da_pre(x, mod, posf, norm_g, w_qkv, q_norm, k_norm):
    b, s, d = x.shape
    t = min(TOK_TILE, s)
    dh = DA_HEAD_DIM
    nseg = d // dh
    lane = np.arange(LANES)
    invf = (ROPE_THETA ** (-np.arange(0, dh, 2, dtype=np.float32) / dh)).astype(np.float32)
    invf_row = invf[lane % (dh // 2)][None, :]
    sgn_row = np.where((lane % dh) < dh // 2, -1.0, 1.0).astype(np.float32)[None, :]
    seg = np.zeros((d, LANES), np.float32)
    seg[np.arange(d), np.arange(d) // dh] = 1.0 / dh
    expm = np.zeros((2 * LANES, d), np.float32)
    expm[np.arange(d) // dh, np.arange(d)] = 1.0
    expm[LANES + np.arange(d) // dh, np.arange(d)] = 1.0
    assert nseg <= LANES
    row = lambda bi, ti: (0, 0)
    tok = lambda bi, ti: (bi, ti, 0)
    out = jax.ShapeDtypeStruct((b, s, d), BF16)
    return pl.pallas_call(
        functools.partial(_da_pre_kernel, d=d, scale=dh ** -0.5),
        grid=(b, s // t),
        in_specs=[pl.BlockSpec((1, t, d), tok),
                  pl.BlockSpec((1, 6, d), lambda bi, ti: (bi, 0, 0)),
                  pl.BlockSpec((1, t, 1), tok),
                  pl.BlockSpec((1, d), row),
                  pl.BlockSpec((d, 3 * d), row),
                  pl.BlockSpec((1, d), row),
                  pl.BlockSpec((1, d), row),
                  pl.BlockSpec((1, LANES), row),
                  pl.BlockSpec((1, LANES), row),
                  pl.BlockSpec((d, LANES), row),
                  pl.BlockSpec((2 * LANES, d), row)],
        out_specs=[pl.BlockSpec((1, t, d), tok)] * 3,
        out_shape=[out, out, out],
        compiler_params=_cparams(("arbitrary", "arbitrary")),
    )(x, mod, posf, norm_g.reshape(1, d), w_qkv.astype(BF16),
      jnp.tile(q_norm, nseg).reshape(1, d), jnp.tile(k_norm, nseg).reshape(1, d),
      jnp.asarray(invf_row), jnp.asarray(sgn_row), jnp.asarray(seg, BF16), jnp.asarray(expm, BF16))


def _softmax_step(qs, kblk, vblk, carry, mask):
    m, l, acc = carry
    s = lax.dot_general(qs, kblk, (((1,), (1,)), ((), ())), preferred_element_type=F32)
    if mask is not None:
        s = jnp.where(mask, s, NEG_INF)
    m_new = jnp.maximum(m, jnp.max(s, axis=1, keepdims=True))
    alpha = jnp.exp(m - m_new)
    p = jnp.exp(s - m_new)
    l = alpha * l + jnp.sum(p, axis=1, keepdims=True)
    acc = alpha * acc + jnp.dot(p.astype(BF16), vblk, preferred_element_type=F32)
    return m_new, l, acc


def _init_carry(rows):
    return (jnp.full((rows, 1), NEG_INF, F32), jnp.zeros((rows, 1), F32), jnp.zeros((rows, LANES), F32))


def _causal_mask(rows, tq):
    r = lax.broadcasted_iota(jnp.int32, (rows, tq), 0)
    r = jnp.where(r >= tq, r - tq, r)
    c = lax.broadcasted_iota(jnp.int32, (rows, tq), 1)
    return c <= r


def _da_attn_kernel(lq1_ref, lk1_ref, lq2_ref, lk2_ref, sub_ref, q_ref, k_ref, v_ref, o_ref, *, tq, lam_init):
    qi = pl.program_id(2)
    q = q_ref[0]
    lane = lax.broadcasted_iota(jnp.int32, (tq, LANES), 1)
    zero = jnp.zeros_like(q)
    qs = jnp.concatenate([jnp.where(lane < DA_HEAD_DIM, q, zero), jnp.where(lane >= DA_HEAD_DIM, q, zero)], axis=0)

    def body(kb, carry):
        off = pl.multiple_of(kb * tq, tq)
        return _softmax_step(qs, k_ref[0, pl.ds(off, tq), :], v_ref[0, pl.ds(off, tq), :], carry, None)

    carry = lax.fori_loop(0, qi, body, _init_carry(2 * tq))
    off = pl.multiple_of(qi * tq, tq)
    _, l, acc = _softmax_step(qs, k_ref[0, pl.ds(off, tq), :], v_ref[0, pl.ds(off, tq), :], carry,
                              _causal_mask(2 * tq, tq))
    o = acc / l
    lam = (jnp.exp(jnp.sum(lq1_ref[...] * lk1_ref[...], axis=1, keepdims=True))
           - jnp.exp(jnp.sum(lq2_ref[...] * lk2_ref[...], axis=1, keepdims=True)) + lam_init)
    dd = o[:tq] - lam * o[tq:]
    ms = jnp.mean(dd * dd, axis=-1, keepdims=True)
    o_ref[0] = (dd * lax.rsqrt(ms + EPS) * sub_ref[...] * (1.0 - lam_init)).astype(BF16)


def _da_attn(q, k, v, lq1, lk1, lq2, lk2, subln, lam_init):
    b, s, d = q.shape
    nh = d // LANES
    tq = min(ATT_TILE, s)
    vec = lambda bi, hi, qi: (0, 0)
    return pl.pallas_call(
        functools.partial(_da_attn_kernel, tq=tq, lam_init=lam_init),
        grid=(b, nh, s // tq),
        in_specs=[pl.BlockSpec((1, DA_HEAD_DIM), vec)] * 4 + [
            pl.BlockSpec((1, LANES), vec),
            pl.BlockSpec((1, tq, LANES), lambda bi, hi, qi: (bi, qi, hi)),
            pl.BlockSpec((1, s, LANES), lambda bi, hi, qi: (bi, 0, hi)),
            pl.BlockSpec((1, s, LANES), lambda bi, hi, qi: (bi, 0, hi))],
        out_specs=pl.BlockSpec((1, tq, LANES), lambda bi, hi, qi: (bi, qi, hi)),
        out_shape=jax.ShapeDtypeStruct((b, s, d), BF16),
        compiler_params=_cparams(("arbitrary", "arbitrary", "arbitrary")),
    )(lq1.reshape(1, -1), lk1.reshape(1, -1), lq2.reshape(1, -1), lk2.reshape(1, -1),
      subln.reshape(1, -1), q, k, v)


def _mla_attn_kernel(q_ref, k_ref, v_ref, o_ref, *, tq):
    qi = pl.program_id(2)
    qa = q_ref[0, :, :LANES]
    qb = q_ref[0, :, LANES:]

    def step(off, carry, mask):
        ca, cb = carry
        vblk = v_ref[0, pl.ds(off, tq), :]
        ca = _softmax_step(qa, k_ref[0, pl.ds(off, tq), :LANES], vblk, ca, mask)
        cb = _softmax_step(qb, k_ref[0, pl.ds(off, tq), LANES:], vblk, cb, mask)
        return ca, cb

    carry = lax.fori_loop(0, qi, lambda kb, c: step(pl.multiple_of(kb * tq, tq), c, None),
                          (_init_carry(tq), _init_carry(tq)))
    (_, la, acca), (_, lb, accb) = step(pl.multiple_of(qi * tq, tq), carry, _causal_mask(tq, tq))
    lane = lax.broadcasted_iota(jnp.int32, (tq, LANES), 1)
    o_ref[0] = jnp.where(lane < MLA_V, acca / la, accb / lb).astype(BF16)


def _mla_attn(q, k, v):
    b, s, dq = q.shape
    npair = dq // (2 * LANES)
    tq = min(ATT_TILE, s)
    return pl.pallas_call(
        functools.partial(_mla_attn_kernel, tq=tq),
        grid=(b, npair, s // tq),
        in_specs=[pl.BlockSpec((1, tq, 2 * LANES), lambda bi, hi, qi: (bi, qi, hi)),
                  pl.BlockSpec((1, s, 2 * LANES), lambda bi, hi, qi: (bi, 0, hi)),
                  pl.BlockSpec((1, s, LANES), lambda bi, hi, qi: (bi, 0, hi))],
        out_specs=pl.BlockSpec((1, tq, LANES), lambda bi, hi, qi: (bi, qi, hi)),
        out_shape=jax.ShapeDtypeStruct((b, s, npair * LANES), BF16),
        compiler_params=_cparams(("arbitrary", "arbitrary", "arbitrary")),
    )(q, k, v)


def _mla_pre_kernel(x_ref, mod_ref, pos_ref, g_ref, win_ref, qag_ref, kvag_ref, kpeg_ref,
                    wuq_ref, wuk_ref, wuv_ref, qg_ref, kg_ref, invf_ref, sgn_ref,
                    segq_ref, expq_ref, segk_ref, expk_ref, place_ref,
                    q_out, k_out, v_out, *, scale):
    h = _adaln_h(x_ref[0], mod_ref, g_ref, 0, 1)
    lat = jnp.dot(h.astype(BF16), win_ref[...], preferred_element_type=F32)
    cq = lat[:, :MLA_Q_RANK]
    ckv = lat[:, MLA_Q_RANK:MLA_Q_RANK + MLA_KV_RANK]
    kpe = lat[:, MLA_Q_RANK + MLA_KV_RANK:]
    cqn = (cq * lax.rsqrt(jnp.mean(cq * cq, axis=-1, keepdims=True) + EPS) * qag_ref[...]).astype(BF16)
    ckvn = (ckv * lax.rsqrt(jnp.mean(ckv * ckv, axis=-1, keepdims=True) + EPS) * kvag_ref[...]).astype(BF16)
    kpen = kpe * lax.rsqrt(jnp.sum(kpe * kpe, axis=-1, keepdims=True) * (1.0 / MLA_ROPE) + EPS) * kpeg_ref[...]

    q = jnp.dot(cqn, wuq_ref[...], preferred_element_type=F32)
    kn = jnp.dot(ckvn, wuk_ref[...], preferred_element_type=F32)
    v_out[0] = jnp.dot(ckvn, wuv_ref[...], preferred_element_type=F32).astype(BF16)

    w = q.shape[1]
    reps = w // LANES
    ang = pos_ref[0] * invf_ref[...]
    cosf = jnp.tile(jnp.cos(ang), (1, reps))
    sinf = jnp.tile(jnp.sin(ang) * sgn_ref[...], (1, reps))
    lane = lax.broadcasted_iota(jnp.int32, (1, w), 1)
    half = MLA_ROPE // 2
    low = (lane & (MLA_ROPE - 1)) < half

    qn = q * _seg_rsqrt_bcast(q, segq_ref, expq_ref) * qg_ref[...]
    q_out[0] = (_rope(qn, cosf, sinf, low, half, w) * scale).astype(BF16)
    knn = kn * _seg_rsqrt_bcast(kn, segk_ref, expk_ref) * kg_ref[...]
    kfull = knn + jnp.dot(_split_bf16(kpen), place_ref[...], preferred_element_type=F32)
    k_out[0] = _rope(kfull, cosf, sinf, low, half, w).astype(BF16)


def _mla_pre(x, mod, posf, norm_g, w_in, q_a_norm, kv_a_norm, w_uq, w_ukv,
             q_nope_norm, q_pe_norm, k_nope_norm, k_pe_norm):
    b, s, d = x.shape
    t = min(TOK_TILE, s)
    nh, nope, rope, vd = MLA_HEADS, MLA_NOPE, MLA_ROPE, MLA_V
    w = nh * LANES
    in_w = MLA_Q_RANK + MLA_KV_RANK + rope
    w_in_p = jnp.pad(w_in, ((0, 0), (0, MLA_IN_PAD - in_w))).astype(BF16)
    wuq = w_uq.reshape(MLA_Q_RANK, nh, nope + rope)
    wuq_p = jnp.pad(wuq, ((0, 0), (0, 0), (0, LANES - nope - rope))).reshape(MLA_Q_RANK, w).astype(BF16)
    wukv = w_ukv.reshape(MLA_KV_RANK, nh, nope + vd)
    wuk_p = jnp.pad(wukv[:, :, :nope], ((0, 0), (0, 0), (0, LANES - nope))).reshape(MLA_KV_RANK, w).astype(BF16)
    wuv = wukv[:, :, nope:].reshape(MLA_KV_RANK, nh * vd).astype(BF16)
    zpad = jnp.zeros((LANES - nope - rope,), F32)
    qg = jnp.tile(jnp.concatenate([q_nope_norm, q_pe_norm, zpad]), nh).reshape(1, w)
    kg = jnp.tile(jnp.concatenate([k_nope_norm, jnp.zeros((LANES - nope,), F32)]), nh).reshape(1, w)
    kpeg = jnp.concatenate([k_pe_norm, jnp.zeros((LANES - rope,), F32)]).reshape(1, LANES)

    lane = np.arange(LANES)
    in_rope = (lane >= nope) & (lane < nope + rope)
    invf = (ROPE_THETA ** (-np.arange(0, rope, 2, dtype=np.float32) / rope)).astype(np.float32)
    invf_row = np.where(in_rope, invf[(lane - nope) % (rope // 2)], 0.0).astype(np.float32)[None, :]
    sgn_row = np.where(in_rope, np.where((lane - nope) < rope // 2, -1.0, 1.0), 0.0).astype(np.float32)[None, :]

    col = np.arange(w)
    hd, off = col // LANES, col % LANES
    segq = np.zeros((w, LANES), np.float32)
    expq = np.zeros((2 * LANES, w), np.float32)
    is_nope, is_pe = off < nope, (off >= nope) & (off < nope + rope)
    segq[col[is_nope], 2 * hd[is_nope]] = 1.0 / nope
    segq[col[is_pe], 2 * hd[is_pe] + 1] = 1.0 / rope
    for base in (0, LANES):
        expq[base + 2 * hd[is_nope], col[is_nope]] = 1.0
        expq[base + 2 * hd[is_pe] + 1, col[is_pe]] = 1.0
    segk = np.zeros((w, LANES), np.float32)
    expk = np.zeros((2 * LANES, w), np.float32)
    segk[col[is_nope], hd[is_nope]] = 1.0 / nope
    place = np.zeros((2 * LANES, w), np.float32)
    for base in (0, LANES):
        expk[base + hd[is_nope], col[is_nope]] = 1.0
        place[base + off[is_pe] - nope, col[is_pe]] = 1.0
    assert 2 * nh <= LANES

    row = lambda bi, ti: (0, 0)
    tok = lambda bi, ti: (bi, ti, 0)
    full = lambda a: pl.BlockSpec(a.shape, row)
    consts = [norm_g.reshape(1, d), w_in_p, q_a_norm.reshape(1, -1), kv_a_norm.reshape(1, -1), kpeg,
              wuq_p, wuk_p, wuv, qg, kg, jnp.asarray(invf_row), jnp.asarray(sgn_row),
              jnp.asarray(segq, BF16), jnp.asarray(expq, BF16), jnp.asarray(segk, BF16),
              jnp.asarray(expk, BF16), jnp.asarray(place, BF16)]
    return pl.pallas_call(
        functools.partial(_mla_pre_kernel, scale=(nope + rope) ** -0.5),
        grid=(b, s // t),
        in_specs=[pl.BlockSpec((1, t, d), tok),
                  pl.BlockSpec((1, 6, d), lambda bi, ti: (bi, 0, 0)),
                  pl.BlockSpec((1, t, 1), tok)] + [full(a) for a in consts],
        out_specs=[pl.BlockSpec((1, t, w), tok), pl.BlockSpec((1, t, w), tok),
                   pl.BlockSpec((1, t, nh * vd), tok)],
        out_shape=[jax.ShapeDtypeStruct((b, s, w), BF16), jax.ShapeDtypeStruct((b, s, w), BF16),
                   jax.ShapeDtypeStruct((b, s, nh * vd), BF16)],
        compiler_params=_cparams(("arbitrary", "arbitrary")),
    )(x, mod, posf, *consts)


def _post_kernel(o_ref, wo_ref, x_ref, mod_ref, g_ref, wr_ref, bsel_ref,
                 x_out, h_out, route_out, cnt_out, run_ref, *, d):
    first = (pl.program_id(0) == 0) & (pl.program_id(1) == 0)

    @pl.when(first)
    def _():
        run_ref[...] = jnp.zeros_like(run_ref)

    y = jnp.dot(o_ref[0], wo_ref[...], preferred_element_type=F32)
    x = x_ref[0] + mod_ref[0, 2:3, :] * y
    x_out[0] = x
    h = _adaln_h(x, mod_ref, g_ref, 3, 4)
    t = h.shape[0]

    hh = h.astype(BF16)
    hl = (h - hh.astype(F32)).astype(BF16)
    logits = (jnp.dot(hh, wr_ref[0], preferred_element_type=F32)
              + jnp.dot(hl, wr_ref[0], preferred_element_type=F32)
              + jnp.dot(hh, wr_ref[1], preferred_element_type=F32))

    lane_i = lax.broadcasted_iota(jnp.int32, (t, LANES), 1)
    lane = lane_i.astype(F32)
    big = float(LANES)
    ng, ne = N_GROUPS, EXPERTS_PER_GROUP

    def first_argmax(val):
        mx = jnp.max(val, axis=1, keepdims=True)
        return jnp.min(jnp.where(val == mx, lane, big), axis=1, keepdims=True)

    def pick(val, idx):
        return jnp.sum(jnp.where(lane == idx, val, 0.0), axis=1, keepdims=True)

    gmask = lane_i < ng
    gl = jnp.where(gmask, logits, NEG_INF)
    ge = jnp.exp(gl - jnp.max(gl, axis=1, keepdims=True))
    gprob = ge / jnp.sum(ge, axis=1, keepdims=True)
    gidx = first_argmax(jnp.where(gmask, gprob + bsel_ref[...], NEG_INF))
    ggate = pick(gprob, gidx)

    base = ng + ne * gidx
    emask = (lane >= base) & (lane < base + ne)
    el = jnp.where(emask, logits, NEG_INF)
    ee = jnp.exp(el - jnp.max(el, axis=1, keepdims=True))
    eprob = ee / jnp.sum(ee, axis=1, keepdims=True)
    sel = jnp.where(emask, eprob + bsel_ref[...], NEG_INF)
    i1 = first_argmax(sel)
    i2 = first_argmax(jnp.where(lane == i1, NEG_INF, sel))
    p1 = pick(eprob, i1)
    p2 = pick(eprob, i2)
    psum = p1 + p2
    w1 = p1 / psum * ggate
    w2 = p2 / psum * ggate
    e1 = i1 - base
    e2 = i2 - base
    swap = e2 < e1
    lo = jnp.where(swap, e2, e1)
    hi = jnp.where(swap, e1, e2)
    wa = jnp.where(swap, w2, w1)
    wb = jnp.where(swap, w1, w2)
    combo = gidx * float(N_PAIRS) + lo * (2.0 * ne - 1.0 - lo) * 0.5 + (hi - lo - 1.0)

    h_out[0, :, :d] = h
    h_out[0, :, d:] = jnp.where(lane_i == 0, wa, jnp.where(lane_i == 1, wb, 0.0))

    onehot = lane == combo
    r_i = lax.broadcasted_iota(jnp.int32, (t, t), 0)
    c_i = lax.broadcasted_iota(jnp.int32, (t, t), 1)
    tri = jnp.where(c_i < r_i, 1.0, 0.0).astype(BF16)
    oh = jnp.where(onehot, 1.0, 0.0)
    before = jnp.dot(tri, oh.astype(BF16), preferred_element_type=F32) + run_ref[...]
    rank = jnp.sum(jnp.where(onehot, before, 0.0), axis=1, keepdims=True)
    run = run_ref[...] + jnp.sum(oh, axis=0, keepdims=True)
    run_ref[...] = run
    cnt_out[...] = run
    route_out[0] = jnp.where(lane_i == 0, combo, jnp.where(lane_i == 1, rank, 0.0))


def _post(o, w_o, x, mod, norm_g, w_group, b_group, w_router, b_router):
    b, s, d = x.shape
    t = min(TOK_TILE, s)
    do = o.shape[2]
    ng, ne = N_GROUPS, EXPERTS_PER_GROUP
    wr = jnp.pad(jnp.concatenate([w_group, w_router], axis=1), ((0, 0), (0, LANES - ng - ng * ne)))
    wr_hi = wr.astype(BF16)
    wr_lo = (wr - wr_hi.astype(F32)).astype(BF16)
    wr2 = jnp.stack([wr_hi, wr_lo])
    bsel = jnp.pad(jnp.concatenate([b_group, b_router.reshape(-1)]), (0, LANES - ng - ng * ne)).reshape(1, LANES)
    row = lambda bi, ti: (0, 0)
    tok = lambda bi, ti: (bi, ti, 0)
    return pl.pallas_call(
        functools.partial(_post_kernel, d=d),
        grid=(b, s // t),
        in_specs=[pl.BlockSpec((1, t, do), tok),
                  pl.BlockSpec((do, d), row),
                  pl.BlockSpec((1, t, d), tok),
                  pl.BlockSpec((1, 6, d), lambda bi, ti: (bi, 0, 0)),
                  pl.BlockSpec((1, d), row),
                  pl.BlockSpec((2, d, LANES), lambda bi, ti: (0, 0, 0)),
                  pl.BlockSpec((1, LANES), row)],
        out_specs=[pl.BlockSpec((1, t, d), tok),
                   pl.BlockSpec((1, t, d + EXT_W), tok),
                   pl.BlockSpec((1, t, LANES), tok),
                   pl.BlockSpec((1, LANES), row)],
        out_shape=[jax.ShapeDtypeStruct((b, s, d), F32),
                   jax.ShapeDtypeStruct((b, s, d + EXT_W), F32),
                   jax.ShapeDtypeStruct((b, s, LANES), F32),
                   jax.ShapeDtypeStruct((1, LANES), F32)],
        scratch_shapes=[pltpu.VMEM((1, LANES), F32)],
        compiler_params=_cparams(("arbitrary", "arbitrary")),
    )(o, w_o.astype(BF16), x, mod, norm_g.reshape(1, d), wr2, bsel)


def _row_copy(src_ref, src_row, dst_ref, dst_row, sem):
    return pltpu.make_async_copy(src_ref.at[pl.ds(src_row, 1), :], dst_ref.at[pl.ds(dst_row, 1), :], sem)


def _dispatch_kernel(dest_ref, h_ref, init_ref, hs_ref, sem, *, rows):
    del init_ref

    def start(r, c):
        _row_copy(h_ref, r, hs_ref, dest_ref[r], sem).start()
        return c

    def wait(r, c):
        _row_copy(h_ref, 0, hs_ref, 0, sem).wait()
        return c

    lax.fori_loop(0, rows, start, 0)
    lax.fori_loop(0, rows, wait, 0)


def _dispatch(dest, h2, n_rows):
    n, wd = h2.shape
    rows = min(ROW_TILE, n)
    return pl.pallas_call(
        functools.partial(_dispatch_kernel, rows=rows),
        grid=(n // rows,),
        in_specs=[pl.BlockSpec((rows,), lambda i: (i,), memory_space=pltpu.SMEM),
                  pl.BlockSpec((rows, wd), lambda i: (i, 0)),
                  pl.BlockSpec(memory_space=pl.ANY)],
        out_specs=pl.BlockSpec(memory_space=pl.ANY),
        out_shape=jax.ShapeDtypeStruct((n_rows, wd), F32),
        scratch_shapes=[pltpu.SemaphoreType.DMA(())],
        input_output_aliases={2: 0},
        compiler_params=_cparams(("arbitrary",)),
    )(dest, h2, jnp.zeros((n_rows, wd), F32))


def _moe_kernel(elo_ref, ehi_ref, blk_ref, nact_ref, hs_ref, w1a_ref, w3a_ref, w2a_ref,
                w1b_ref, w3b_ref, w2b_ref, y_ref, *, d):
    del elo_ref, ehi_ref, blk_ref
    active = pl.program_id(0) < nact_ref[0]

    @pl.when(jnp.logical_not(active))
    def _():
        y_ref[...] = jnp.zeros_like(y_ref)

    @pl.when(active)
    def _():
        x = hs_ref[:, :d].astype(BF16)
        wa = hs_ref[:, d:d + 1]
        wb = hs_ref[:, d + 1:d + 2]

        def expert(w1_ref, w3_ref, wgt):
            a = jnp.dot(x, w1_ref[0], preferred_element_type=F32)
            g = jnp.dot(x, w3_ref[0], preferred_element_type=F32)
            return (a * jax.nn.sigmoid(a) * g * wgt).astype(BF16)

        y_ref[...] = (jnp.dot(expert(w1a_ref, w3a_ref, wa), w2a_ref[0], preferred_element_type=F32)
                      + jnp.dot(expert(w1b_ref, w3b_ref, wb), w2b_ref[0], preferred_element_type=F32))


def _moe(hs, e_lo, e_hi, blk, nact, w1, w3, w2):
    n_rows, wd = hs.shape
    d = wd - EXT_W
    ff = w1.shape[2]
    tm = MOE_TILE
    n_tiles = n_rows // tm
    wspec = lambda shape, which: pl.BlockSpec(shape, (lambda j, lo, hi, bk, na: (lo[j], 0, 0)) if which == 0
                                              else (lambda j, lo, hi, bk, na: (hi[j], 0, 0)))
    grid_spec = pltpu.PrefetchScalarGridSpec(
        num_scalar_prefetch=4,
        grid=(n_tiles,),
        in_specs=[pl.BlockSpec((tm, wd), lambda j, lo, hi, bk, na: (bk[j], 0)),
                  wspec((1, d, ff), 0), wspec((1, d, ff), 0), wspec((1, ff, d), 0),
                  wspec((1, d, ff), 1), wspec((1, d, ff), 1), wspec((1, ff, d), 1)],
        out_specs=pl.BlockSpec((tm, d), lambda j, lo, hi, bk, na: (j, 0)),
    )
    return pl.pallas_call(
        functools.partial(_moe_kernel, d=d),
        grid_spec=grid_spec,
        out_shape=jax.ShapeDtypeStruct((n_rows, d), F32),
        compiler_params=_cparams(("arbitrary",)),
    )(e_lo, e_hi, blk, nact, hs, w1, w3, w2, w1, w3, w2)


def _combine_kernel(dest_ref, x_ref, gate_ref, y_ref, o_ref, buf_ref, sem, *, rows):
    def start(r, c):
        _row_copy(y_ref, dest_ref[r], buf_ref, r, sem).start()
        return c

    def wait(r, c):
        _row_copy(y_ref, 0, buf_ref, 0, sem).wait()
        return c

    lax.fori_loop(0, rows, start, 0)
    lax.fori_loop(0, rows, wait, 0)
    o_ref[...] = x_ref[...] + gate_ref[0] * buf_ref[...]


def _combine(dest, x2, gate, y, seq):
    n, d = x2.shape
    rows = min(ROW_TILE, seq)
    per_seq = seq // rows
    return pl.pallas_call(
        functools.partial(_combine_kernel, rows=rows),
        grid=(n // rows,),
        in_specs=[pl.BlockSpec((rows,), lambda i: (i,), memory_space=pltpu.SMEM),
                  pl.BlockSpec((rows, d), lambda i: (i, 0)),
                  pl.BlockSpec((1, 1, d), lambda i: (i // per_seq, 0, 0)),
                  pl.BlockSpec(memory_space=pl.ANY)],
        out_specs=pl.BlockSpec((rows, d), lambda i: (i, 0)),
        out_shape=jax.ShapeDtypeStruct((n, d), F32),
        scratch_shapes=[pltpu.VMEM((rows, d), F32), pltpu.SemaphoreType.DMA(())],
        compiler_params=_cparams(("arbitrary",)),
    )(dest, x2, gate, y)


def _moe_layer(x_new, h2, route, counts, gate_f, w1, w3, w2):
    b, s, d = x_new.shape
    n = b * s
    tm = MOE_TILE
    n_tiles = n // tm + N_COMBOS
    combo = route[:, :, 0].reshape(n).astype(jnp.int32)
    rank = route[:, :, 1].reshape(n).astype(jnp.int32)
    cnt = counts[0, :N_COMBOS].astype(jnp.int32)
    tiles_per = (cnt + tm - 1) // tm
    tile_end = jnp.cumsum(tiles_per)
    row_off = (tile_end - tiles_per) * tm
    dest = jnp.take(row_off, combo) + rank
    nact = tile_end[-1:]
    blk = jnp.minimum(jnp.arange(n_tiles, dtype=jnp.int32), nact[0] - 1)
    tile_combo = jnp.searchsorted(tile_end, blk, side="right").astype(jnp.int32)
    e_lo = jnp.take(jnp.asarray(_COMBO_LO), tile_combo)
    e_hi = jnp.take(jnp.asarray(_COMBO_HI), tile_combo)

    hs = _dispatch(dest, h2.reshape(n, d + EXT_W), n_tiles * tm)
    ne = w1.shape[0] * w1.shape[1]
    y = _moe(hs, e_lo, e_hi, blk, nact.astype(jnp.int32),
             w1.reshape(ne, d, -1).astype(BF16), w3.reshape(ne, d, -1).astype(BF16),
             w2.reshape(ne, -1, d).astype(BF16))
    return _combine(dest, x_new.reshape(n, d), gate_f, y, s).reshape(b, s, d)


def kernel(x, c, positions, ada_w, ada_b, norm_mix, norm_ffn, da_w_qkv, da_q_norm, da_k_norm, da_lambda_q1, da_lambda_k1, da_lambda_q2, da_lambda_k2, da_subln, da_w_o, mla_w_in, mla_q_a_norm, mla_kv_a_norm, mla_w_uq, mla_w_ukv, mla_q_nope_norm, mla_q_pe_norm, mla_k_nope_norm, mla_k_pe_norm, mla_w_o, moe_w_group, moe_b_group, moe_w_router, moe_b_router, moe_w1, moe_w3, moe_w2):
    b, s, d = x.shape
    depth = ada_w.shape[0]
    mod_all = _modulation(c, ada_w, ada_b).reshape(depth, b, 6, d)
    posf = positions.astype(F32).reshape(b, s, 1)
    for i in range(depth):
        mod = mod_all[i]
        j = i // 2
        if i % 2 == 0:
            lam_init = 0.8 - 0.6 * math.exp(-0.3 * i)
            q, k, v = _da_pre(x, mod, posf, norm_mix[i], da_w_qkv[j], da_q_norm[j], da_k_norm[j])
            o = _da_attn(q, k, v, da_lambda_q1[j], da_lambda_k1[j], da_lambda_q2[j], da_lambda_k2[j],
                         da_subln[j], lam_init)
            w_o = da_w_o[j]
        else:
            q, k, v = _mla_pre(x, mod, posf, norm_mix[i], mla_w_in[j], mla_q_a_norm[j], mla_kv_a_norm[j],
                               mla_w_uq[j], mla_w_ukv[j], mla_q_nope_norm[j], mla_q_pe_norm[j],
                               mla_k_nope_norm[j], mla_k_pe_norm[j])
            o = _mla_attn(q, k, v)
            w_o = mla_w_o[j]
        x_new, h2, route, counts = _post(o, w_o, x, mod, norm_ffn[i], moe_w_group[i], moe_b_group[i],
                                         moe_w_router[i], moe_b_router[i])
        x = _moe_layer(x_new, h2, route, counts, mod[:, 5:6, :], moe_w1[i], moe_w3[i], moe_w2[i])
    return x
```

```python
import functools
import math

import numpy as np
import jax
import jax.numpy as jnp
from jax import lax
from jax.experimental import pallas as pl
from jax.experimental.pallas import tpu as pltpu

F32 = jnp.float32
BF16 = jnp.bfloat16

ROPE_THETA = 10000.0
EPS = 1e-6
NEG_INF = -1e30
LOG2E = math.log2(math.e)
DA_HEAD_DIM = 64
MLA_HEADS = 16
MLA_NOPE = 64
MLA_ROPE = 32
MLA_V = 64
MLA_Q_RANK = 384
MLA_KV_RANK = 256
N_GROUPS = 4
EXPERTS_PER_GROUP = 8
EXPERT_FF = 256

LANES = 128
VMEM_LIMIT = 56 * 1024 * 1024
TOK_TILE = 512
ATT_TILE = 256
MOE_TILE = 256
ROW_TILE = 1024
ROW_UNROLL = 8
MLA_IN_PAD = 768
EXT_W = 128

N_PAIRS = EXPERTS_PER_GROUP * (EXPERTS_PER_GROUP - 1) // 2
N_COMBOS = N_GROUPS * N_PAIRS


def _combo_tables():
    lo_t, hi_t = [], []
    for g in range(N_GROUPS):
        for lo in range(EXPERTS_PER_GROUP):
            for hi in range(lo + 1, EXPERTS_PER_GROUP):
                lo_t.append(g * EXPERTS_PER_GROUP + lo)
                hi_t.append(g * EXPERTS_PER_GROUP + hi)
    return np.asarray(lo_t, np.int32), np.asarray(hi_t, np.int32)


_COMBO_LO, _COMBO_HI = _combo_tables()


def _cparams(sem):
    return pltpu.CompilerParams(dimension_semantics=sem, vmem_limit_bytes=VMEM_LIMIT)


def _split_bf16(x):
    hi = x.astype(BF16)
    lo = (x - hi.astype(F32)).astype(BF16)
    return jnp.concatenate([hi, lo], axis=1)


def _seg_rsqrt_bcast(t, seg_ref, exp_ref):
    ss = jnp.dot((t * t).astype(BF16), seg_ref[...], preferred_element_type=F32)
    r = lax.rsqrt(ss + EPS)
    return jnp.dot(_split_bf16(r), exp_ref[...], preferred_element_type=F32)


def _rope(t, cosf, sinf, low_mask, half, width):
    partner = jnp.where(low_mask, pltpu.roll(t, width - half, 1), pltpu.roll(t, half, 1))
    return t * cosf + partner * sinf


def _store_vt(v_out, v):
    vt = v.T
    tk = v_out.shape[3]
    for j in range(v_out.shape[1]):
        v_out[0, j] = vt[:, j * tk:(j + 1) * tk].astype(BF16)


def _adaln_h(x, mod_ref, g_ref, shift_row, scale_row):
    ms = jnp.mean(x * x, axis=-1, keepdims=True)
    h = x * lax.rsqrt(ms + EPS) * g_ref[...]
    return h * (1.0 + mod_ref[0, scale_row:scale_row + 1, :]) + mod_ref[0, shift_row:shift_row + 1, :]


def _mod_kernel(c_ref, w_ref, b_ref, o_ref):
    c = c_ref[...]
    cond = (c * jax.nn.sigmoid(c)).astype(BF16)
    o_ref[0] = jnp.dot(cond, w_ref[0].astype(BF16), preferred_element_type=F32) + b_ref[0]


def _modulation(c, ada_w, ada_b):
    depth, d, n6 = ada_w.shape
    b = c.shape[0]
    tn = 1536
    return pl.pallas_call(
        _mod_kernel,
        grid=(depth, n6 // tn),
        in_specs=[pl.BlockSpec((b, d), lambda i, j: (0, 0)),
                  pl.BlockSpec((1, d, tn), lambda i, j: (i, 0, j)),
                  pl.BlockSpec((1, 1, tn), lambda i, j: (i, 0, j))],
        out_specs=pl.BlockSpec((1, b, tn), lambda i, j: (i, 0, j)),
        out_shape=jax.ShapeDtypeStruct((depth, b, n6), F32),
        compiler_params=_cparams(("arbitrary", "arbitrary")),
    )(c, ada_w, ada_b.reshape(depth, 1, n6))


def _da_pre_kernel(x_ref, mod_ref, pos_ref, g_ref, w_ref, qg_ref, kg_ref, invf_ref, sgn_ref,
                   seg_ref, exp_ref, q_out, k_out, v_out, *, d, scale):
    h = _adaln_h(x_ref[0], mod_ref, g_ref, 0, 1)
    qkv = jnp.dot(h.astype(BF16), w_ref[...], preferred_element_type=F32)
    q = qkv[:, :d]
    k = qkv[:, d:2 * d]
    _store_vt(v_out, qkv[:, 2 * d:])

    ang = pos_ref[0] * invf_ref[...]
    reps = d // LANES
    cosf = jnp.tile(jnp.cos(ang), (1, reps))
    sinf = jnp.tile(jnp.sin(ang) * sgn_ref[...], (1, reps))
    lane = lax.broadcasted_iota(jnp.int32, (1, d), 1)
    half = DA_HEAD_DIM // 2
    low = (lane & (DA_HEAD_DIM - 1)) < half

    qn = q * _seg_rsqrt_bcast(q, seg_ref, exp_ref) * qg_ref[...]
    q_out[0] = (_rope(qn, cosf, sinf, low, half, d) * scale).T.astype(BF16)
    kn = k * _seg_rsqrt_bcast(k, seg_ref, exp_ref) * kg_ref[...]
    k_out[0] = _rope(kn, cosf, sinf, low, half, d).astype(BF16)


def _da_pre(x, mod, posf, norm_g, w_qkv, q_norm, k_norm):
    b, s, d = x.shape
    t = min(TOK_TILE, s)
    dh = DA_HEAD_DIM
    nseg = d // dh
    lane = np.arange(LANES)
    invf = (ROPE_THETA ** (-np.arange(0, dh, 2, dtype=np.float32) / dh)).astype(np.float32)
    invf_row = invf[lane % (dh // 2)][None, :]
    sgn_row = np.where((lane % dh) < dh // 2, -1.0, 1.0).astype(np.float32)[None, :]
    seg = np.zeros((d, LANES), np.float32)
    seg[np.arange(d), np.arange(d) // dh] = 1.0 / dh
    expm = np.zeros((2 * LANES, d), np.float32)
    expm[np.arange(d) // dh, np.arange(d)] = 1.0
    expm[LANES + np.arange(d) // dh, np.arange(d)] = 1.0
    assert nseg <= LANES
    row = lambda bi, ti: (0, 0)
    tok = lambda bi, ti: (bi, ti, 0)
    tk = min(ATT_TILE, t)
    return pl.pallas_call(
        functools.partial(_da_pre_kernel, d=d, scale=dh ** -0.5 * LOG2E),
        grid=(b, s // t),
        in_specs=[pl.BlockSpec((1, t, d), tok),
                  pl.BlockSpec((1, 6, d), lambda bi, ti: (bi, 0, 0)),
                  pl.BlockSpec((1, t, 1), tok),
                  pl.BlockSpec((1, d), row),
                  pl.BlockSpec((d, 3 * d), row),
                  pl.BlockSpec((1, d), row),
                  pl.BlockSpec((1, d), row),
                  pl.BlockSpec((1, LANES), row),
                  pl.BlockSpec((1, LANES), row),
                  pl.BlockSpec((d, LANES), row),
                  pl.BlockSpec((2 * LANES, d), row)],
        out_specs=[pl.BlockSpec((1, d, t), lambda bi, ti: (bi, 0, ti)),
                   pl.BlockSpec((1, t, d), tok),
                   pl.BlockSpec((1, t // tk, d, tk), lambda bi, ti: (bi, ti, 0, 0))],
        out_shape=[jax.ShapeDtypeStruct((b, d, s), BF16),
                   jax.ShapeDtypeStruct((b, s, d), BF16),
                   jax.ShapeDtypeStruct((b, s // tk, d, tk), BF16)],
        compiler_params=_cparams(("arbitrary", "arbitrary")),
    )(x, mod, posf, norm_g.reshape(1, d), w_qkv.astype(BF16),
      jnp.tile(q_norm, nseg).reshape(1, d), jnp.tile(k_norm, nseg).reshape(1, d),
      jnp.asarray(invf_row), jnp.asarray(sgn_row), jnp.asarray(seg, BF16), jnp.asarray(expm, BF16))


def _softmax_block(s, vt_blk, g, m_ref, l_ref, acc_ref, mask):
    if mask is not None:
        s = jnp.where(mask, s, NEG_INF)
    m_prev = m_ref[g]
    m_new = jnp.maximum(m_prev, jnp.max(s, axis=0, keepdims=True))
    alpha = jnp.exp2(m_prev - m_new)
    p = jnp.exp2(s - m_new)
    l_ref[g] = alpha * l_ref[g] + jnp.sum(p, axis=0, keepdims=True)
    acc_ref[g] = alpha * acc_ref[g] + jnp.dot(vt_blk, p.astype(BF16), preferred_element_type=F32)
    m_ref[g] = m_new


def _causal_sweep(qi, scores, softmax, sa_ref, sb_ref, diag_mask):
    def fill(ref, kb):
        for g, s in enumerate(scores(kb)):
            ref[g] = s

    fill(sa_ref, 0)

    def body(j, c):
        kb = 2 * j
        fill(sb_ref, kb + 1)
        softmax(sa_ref, kb, None)
        fill(sa_ref, kb + 2)
        softmax(sb_ref, kb + 1, None)
        return c

    lax.fori_loop(0, lax.shift_right_logical(qi, 1), body, 0)
    odd = (qi & 1) == 1

    @pl.when(jnp.logical_not(odd))
    def _():
        softmax(sa_ref, qi, diag_mask)

    @pl.when(odd)
    def _():
        fill(sb_ref, qi)
        softmax(sa_ref, qi - 1, None)
        softmax(sb_ref, qi, diag_mask)


def _init_stats(m_ref, l_ref, acc_ref):
    m_ref[...] = jnp.full(m_ref.shape, NEG_INF, F32)
    l_ref[...] = jnp.zeros(l_ref.shape, F32)
    acc_ref[...] = jnp.zeros(acc_ref.shape, F32)


def _causal_mask_t(tk, cols, tq):
    c = lax.broadcasted_iota(jnp.int32, (tk, cols), 0)
    r = lax.broadcasted_iota(jnp.int32, (tk, cols), 1)
    r = jnp.where(r >= tq, r - tq, r)
    return c <= r


def _da_attn_kernel(lq1_ref, lk1_ref, lq2_ref, lk2_ref, sub_ref, qt_ref, k_ref, vt_ref, o_ref,
                    qs_ref, m_ref, l_ref, acc_ref, sa_ref, sb_ref, *, tq, heads, lam_init):
    qi = pl.program_id(2)
    sub = lax.broadcasted_iota(jnp.int32, (LANES, tq), 0)
    for g in range(heads):
        qt = qt_ref[0, g * LANES:(g + 1) * LANES, :]
        zero = jnp.zeros_like(qt)
        qs_ref[g] = jnp.concatenate([jnp.where(sub < DA_HEAD_DIM, qt, zero),
                                     jnp.where(sub >= DA_HEAD_DIM, qt, zero)], axis=1)
    _init_stats(m_ref, l_ref, acc_ref)

    def scores(kb):
        off = pl.multiple_of(kb * tq, tq)
        return [jnp.dot(k_ref[0, pl.ds(off, tq), g * LANES:(g + 1) * LANES], qs_ref[g],
                        preferred_element_type=F32) for g in range(heads)]

    def softmax(s, kb, mask):
        for g in range(heads):
            _softmax_block(s[g], vt_ref[0, kb, g * LANES:(g + 1) * LANES, :], g, m_ref, l_ref, acc_ref, mask)

    _causal_sweep(qi, scores, softmax, sa_ref, sb_ref, _causal_mask_t(tq, 2 * tq, tq))

    lam = (jnp.exp(jnp.sum(lq1_ref[...] * lk1_ref[...], axis=1, keepdims=True))
           - jnp.exp(jnp.sum(lq2_ref[...] * lk2_ref[...], axis=1, keepdims=True)) + lam_init)
    for g in range(heads):
        ot = acc_ref[g] / l_ref[g]
        dd = (ot[:, :tq] - lam * ot[:, tq:]).T
        ms = jnp.mean(dd * dd, axis=-1, keepdims=True)
        o_ref[0, :, g * LANES:(g + 1) * LANES] = (
            dd * lax.rsqrt(ms + EPS) * sub_ref[...] * (1.0 - lam_init)).astype(BF16)


def _da_attn(qt, k, vt, lq1, lk1, lq2, lk2, subln, lam_init):
    b, s, d = k.shape
    heads = 2
    wb = heads * LANES
    tq = vt.shape[3]
    vec = lambda bi, hi, qi: (0, 0)
    return pl.pallas_call(
        functools.partial(_da_attn_kernel, tq=tq, heads=heads, lam_init=lam_init),
        grid=(b, d // wb, s // tq),
        in_specs=[pl.BlockSpec((1, DA_HEAD_DIM), vec)] * 4 + [
            pl.BlockSpec((1, LANES), vec),
            pl.BlockSpec((1, wb, tq), lambda bi, hi, qi: (bi, hi, qi)),
            pl.BlockSpec((1, s, wb), lambda bi, hi, qi: (bi, 0, hi)),
            pl.BlockSpec((1, s // tq, wb, tq), lambda bi, hi, qi: (bi, 0, hi, 0))],
        out_specs=pl.BlockSpec((1, tq, wb), lambda bi, hi, qi: (bi, qi, hi)),
        out_shape=jax.ShapeDtypeStruct((b, s, d), BF16),
        scratch_shapes=[pltpu.VMEM((heads, LANES, 2 * tq), BF16),
                        pltpu.VMEM((heads, 1, 2 * tq), F32),
                        pltpu.VMEM((heads, 1, 2 * tq), F32),
                        pltpu.VMEM((heads, LANES, 2 * tq), F32),
                        pltpu.VMEM((heads, tq, 2 * tq), F32),
                        pltpu.VMEM((heads, tq, 2 * tq), F32)],
        compiler_params=_cparams(("arbitrary", "arbitrary", "arbitrary")),
    )(lq1.reshape(1, -1), lk1.reshape(1, -1), lq2.reshape(1, -1), lk2.reshape(1, -1),
      subln.reshape(1, -1), qt, k, vt)


def _mla_attn_kernel(qt_ref, k_ref, vt_ref, o_ref, m_ref, l_ref, acc_ref, sa_ref, sb_ref, *, tq, heads):
    qi = pl.program_id(2)
    _init_stats(m_ref, l_ref, acc_ref)

    def scores(kb):
        off = pl.multiple_of(kb * tq, tq)
        return [jnp.dot(k_ref[0, pl.ds(off, tq), g * LANES:(g + 1) * LANES],
                        qt_ref[0, g * LANES:(g + 1) * LANES, :],
                        preferred_element_type=F32) for g in range(heads)]

    def softmax(s, kb, mask):
        for g in range(heads):
            _softmax_block(s[g], vt_ref[0, kb, g * MLA_V:(g + 1) * MLA_V, :], g, m_ref, l_ref, acc_ref, mask)

    _causal_sweep(qi, scores, softmax, sa_ref, sb_ref, _causal_mask_t(tq, tq, tq))
    for p in range(heads // 2):
        ot = jnp.concatenate([acc_ref[2 * p] / l_ref[2 * p], acc_ref[2 * p + 1] / l_ref[2 * p + 1]], axis=0)
        o_ref[0, :, p * LANES:(p + 1) * LANES] = ot.T.astype(BF16)


def _mla_attn(qt, k, vt):
    b, s, dk = k.shape
    heads = 4
    tq = vt.shape[3]
    return pl.pallas_call(
        functools.partial(_mla_attn_kernel, tq=tq, heads=heads),
        grid=(b, dk // (heads * LANES), s // tq),
        in_specs=[pl.BlockSpec((1, heads * LANES, tq), lambda bi, hi, qi: (bi, hi, qi)),
                  pl.BlockSpec((1, s, heads * LANES), lambda bi, hi, qi: (bi, 0, hi)),
                  pl.BlockSpec((1, s // tq, heads * MLA_V, tq), lambda bi, hi, qi: (bi, 0, hi, 0))],
        out_specs=pl.BlockSpec((1, tq, heads * MLA_V), lambda bi, hi, qi: (bi, qi, hi)),
        out_shape=jax.ShapeDtypeStruct((b, s, dk // LANES * MLA_V), BF16),
        scratch_shapes=[pltpu.VMEM((heads, 1, tq), F32),
                        pltpu.VMEM((heads, 1, tq), F32),
                        pltpu.VMEM((heads, MLA_V, tq), F32),
                        pltpu.VMEM((heads, tq, tq), F32),
                        pltpu.VMEM((heads, tq, tq), F32)],
        compiler_params=_cparams(("arbitrary", "arbitrary", "arbitrary")),
    )(qt, k, vt)


def _mla_pre_kernel(x_ref, mod_ref, pos_ref, g_ref, win_ref, qag_ref, kvag_ref, kpeg_ref,
                    wuq_ref, wuk_ref, wuv_ref, qg_ref, kg_ref, invf_ref, sgn_ref,
                    segq_ref, expq_ref, segk_ref, expk_ref, place_ref,
                    q_out, k_out, v_out, *, scale):
    h = _adaln_h(x_ref[0], mod_ref, g_ref, 0, 1)
    lat = jnp.dot(h.astype(BF16), win_ref[...], preferred_element_type=F32)
    cq = lat[:, :MLA_Q_RANK]
    ckv = lat[:, MLA_Q_RANK:MLA_Q_RANK + MLA_KV_RANK]
    kpe = lat[:, MLA_Q_RANK + MLA_KV_RANK:]
    cqn = (cq * lax.rsqrt(jnp.mean(cq * cq, axis=-1, keepdims=True) + EPS) * qag_ref[...]).astype(BF16)
    ckvn = (ckv * lax.rsqrt(jnp.mean(ckv * ckv, axis=-1, keepdims=True) + EPS) * kvag_ref[...]).astype(BF16)
    kpen = kpe * lax.rsqrt(jnp.sum(kpe * kpe, axis=-1, keepdims=True) * (1.0 / MLA_ROPE) + EPS) * kpeg_ref[...]

    q = jnp.dot(cqn, wuq_ref[...], preferred_element_type=F32)
    kn = jnp.dot(ckvn, wuk_ref[...], preferred_element_type=F32)
    _store_vt(v_out, jnp.dot(ckvn, wuv_ref[...], preferred_element_type=F32))

    w = q.shape[1]
    reps = w // LANES
    ang = pos_ref[0] * invf_ref[...]
    cosf = jnp.tile(jnp.cos(ang), (1, reps))
    sinf = jnp.tile(jnp.sin(ang) * sgn_ref[...], (1, reps))
    lane = lax.broadcasted_iota(jnp.int32, (1, w), 1)
    half = MLA_ROPE // 2
    low = (lane & (MLA_ROPE - 1)) < half

    qn = q * _seg_rsqrt_bcast(q, segq_ref, expq_ref) * qg_ref[...]
    q_out[0] = (_rope(qn, cosf, sinf, low, half, w) * scale).T.astype(BF16)
    knn = kn * _seg_rsqrt_bcast(kn, segk_ref, expk_ref) * kg_ref[...]
    kfull = knn + jnp.dot(_split_bf16(kpen), place_ref[...], preferred_element_type=F32)
    k_out[0] = _rope(kfull, cosf, sinf, low, half, w).astype(BF16)


def _mla_pre(x, mod, posf, norm_g, w_in, q_a_norm, kv_a_norm, w_uq, w_ukv,
             q_nope_norm, q_pe_norm, k_nope_norm, k_pe_norm):
    b, s, d = x.shape
    t = min(TOK_TILE, s)
    nh, nope, rope, vd = MLA_HEADS, MLA_NOPE, MLA_ROPE, MLA_V
    w = nh * LANES
    in_w = MLA_Q_RANK + MLA_KV_RANK + rope
    w_in_p = jnp.pad(w_in, ((0, 0), (0, MLA_IN_PAD - in_w))).astype(BF16)
    wuq = w_uq.reshape(MLA_Q_RANK, nh, nope + rope)
    wuq_p = jnp.pad(wuq, ((0, 0), (0, 0), (0, LANES - nope - rope))).reshape(MLA_Q_RANK, w).astype(BF16)
    wukv = w_ukv.reshape(MLA_KV_RANK, nh, nope + vd)
    wuk_p = jnp.pad(wukv[:, :, :nope], ((0, 0), (0, 0), (0, LANES - nope))).reshape(MLA_KV_RANK, w).astype(BF16)
    wuv = wukv[:, :, nope:].reshape(MLA_KV_RANK, nh * vd).astype(BF16)
    zpad = jnp.zeros((LANES - nope - rope,), F32)
    qg = jnp.tile(jnp.concatenate([q_nope_norm, q_pe_norm, zpad]), nh).reshape(1, w)
    kg = jnp.tile(jnp.concatenate([k_nope_norm, jnp.zeros((LANES - nope,), F32)]), nh).reshape(1, w)
    kpeg = jnp.concatenate([k_pe_norm, jnp.zeros((LANES - rope,), F32)]).reshape(1, LANES)

    lane = np.arange(LANES)
    in_rope = (lane >= nope) & (lane < nope + rope)
    invf = (ROPE_THETA ** (-np.arange(0, rope, 2, dtype=np.float32) / rope)).astype(np.float32)
    invf_row = np.where(in_rope, invf[(lane - nope) % (rope // 2)], 0.0).astype(np.float32)[None, :]
    sgn_row = np.where(in_rope, np.where((lane - nope) < rope // 2, -1.0, 1.0), 0.0).astype(np.float32)[None, :]

    col = np.arange(w)
    hd, off = col // LANES, col % LANES
    segq = np.zeros((w, LANES), np.float32)
    expq = np.zeros((2 * LANES, w), np.float32)
    is_nope, is_pe = off < nope, (off >= nope) & (off < nope + rope)
    segq[col[is_nope], 2 * hd[is_nope]] = 1.0 / nope
    segq[col[is_pe], 2 * hd[is_pe] + 1] = 1.0 / rope
    for base in (0, LANES):
        expq[base + 2 * hd[is_nope], col[is_nope]] = 1.0
        expq[base + 2 * hd[is_pe] + 1, col[is_pe]] = 1.0
    segk = np.zeros((w, LANES), np.float32)
    expk = np.zeros((2 * LANES, w), np.float32)
    segk[col[is_nope], hd[is_nope]] = 1.0 / nope
    place = np.zeros((2 * LANES, w), np.float32)
    for base in (0, LANES):
        expk[base + hd[is_nope], col[is_nope]] = 1.0
        place[base + off[is_pe] - nope, col[is_pe]] = 1.0
    assert 2 * nh <= LANES

    row = lambda bi, ti: (0, 0)
    tok = lambda bi, ti: (bi, ti, 0)
    full = lambda a: pl.BlockSpec(a.shape, row)
    consts = [norm_g.reshape(1, d), w_in_p, q_a_norm.reshape(1, -1), kv_a_norm.reshape(1, -1), kpeg,
              wuq_p, wuk_p, wuv, qg, kg, jnp.asarray(invf_row), jnp.asarray(sgn_row),
              jnp.asarray(segq, BF16), jnp.asarray(expq, BF16), jnp.asarray(segk, BF16),
              jnp.asarray(expk, BF16), jnp.asarray(place, BF16)]
    tk = min(ATT_TILE, t)
    return pl.pallas_call(
        functools.partial(_mla_pre_kernel, scale=(nope + rope) ** -0.5 * LOG2E),
        grid=(b, s // t),
        in_specs=[pl.BlockSpec((1, t, d), tok),
                  pl.BlockSpec((1, 6, d), lambda bi, ti: (bi, 0, 0)),
                  pl.BlockSpec((1, t, 1), tok)] + [full(a) for a in consts],
        out_specs=[pl.BlockSpec((1, w, t), lambda bi, ti: (bi, 0, ti)),
                   pl.BlockSpec((1, t, w), tok),
                   pl.BlockSpec((1, t // tk, nh * vd, tk), lambda bi, ti: (bi, ti, 0, 0))],
        out_shape=[jax.ShapeDtypeStruct((b, w, s), BF16),
                   jax.ShapeDtypeStruct((b, s, w), BF16),
                   jax.ShapeDtypeStruct((b, s // tk, nh * vd, tk), BF16)],
        compiler_params=_cparams(("arbitrary", "arbitrary")),
    )(x, mod, posf, *consts)


def _post_kernel(o_ref, wo_ref, x_ref, mod_ref, g_ref, wr_ref, bsel_ref,
                 x_out, h_out, route_out, cnt_out, run_ref, *, d):
    first = (pl.program_id(0) == 0) & (pl.program_id(1) == 0)

    @pl.when(first)
    def _():
        run_ref[...] = jnp.zeros_like(run_ref)

    y = jnp.dot(o_ref[0], wo_ref[...], preferred_element_type=F32)
    x = x_ref[0] + mod_ref[0, 2:3, :] * y
    x_out[0] = x
    h = _adaln_h(x, mod_ref, g_ref, 3, 4)
    t = h.shape[0]

    hh = h.astype(BF16)
    hl = (h - hh.astype(F32)).astype(BF16)
    logits = (jnp.dot(hh, wr_ref[0], preferred_element_type=F32)
              + jnp.dot(hl, wr_ref[0], preferred_element_type=F32)
              + jnp.dot(hh, wr_ref[1], preferred_element_type=F32))

    lane_i = lax.broadcasted_iota(jnp.int32, (t, LANES), 1)
    lane = lane_i.astype(F32)
    big = float(LANES)
    ng, ne = N_GROUPS, EXPERTS_PER_GROUP

    def first_argmax(val):
        mx = jnp.max(val, axis=1, keepdims=True)
        return jnp.min(jnp.where(val == mx, lane, big), axis=1, keepdims=True)

    def pick(val, idx):
        return jnp.sum(jnp.where(lane == idx, val, 0.0), axis=1, keepdims=True)

    gmask = lane_i < ng
    gl = jnp.where(gmask, logits, NEG_INF)
    ge = jnp.exp(gl - jnp.max(gl, axis=1, keepdims=True))
    gprob = ge / jnp.sum(ge, axis=1, keepdims=True)
    gidx = first_argmax(jnp.where(gmask, gprob + bsel_ref[...], NEG_INF))
    ggate = pick(gprob, gidx)

    base = ng + ne * gidx
    emask = (lane >= base) & (lane < base + ne)
    el = jnp.where(emask, logits, NEG_INF)
    ee = jnp.exp(el - jnp.max(el, axis=1, keepdims=True))
    eprob = ee / jnp.sum(ee, axis=1, keepdims=True)
    sel = jnp.where(emask, eprob + bsel_ref[...], NEG_INF)
    i1 = first_argmax(sel)
    i2 = first_argmax(jnp.where(lane == i1, NEG_INF, sel))
    p1 = pick(eprob, i1)
    p2 = pick(eprob, i2)
    psum = p1 + p2
    w1 = p1 / psum * ggate
    w2 = p2 / psum * ggate
    e1 = i1 - base
    e2 = i2 - base
    swap = e2 < e1
    lo = jnp.where(swap, e2, e1)
    hi = jnp.where(swap, e1, e2)
    wa = jnp.where(swap, w2, w1)
    wb = jnp.where(swap, w1, w2)
    combo = gidx * float(N_PAIRS) + lo * (2.0 * ne - 1.0 - lo) * 0.5 + (hi - lo - 1.0)

    h_out[0, :, :d] = h
    h_out[0, :, d:] = jnp.where(lane_i == 0, wa, jnp.where(lane_i == 1, wb, 0.0))

    onehot = lane == combo
    r_i = lax.broadcasted_iota(jnp.int32, (t, t), 0)
    c_i = lax.broadcasted_iota(jnp.int32, (t, t), 1)
    tri = jnp.where(c_i < r_i, 1.0, 0.0).astype(BF16)
    oh = jnp.where(onehot, 1.0, 0.0)
    before = jnp.dot(tri, oh.astype(BF16), preferred_element_type=F32) + run_ref[...]
    rank = jnp.sum(jnp.where(onehot, before, 0.0), axis=1, keepdims=True)
    run = run_ref[...] + jnp.sum(oh, axis=0, keepdims=True)
    run_ref[...] = run
    cnt_out[...] = run
    route_out[0] = jnp.where(lane_i == 0, combo, jnp.where(lane_i == 1, rank, 0.0))


def _post(o, w_o, x, mod, norm_g, w_group, b_group, w_router, b_router):
    b, s, d = x.shape
    t = min(TOK_TILE, s)
    do = o.shape[2]
    ng, ne = N_GROUPS, EXPERTS_PER_GROUP
    wr = jnp.pad(jnp.concatenate([w_group, w_router], axis=1), ((0, 0), (0, LANES - ng - ng * ne)))
    wr_hi = wr.astype(BF16)
    wr_lo = (wr - wr_hi.astype(F32)).astype(BF16)
    wr2 = jnp.stack([wr_hi, wr_lo])
    bsel = jnp.pad(jnp.concatenate([b_group, b_router.reshape(-1)]), (0, LANES - ng - ng * ne)).reshape(1, LANES)
    row = lambda bi, ti: (0, 0)
    tok = lambda bi, ti: (bi, ti, 0)
    return pl.pallas_call(
        functools.partial(_post_kernel, d=d),
        grid=(b, s // t),
        in_specs=[pl.BlockSpec((1, t, do), tok),
                  pl.BlockSpec((do, d), row),
                  pl.BlockSpec((1, t, d), tok),
                  pl.BlockSpec((1, 6, d), lambda bi, ti: (bi, 0, 0)),
                  pl.BlockSpec((1, d), row),
                  pl.BlockSpec((2, d, LANES), lambda bi, ti: (0, 0, 0)),
                  pl.BlockSpec((1, LANES), row)],
        out_specs=[pl.BlockSpec((1, t, d), tok),
                   pl.BlockSpec((1, t, d + EXT_W), tok),
                   pl.BlockSpec((1, t, LANES), tok),
                   pl.BlockSpec((1, LANES), row)],
        out_shape=[jax.ShapeDtypeStruct((b, s, d), F32),
                   jax.ShapeDtypeStruct((b, s, d + EXT_W), F32),
                   jax.ShapeDtypeStruct((b, s, LANES), F32),
                   jax.ShapeDtypeStruct((1, LANES), F32)],
        scratch_shapes=[pltpu.VMEM((1, LANES), F32)],
        compiler_params=_cparams(("arbitrary", "arbitrary")),
    )(o, w_o.astype(BF16), x, mod, norm_g.reshape(1, d), wr2, bsel)


def _row_copy(src_ref, src_row, dst_ref, dst_row, sem):
    return pltpu.make_async_copy(src_ref.at[pl.ds(src_row, 1), :], dst_ref.at[pl.ds(dst_row, 1), :], sem)


def _row_copies(rows, make_copy):
    def start(i, c):
        for u in range(ROW_UNROLL):
            make_copy(i * ROW_UNROLL + u).start(priority=u % 2)
        return c

    def wait(i, c):
        for u in range(ROW_UNROLL):
            make_copy(0).wait()
        return c

    lax.fori_loop(0, rows // ROW_UNROLL, start, 0)
    lax.fori_loop(0, rows // ROW_UNROLL, wait, 0)


def _dispatch_kernel(dest_ref, h_ref, init_ref, hs_ref, sem, *, rows):
    del init_ref
    _row_copies(rows, lambda r: _row_copy(h_ref, r, hs_ref, dest_ref[r], sem))


def _dispatch(dest, h2, n_rows):
    n, wd = h2.shape
    rows = min(ROW_TILE, n)
    return pl.pallas_call(
        functools.partial(_dispatch_kernel, rows=rows),
        grid=(n // rows,),
        in_specs=[pl.BlockSpec((rows,), lambda i: (i,), memory_space=pltpu.SMEM),
                  pl.BlockSpec((rows, wd), lambda i: (i, 0)),
                  pl.BlockSpec(memory_space=pl.ANY)],
        out_specs=pl.BlockSpec(memory_space=pl.ANY),
        out_shape=jax.ShapeDtypeStruct((n_rows, wd), F32),
        scratch_shapes=[pltpu.SemaphoreType.DMA(())],
        input_output_aliases={2: 0},
        compiler_params=_cparams(("arbitrary",)),
    )(dest, h2, jnp.zeros((n_rows, wd), F32))


def _moe_kernel(elo_ref, ehi_ref, blk_ref, nact_ref, hs_ref, w1a_ref, w3a_ref, w2a_ref,
                w1b_ref, w3b_ref, w2b_ref, y_ref, *, d):
    del elo_ref, ehi_ref, blk_ref
    active = pl.program_id(0) < nact_ref[0]

    @pl.when(jnp.logical_not(active))
    def _():
        y_ref[...] = jnp.zeros_like(y_ref)

    @pl.when(active)
    def _():
        x = hs_ref[:, :d].astype(BF16)
        wa = hs_ref[:, d:d + 1]
        wb = hs_ref[:, d + 1:d + 2]

        def expert(w1_ref, w3_ref, wgt):
            a = jnp.dot(x, w1_ref[0], preferred_element_type=F32)
            g = jnp.dot(x, w3_ref[0], preferred_element_type=F32)
            return (a * jax.nn.sigmoid(a) * g * wgt).astype(BF16)

        y_ref[...] = (jnp.dot(expert(w1a_ref, w3a_ref, wa), w2a_ref[0], preferred_element_type=F32)
                      + jnp.dot(expert(w1b_ref, w3b_ref, wb), w2b_ref[0], preferred_element_type=F32))


def _moe(hs, e_lo, e_hi, blk, nact, w1, w3, w2):
    n_rows, wd = hs.shape
    d = wd - EXT_W
    ff = w1.shape[2]
    tm = MOE_TILE
    n_tiles = n_rows // tm
    wspec = lambda shape, which: pl.BlockSpec(shape, (lambda j, lo, hi, bk, na: (lo[j], 0, 0)) if which == 0
                                              else (lambda j, lo, hi, bk, na: (hi[j], 0, 0)))
    grid_spec = pltpu.PrefetchScalarGridSpec(
        num_scalar_prefetch=4,
        grid=(n_tiles,),
        in_specs=[pl.BlockSpec((tm, wd), lambda j, lo, hi, bk, na: (bk[j], 0)),
                  wspec((1, d, ff), 0), wspec((1, d, ff), 0), wspec((1, ff, d), 0),
                  wspec((1, d, ff), 1), wspec((1, d, ff), 1), wspec((1, ff, d), 1)],
        out_specs=pl.BlockSpec((tm, d), lambda j, lo, hi, bk, na: (j, 0)),
    )
    return pl.pallas_call(
        functools.partial(_moe_kernel, d=d),
        grid_spec=grid_spec,
        out_shape=jax.ShapeDtypeStruct((n_rows, d), F32),
        compiler_params=_cparams(("arbitrary",)),
    )(e_lo, e_hi, blk, nact, hs, w1, w3, w2, w1, w3, w2)


def _combine_kernel(dest_ref, x_ref, gate_ref, y_ref, o_ref, buf_ref, sem, *, rows):
    _row_copies(rows, lambda r: _row_copy(y_ref, dest_ref[r], buf_ref, r, sem))
    o_ref[...] = x_ref[...] + gate_ref[0] * buf_ref[...]


def _combine(dest, x2, gate, y, seq):
    n, d = x2.shape
    rows = min(ROW_TILE, seq)
    per_seq = seq // rows
    return pl.pallas_call(
        functools.partial(_combine_kernel, rows=rows),
        grid=(n // rows,),
        in_specs=[pl.BlockSpec((rows,), lambda i: (i,), memory_space=pltpu.SMEM),
                  pl.BlockSpec((rows, d), lambda i: (i, 0)),
                  pl.BlockSpec((1, 1, d), lambda i: (i // per_seq, 0, 0)),
                  pl.BlockSpec(memory_space=pl.ANY)],
        out_specs=pl.BlockSpec((rows, d), lambda i: (i, 0)),
        out_shape=jax.ShapeDtypeStruct((n, d), F32),
        scratch_shapes=[pltpu.VMEM((rows, d), F32), pltpu.SemaphoreType.DMA(())],
        compiler_params=_cparams(("arbitrary",)),
    )(dest, x2, gate, y)


def _moe_layer(x_new, h2, route, counts, gate_f, w1, w3, w2):
    b, s, d = x_new.shape
    n = b * s
    tm = MOE_TILE
    n_tiles = n // tm + N_COMBOS
    combo = route[:, :, 0].reshape(n).astype(jnp.int32)
    rank = route[:, :, 1].reshape(n).astype(jnp.int32)
    cnt = counts[0, :N_COMBOS].astype(jnp.int32)
    tiles_per = (cnt + tm - 1) // tm
    tile_end = jnp.cumsum(tiles_per)
    row_off = (tile_end - tiles_per) * tm
    ids = jnp.arange(N_COMBOS, dtype=jnp.int32)

    def lookup(table, idx):
        return jnp.sum(jnp.where(idx[:, None] == ids[None, :], table[None, :], 0), axis=1)

    dest = lookup(row_off, combo) + rank
    nact = tile_end[-1:]
    blk = jnp.minimum(jnp.arange(n_tiles, dtype=jnp.int32), nact[0] - 1)
    tile_combo = jnp.sum((tile_end[None, :] <= blk[:, None]).astype(jnp.int32), axis=1)
    e_lo = lookup(jnp.asarray(_COMBO_LO), tile_combo)
    e_hi = lookup(jnp.asarray(_COMBO_HI), tile_combo)

    hs = _dispatch(dest, h2.reshape(n, d + EXT_W), n_tiles * tm)
    ne = w1.shape[0] * w1.shape[1]
    y = _moe(hs, e_lo, e_hi, blk, nact.astype(jnp.int32),
             w1.reshape(ne, d, -1).astype(BF16), w3.reshape(ne, d, -1).astype(BF16),
             w2.reshape(ne, -1, d).astype(BF16))
    return _combine(dest, x_new.reshape(n, d), gate_f, y, s).reshape(b, s, d)


def kernel(x, c, positions, ada_w, ada_b, norm_mix, norm_ffn, da_w_qkv, da_q_norm, da_k_norm, da_lambda_q1, da_lambda_k1, da_lambda_q2, da_lambda_k2, da_subln, da_w_o, mla_w_in, mla_q_a_norm, mla_kv_a_norm, mla_w_uq, mla_w_ukv, mla_q_nope_norm, mla_q_pe_norm, mla_k_nope_norm, mla_k_pe_norm, mla_w_o, moe_w_group, moe_b_group, moe_w_router, moe_b_router, moe_w1, moe_w3, moe_w2):
    b, s, d = x.shape
    depth = ada_w.shape[0]
    mod_all = _modulation(c, ada_w, ada_b).reshape(depth, b, 6, d)
    posf = positions.astype(F32).reshape(b, s, 1)
    for i in range(depth):
        mod = mod_all[i]
        j = i // 2
        if i % 2 == 0:
            lam_init = 0.8 - 0.6 * math.exp(-0.3 * i)
            q, k, v = _da_pre(x, mod, posf, norm_mix[i], da_w_qkv[j], da_q_norm[j], da_k_norm[j])
            o = _da_attn(q, k, v, da_lambda_q1[j], da_lambda_k1[j], da_lambda_q2[j], da_lambda_k2[j],
                         da_subln[j], lam_init)
            w_o = da_w_o[j]
        else:
            q, k, v = _mla_pre(x, mod, posf, norm_mix[i], mla_w_in[j], mla_q_a_norm[j], mla_kv_a_norm[j],
                               mla_w_uq[j], mla_w_ukv[j], mla_q_nope_norm[j], mla_q_pe_norm[j],
                               mla_k_nope_norm[j], mla_k_pe_norm[j])
            o = _mla_attn(q, k, v)
            w_o = mla_w_o[j]
        x_new, h2, route, counts = _post(o, w_o, x, mod, norm_ffn[i], moe_w_group[i], moe_b_group[i],
                                         moe_w_router[i], moe_b_router[i])
        x = _moe_layer(x_new, h2, route, counts, mod[:, 5:6, :], moe_w1[i], moe_w3[i], moe_w2[i])
    return x
```

```python
import functools
import math

import numpy as np
import jax
import jax.numpy as jnp
from jax import lax
from jax.experimental import pallas as pl
from jax.experimental.pallas import tpu as pltpu

F32 = jnp.float32
BF16 = jnp.bfloat16

ROPE_THETA = 10000.0
EPS = 1e-6
NEG_INF = -1e30
LOG2E = math.log2(math.e)
DA_HEAD_DIM = 64
MLA_HEADS = 16
MLA_NOPE = 64
MLA_ROPE = 32
MLA_V = 64
MLA_Q_RANK = 384
MLA_KV_RANK = 256
N_GROUPS = 4
EXPERTS_PER_GROUP = 8
EXPERT_FF = 256

LANES = 128
VMEM_LIMIT = 56 * 1024 * 1024
TOK_TILE = 512
ATT_TILE = 256
MOE_TILE = 256
ROW_TILE = 1024
ROW_UNROLL = 8
MLA_IN_PAD = 768
SUBLANES = 8

N_PAIRS = EXPERTS_PER_GROUP * (EXPERTS_PER_GROUP - 1) // 2
N_COMBOS = N_GROUPS * N_PAIRS


def _combo_tables():
    lo_t, hi_t = [], []
    for g in range(N_GROUPS):
        for lo in range(EXPERTS_PER_GROUP):
            for hi in range(lo + 1, EXPERTS_PER_GROUP):
                lo_t.append(g * EXPERTS_PER_GROUP + lo)
                hi_t.append(g * EXPERTS_PER_GROUP + hi)
    return np.asarray(lo_t, np.int32), np.asarray(hi_t, np.int32)


_COMBO_LO, _COMBO_HI = _combo_tables()


def _cparams(sem):
    return pltpu.CompilerParams(dimension_semantics=sem, vmem_limit_bytes=VMEM_LIMIT)


def _split_bf16(x):
    hi = x.astype(BF16)
    lo = (x - hi.astype(F32)).astype(BF16)
    return jnp.concatenate([hi, lo], axis=1)


def _seg_rsqrt_bcast(t, seg_ref, exp_ref):
    ss = jnp.dot((t * t).astype(BF16), seg_ref[...], preferred_element_type=F32)
    r = lax.rsqrt(ss + EPS)
    return jnp.dot(_split_bf16(r), exp_ref[...], preferred_element_type=F32)


def _rope(t, cosf, sinf, low_mask, half, width):
    partner = jnp.where(low_mask, pltpu.roll(t, width - half, 1), pltpu.roll(t, half, 1))
    return t * cosf + partner * sinf


def _store_vt(v_out, v):
    vt = v.T
    tk = v_out.shape[3]
    for j in range(v_out.shape[1]):
        v_out[0, j] = vt[:, j * tk:(j + 1) * tk].astype(BF16)


def _adaln_h(x, mod_ref, g_ref, shift_row, scale_row):
    ms = jnp.mean(x * x, axis=-1, keepdims=True)
    h = x * lax.rsqrt(ms + EPS) * g_ref[...]
    return h * (1.0 + mod_ref[0, scale_row:scale_row + 1, :]) + mod_ref[0, shift_row:shift_row + 1, :]


def _mod_kernel(c_ref, w_ref, b_ref, o_ref):
    c = c_ref[...]
    cond = (c * jax.nn.sigmoid(c)).astype(BF16)
    o_ref[0] = jnp.dot(cond, w_ref[0].astype(BF16), preferred_element_type=F32) + b_ref[0]


def _modulation(c, ada_w, ada_b):
    depth, d, n6 = ada_w.shape
    b = c.shape[0]
    tn = 1536
    return pl.pallas_call(
        _mod_kernel,
        grid=(depth, n6 // tn),
        in_specs=[pl.BlockSpec((b, d), lambda i, j: (0, 0)),
                  pl.BlockSpec((1, d, tn), lambda i, j: (i, 0, j)),
                  pl.BlockSpec((1, 1, tn), lambda i, j: (i, 0, j))],
        out_specs=pl.BlockSpec((1, b, tn), lambda i, j: (i, 0, j)),
        out_shape=jax.ShapeDtypeStruct((depth, b, n6), F32),
        compiler_params=_cparams(("arbitrary", "arbitrary")),
    )(c, ada_w, ada_b.reshape(depth, 1, n6))


def _da_pre_kernel(x_ref, mod_ref, pos_ref, g_ref, w_ref, qg_ref, kg_ref, invf_ref, sgn_ref,
                   seg_ref, exp_ref, q_out, k_out, v_out, *, d, scale):
    h = _adaln_h(x_ref[0], mod_ref, g_ref, 0, 1)
    qkv = jnp.dot(h.astype(BF16), w_ref[...], preferred_element_type=F32)
    q = qkv[:, :d]
    k = qkv[:, d:2 * d]
    _store_vt(v_out, qkv[:, 2 * d:])

    ang = pos_ref[0] * invf_ref[...]
    reps = d // LANES
    cosf = jnp.tile(jnp.cos(ang), (1, reps))
    sinf = jnp.tile(jnp.sin(ang) * sgn_ref[...], (1, reps))
    lane = lax.broadcasted_iota(jnp.int32, (1, d), 1)
    half = DA_HEAD_DIM // 2
    low = (lane & (DA_HEAD_DIM - 1)) < half

    qn = q * _seg_rsqrt_bcast(q, seg_ref, exp_ref) * qg_ref[...]
    q_out[0] = (_rope(qn, cosf, sinf, low, half, d) * scale).T.astype(BF16)
    kn = k * _seg_rsqrt_bcast(k, seg_ref, exp_ref) * kg_ref[...]
    k_out[0] = _rope(kn, cosf, sinf, low, half, d).astype(BF16)


def _da_pre(x, mod, posf, norm_g, w_qkv, q_norm, k_norm):
    b, s, d = x.shape
    t = min(TOK_TILE, s)
    dh = DA_HEAD_DIM
    nseg = d // dh
    lane = np.arange(LANES)
    invf = (ROPE_THETA ** (-np.arange(0, dh, 2, dtype=np.float32) / dh)).astype(np.float32)
    invf_row = invf[lane % (dh // 2)][None, :]
    sgn_row = np.where((lane % dh) < dh // 2, -1.0, 1.0).astype(np.float32)[None, :]
    seg = np.zeros((d, LANES), np.float32)
    seg[np.arange(d), np.arange(d) // dh] = 1.0 / dh
    expm = np.zeros((2 * LANES, d), np.float32)
    expm[np.arange(d) // dh, np.arange(d)] = 1.0
    expm[LANES + np.arange(d) // dh, np.arange(d)] = 1.0
    assert nseg <= LANES
    row = lambda bi, ti: (0, 0)
    tok = lambda bi, ti: (bi, ti, 0)
    tk = min(ATT_TILE, t)
    return pl.pallas_call(
        functools.partial(_da_pre_kernel, d=d, scale=dh ** -0.5 * LOG2E),
        grid=(b, s // t),
        in_specs=[pl.BlockSpec((1, t, d), tok),
                  pl.BlockSpec((1, 6, d), lambda bi, ti: (bi, 0, 0)),
                  pl.BlockSpec((1, t, 1), tok),
                  pl.BlockSpec((1, d), row),
                  pl.BlockSpec((d, 3 * d), row),
                  pl.BlockSpec((1, d), row),
                  pl.BlockSpec((1, d), row),
                  pl.BlockSpec((1, LANES), row),
                  pl.BlockSpec((1, LANES), row),
                  pl.BlockSpec((d, LANES), row),
                  pl.BlockSpec((2 * LANES, d), row)],
        out_specs=[pl.BlockSpec((1, d, t), lambda bi, ti: (bi, 0, ti)),
                   pl.BlockSpec((1, t, d), tok),
                   pl.BlockSpec((1, t // tk, d, tk), lambda bi, ti: (bi, ti, 0, 0))],
        out_shape=[jax.ShapeDtypeStruct((b, d, s), BF16),
                   jax.ShapeDtypeStruct((b, s, d), BF16),
                   jax.ShapeDtypeStruct((b, s // tk, d, tk), BF16)],
        compiler_params=_cparams(("arbitrary", "arbitrary")),
    )(x, mod, posf, norm_g.reshape(1, d), w_qkv.astype(BF16),
      jnp.tile(q_norm, nseg).reshape(1, d), jnp.tile(k_norm, nseg).reshape(1, d),
      jnp.asarray(invf_row), jnp.asarray(sgn_row), jnp.asarray(seg, BF16), jnp.asarray(expm, BF16))


def _softmax_block(s, vt_blk, g, m_ref, l_ref, acc_ref, mask):
    if mask is not None:
        s = jnp.where(mask, s, NEG_INF)
    m_prev = m_ref[g]
    m_new = jnp.maximum(m_prev, jnp.max(s, axis=0, keepdims=True))
    alpha = jnp.exp2(m_prev - m_new)
    p = jnp.exp2(s - m_new)
    l_ref[g] = alpha * l_ref[g] + jnp.sum(p, axis=0, keepdims=True)
    acc_ref[g] = alpha * acc_ref[g] + jnp.dot(vt_blk, p.astype(BF16), preferred_element_type=F32)
    m_ref[g] = m_new


def _causal_sweep(qi, scores, softmax, sa_ref, sb_ref, diag_mask):
    def fill(ref, kb):
        for g, s in enumerate(scores(kb)):
            ref[g] = s

    fill(sa_ref, 0)

    def body(j, c):
        kb = 2 * j
        fill(sb_ref, kb + 1)
        softmax(sa_ref, kb, None)
        fill(sa_ref, kb + 2)
        softmax(sb_ref, kb + 1, None)
        return c

    lax.fori_loop(0, lax.shift_right_logical(qi, 1), body, 0)
    odd = (qi & 1) == 1

    @pl.when(jnp.logical_not(odd))
    def _():
        softmax(sa_ref, qi, diag_mask)

    @pl.when(odd)
    def _():
        fill(sb_ref, qi)
        softmax(sa_ref, qi - 1, None)
        softmax(sb_ref, qi, diag_mask)


def _init_stats(m_ref, l_ref, acc_ref):
    m_ref[...] = jnp.full(m_ref.shape, NEG_INF, F32)
    l_ref[...] = jnp.zeros(l_ref.shape, F32)
    acc_ref[...] = jnp.zeros(acc_ref.shape, F32)


def _causal_mask_t(tk, cols, tq):
    c = lax.broadcasted_iota(jnp.int32, (tk, cols), 0)
    r = lax.broadcasted_iota(jnp.int32, (tk, cols), 1)
    r = jnp.where(r >= tq, r - tq, r)
    return c <= r


def _da_attn_kernel(lq1_ref, lk1_ref, lq2_ref, lk2_ref, sub_ref, qt_ref, k_ref, vt_ref, o_ref,
                    qs_ref, m_ref, l_ref, acc_ref, sa_ref, sb_ref, *, tq, heads, lam_init):
    qi = pl.program_id(2)
    sub = lax.broadcasted_iota(jnp.int32, (LANES, tq), 0)
    for g in range(heads):
        qt = qt_ref[0, g * LANES:(g + 1) * LANES, :]
        zero = jnp.zeros_like(qt)
        qs_ref[g] = jnp.concatenate([jnp.where(sub < DA_HEAD_DIM, qt, zero),
                                     jnp.where(sub >= DA_HEAD_DIM, qt, zero)], axis=1)
    _init_stats(m_ref, l_ref, acc_ref)

    def scores(kb):
        off = pl.multiple_of(kb * tq, tq)
        return [jnp.dot(k_ref[0, pl.ds(off, tq), g * LANES:(g + 1) * LANES], qs_ref[g],
                        preferred_element_type=F32) for g in range(heads)]

    def softmax(s, kb, mask):
        for g in range(heads):
            _softmax_block(s[g], vt_ref[0, kb, g * LANES:(g + 1) * LANES, :], g, m_ref, l_ref, acc_ref, mask)

    _causal_sweep(qi, scores, softmax, sa_ref, sb_ref, _causal_mask_t(tq, 2 * tq, tq))

    lam = (jnp.exp(jnp.sum(lq1_ref[...] * lk1_ref[...], axis=1, keepdims=True))
           - jnp.exp(jnp.sum(lq2_ref[...] * lk2_ref[...], axis=1, keepdims=True)) + lam_init)
    for g in range(heads):
        ot = acc_ref[g] / l_ref[g]
        dd = (ot[:, :tq] - lam * ot[:, tq:]).T
        ms = jnp.mean(dd * dd, axis=-1, keepdims=True)
        o_ref[0, :, g * LANES:(g + 1) * LANES] = (
            dd * lax.rsqrt(ms + EPS) * sub_ref[...] * (1.0 - lam_init)).astype(BF16)


def _da_attn(qt, k, vt, lq1, lk1, lq2, lk2, subln, lam_init):
    b, s, d = k.shape
    heads = 2
    wb = heads * LANES
    tq = vt.shape[3]
    vec = lambda bi, hi, qi: (0, 0)
    return pl.pallas_call(
        functools.partial(_da_attn_kernel, tq=tq, heads=heads, lam_init=lam_init),
        grid=(b, d // wb, s // tq),
        in_specs=[pl.BlockSpec((1, DA_HEAD_DIM), vec)] * 4 + [
            pl.BlockSpec((1, LANES), vec),
            pl.BlockSpec((1, wb, tq), lambda bi, hi, qi: (bi, hi, qi)),
            pl.BlockSpec((1, s, wb), lambda bi, hi, qi: (bi, 0, hi)),
            pl.BlockSpec((1, s // tq, wb, tq), lambda bi, hi, qi: (bi, 0, hi, 0))],
        out_specs=pl.BlockSpec((1, tq, wb), lambda bi, hi, qi: (bi, qi, hi)),
        out_shape=jax.ShapeDtypeStruct((b, s, d), BF16),
        scratch_shapes=[pltpu.VMEM((heads, LANES, 2 * tq), BF16),
                        pltpu.VMEM((heads, 1, 2 * tq), F32),
                        pltpu.VMEM((heads, 1, 2 * tq), F32),
                        pltpu.VMEM((heads, LANES, 2 * tq), F32),
                        pltpu.VMEM((heads, tq, 2 * tq), F32),
                        pltpu.VMEM((heads, tq, 2 * tq), F32)],
        compiler_params=_cparams(("arbitrary", "arbitrary", "arbitrary")),
    )(lq1.reshape(1, -1), lk1.reshape(1, -1), lq2.reshape(1, -1), lk2.reshape(1, -1),
      subln.reshape(1, -1), qt, k, vt)


def _mla_attn_kernel(qt_ref, k_ref, vt_ref, o_ref, m_ref, l_ref, acc_ref, sa_ref, sb_ref, *, tq, heads):
    qi = pl.program_id(2)
    _init_stats(m_ref, l_ref, acc_ref)

    def scores(kb):
        off = pl.multiple_of(kb * tq, tq)
        return [jnp.dot(k_ref[0, pl.ds(off, tq), g * LANES:(g + 1) * LANES],
                        qt_ref[0, g * LANES:(g + 1) * LANES, :],
                        preferred_element_type=F32) for g in range(heads)]

    def softmax(s, kb, mask):
        for g in range(heads):
            _softmax_block(s[g], vt_ref[0, kb, g * MLA_V:(g + 1) * MLA_V, :], g, m_ref, l_ref, acc_ref, mask)

    _causal_sweep(qi, scores, softmax, sa_ref, sb_ref, _causal_mask_t(tq, tq, tq))
    for p in range(heads // 2):
        ot = jnp.concatenate([acc_ref[2 * p] / l_ref[2 * p], acc_ref[2 * p + 1] / l_ref[2 * p + 1]], axis=0)
        o_ref[0, :, p * LANES:(p + 1) * LANES] = ot.T.astype(BF16)


def _mla_attn(qt, k, vt):
    b, s, dk = k.shape
    heads = 4
    tq = vt.shape[3]
    return pl.pallas_call(
        functools.partial(_mla_attn_kernel, tq=tq, heads=heads),
        grid=(b, dk // (heads * LANES), s // tq),
        in_specs=[pl.BlockSpec((1, heads * LANES, tq), lambda bi, hi, qi: (bi, hi, qi)),
                  pl.BlockSpec((1, s, heads * LANES), lambda bi, hi, qi: (bi, 0, hi)),
                  pl.BlockSpec((1, s // tq, heads * MLA_V, tq), lambda bi, hi, qi: (bi, 0, hi, 0))],
        out_specs=pl.BlockSpec((1, tq, heads * MLA_V), lambda bi, hi, qi: (bi, qi, hi)),
        out_shape=jax.ShapeDtypeStruct((b, s, dk // LANES * MLA_V), BF16),
        scratch_shapes=[pltpu.VMEM((heads, 1, tq), F32),
                        pltpu.VMEM((heads, 1, tq), F32),
                        pltpu.VMEM((heads, MLA_V, tq), F32),
                        pltpu.VMEM((heads, tq, tq), F32),
                        pltpu.VMEM((heads, tq, tq), F32)],
        compiler_params=_cparams(("arbitrary", "arbitrary", "arbitrary")),
    )(qt, k, vt)


def _mla_pre_kernel(x_ref, mod_ref, pos_ref, g_ref, win_ref, qag_ref, kvag_ref, kpeg_ref,
                    wuq_ref, wuk_ref, wuv_ref, qg_ref, kg_ref, invf_ref, sgn_ref,
                    segq_ref, expq_ref, segk_ref, expk_ref, place_ref,
                    q_out, k_out, v_out, *, scale):
    h = _adaln_h(x_ref[0], mod_ref, g_ref, 0, 1)
    lat = jnp.dot(h.astype(BF16), win_ref[...], preferred_element_type=F32)
    cq = lat[:, :MLA_Q_RANK]
    ckv = lat[:, MLA_Q_RANK:MLA_Q_RANK + MLA_KV_RANK]
    kpe = lat[:, MLA_Q_RANK + MLA_KV_RANK:]
    cqn = (cq * lax.rsqrt(jnp.mean(cq * cq, axis=-1, keepdims=True) + EPS) * qag_ref[...]).astype(BF16)
    ckvn = (ckv * lax.rsqrt(jnp.mean(ckv * ckv, axis=-1, keepdims=True) + EPS) * kvag_ref[...]).astype(BF16)
    kpen = kpe * lax.rsqrt(jnp.sum(kpe * kpe, axis=-1, keepdims=True) * (1.0 / MLA_ROPE) + EPS) * kpeg_ref[...]

    q = jnp.dot(cqn, wuq_ref[...], preferred_element_type=F32)
    kn = jnp.dot(ckvn, wuk_ref[...], preferred_element_type=F32)
    _store_vt(v_out, jnp.dot(ckvn, wuv_ref[...], preferred_element_type=F32))

    w = q.shape[1]
    reps = w // LANES
    ang = pos_ref[0] * invf_ref[...]
    cosf = jnp.tile(jnp.cos(ang), (1, reps))
    sinf = jnp.tile(jnp.sin(ang) * sgn_ref[...], (1, reps))
    lane = lax.broadcasted_iota(jnp.int32, (1, w), 1)
    half = MLA_ROPE // 2
    low = (lane & (MLA_ROPE - 1)) < half

    qn = q * _seg_rsqrt_bcast(q, segq_ref, expq_ref) * qg_ref[...]
    q_out[0] = (_rope(qn, cosf, sinf, low, half, w) * scale).T.astype(BF16)
    knn = kn * _seg_rsqrt_bcast(kn, segk_ref, expk_ref) * kg_ref[...]
    kfull = knn + jnp.dot(_split_bf16(kpen), place_ref[...], preferred_element_type=F32)
    k_out[0] = _rope(kfull, cosf, sinf, low, half, w).astype(BF16)


def _mla_pre(x, mod, posf, norm_g, w_in, q_a_norm, kv_a_norm, w_uq, w_ukv,
             q_nope_norm, q_pe_norm, k_nope_norm, k_pe_norm):
    b, s, d = x.shape
    t = min(TOK_TILE, s)
    nh, nope, rope, vd = MLA_HEADS, MLA_NOPE, MLA_ROPE, MLA_V
    w = nh * LANES
    in_w = MLA_Q_RANK + MLA_KV_RANK + rope
    w_in_p = jnp.pad(w_in, ((0, 0), (0, MLA_IN_PAD - in_w))).astype(BF16)
    wuq = w_uq.reshape(MLA_Q_RANK, nh, nope + rope)
    wuq_p = jnp.pad(wuq, ((0, 0), (0, 0), (0, LANES - nope - rope))).reshape(MLA_Q_RANK, w).astype(BF16)
    wukv = w_ukv.reshape(MLA_KV_RANK, nh, nope + vd)
    wuk_p = jnp.pad(wukv[:, :, :nope], ((0, 0), (0, 0), (0, LANES - nope))).reshape(MLA_KV_RANK, w).astype(BF16)
    wuv = wukv[:, :, nope:].reshape(MLA_KV_RANK, nh * vd).astype(BF16)
    zpad = jnp.zeros((LANES - nope - rope,), F32)
    qg = jnp.tile(jnp.concatenate([q_nope_norm, q_pe_norm, zpad]), nh).reshape(1, w)
    kg = jnp.tile(jnp.concatenate([k_nope_norm, jnp.zeros((LANES - nope,), F32)]), nh).reshape(1, w)
    kpeg = jnp.concatenate([k_pe_norm, jnp.zeros((LANES - rope,), F32)]).reshape(1, LANES)

    lane = np.arange(LANES)
    in_rope = (lane >= nope) & (lane < nope + rope)
    invf = (ROPE_THETA ** (-np.arange(0, rope, 2, dtype=np.float32) / rope)).astype(np.float32)
    invf_row = np.where(in_rope, invf[(lane - nope) % (rope // 2)], 0.0).astype(np.float32)[None, :]
    sgn_row = np.where(in_rope, np.where((lane - nope) < rope // 2, -1.0, 1.0), 0.0).astype(np.float32)[None, :]

    col = np.arange(w)
    hd, off = col // LANES, col % LANES
    segq = np.zeros((w, LANES), np.float32)
    expq = np.zeros((2 * LANES, w), np.float32)
    is_nope, is_pe = off < nope, (off >= nope) & (off < nope + rope)
    segq[col[is_nope], 2 * hd[is_nope]] = 1.0 / nope
    segq[col[is_pe], 2 * hd[is_pe] + 1] = 1.0 / rope
    for base in (0, LANES):
        expq[base + 2 * hd[is_nope], col[is_nope]] = 1.0
        expq[base + 2 * hd[is_pe] + 1, col[is_pe]] = 1.0
    segk = np.zeros((w, LANES), np.float32)
    expk = np.zeros((2 * LANES, w), np.float32)
    segk[col[is_nope], hd[is_nope]] = 1.0 / nope
    place = np.zeros((2 * LANES, w), np.float32)
    for base in (0, LANES):
        expk[base + hd[is_nope], col[is_nope]] = 1.0
        place[base + off[is_pe] - nope, col[is_pe]] = 1.0
    assert 2 * nh <= LANES

    row = lambda bi, ti: (0, 0)
    tok = lambda bi, ti: (bi, ti, 0)
    full = lambda a: pl.BlockSpec(a.shape, row)
    consts = [norm_g.reshape(1, d), w_in_p, q_a_norm.reshape(1, -1), kv_a_norm.reshape(1, -1), kpeg,
              wuq_p, wuk_p, wuv, qg, kg, jnp.asarray(invf_row), jnp.asarray(sgn_row),
              jnp.asarray(segq, BF16), jnp.asarray(expq, BF16), jnp.asarray(segk, BF16),
              jnp.asarray(expk, BF16), jnp.asarray(place, BF16)]
    tk = min(ATT_TILE, t)
    return pl.pallas_call(
        functools.partial(_mla_pre_kernel, scale=(nope + rope) ** -0.5 * LOG2E),
        grid=(b, s // t),
        in_specs=[pl.BlockSpec((1, t, d), tok),
                  pl.BlockSpec((1, 6, d), lambda bi, ti: (bi, 0, 0)),
                  pl.BlockSpec((1, t, 1), tok)] + [full(a) for a in consts],
        out_specs=[pl.BlockSpec((1, w, t), lambda bi, ti: (bi, 0, ti)),
                   pl.BlockSpec((1, t, w), tok),
                   pl.BlockSpec((1, t // tk, nh * vd, tk), lambda bi, ti: (bi, ti, 0, 0))],
        out_shape=[jax.ShapeDtypeStruct((b, w, s), BF16),
                   jax.ShapeDtypeStruct((b, s, w), BF16),
                   jax.ShapeDtypeStruct((b, s // tk, nh * vd, tk), BF16)],
        compiler_params=_cparams(("arbitrary", "arbitrary")),
    )(x, mod, posf, *consts)


def _post_kernel(o_ref, wo_ref, x_ref, mod_ref, g_ref, wr_ref, bsel_ref,
                 x_out, h_out, route_out, cnt_out, run_ref, *, d):
    first = (pl.program_id(0) == 0) & (pl.program_id(1) == 0)

    @pl.when(first)
    def _():
        run_ref[...] = jnp.zeros_like(run_ref)

    y = jnp.dot(o_ref[0], wo_ref[...], preferred_element_type=F32)
    x = x_ref[0] + mod_ref[0, 2:3, :] * y
    x_out[0] = x
    h = _adaln_h(x, mod_ref, g_ref, 3, 4)
    t = h.shape[0]

    hh = h.astype(BF16)
    hl = (h - hh.astype(F32)).astype(BF16)
    logits = (jnp.dot(hh, wr_ref[0], preferred_element_type=F32)
              + jnp.dot(hl, wr_ref[0], preferred_element_type=F32)
              + jnp.dot(hh, wr_ref[1], preferred_element_type=F32))

    lane_i = lax.broadcasted_iota(jnp.int32, (t, LANES), 1)
    lane = lane_i.astype(F32)
    big = float(LANES)
    ng, ne = N_GROUPS, EXPERTS_PER_GROUP

    def first_argmax(val):
        mx = jnp.max(val, axis=1, keepdims=True)
        return jnp.min(jnp.where(val == mx, lane, big), axis=1, keepdims=True)

    def pick(val, idx):
        return jnp.sum(jnp.where(lane == idx, val, 0.0), axis=1, keepdims=True)

    gmask = lane_i < ng
    gl = jnp.where(gmask, logits, NEG_INF)
    ge = jnp.exp(gl - jnp.max(gl, axis=1, keepdims=True))
    gprob = ge / jnp.sum(ge, axis=1, keepdims=True)
    gidx = first_argmax(jnp.where(gmask, gprob + bsel_ref[...], NEG_INF))
    ggate = pick(gprob, gidx)

    base = ng + ne * gidx
    emask = (lane >= base) & (lane < base + ne)
    el = jnp.where(emask, logits, NEG_INF)
    ee = jnp.exp(el - jnp.max(el, axis=1, keepdims=True))
    eprob = ee / jnp.sum(ee, axis=1, keepdims=True)
    sel = jnp.where(emask, eprob + bsel_ref[...], NEG_INF)
    i1 = first_argmax(sel)
    i2 = first_argmax(jnp.where(lane == i1, NEG_INF, sel))
    p1 = pick(eprob, i1)
    p2 = pick(eprob, i2)
    psum = p1 + p2
    w1 = p1 / psum * ggate
    w2 = p2 / psum * ggate
    e1 = i1 - base
    e2 = i2 - base
    swap = e2 < e1
    lo = jnp.where(swap, e2, e1)
    hi = jnp.where(swap, e1, e2)
    wa = jnp.where(swap, w2, w1)
    wb = jnp.where(swap, w1, w2)
    combo = gidx * float(N_PAIRS) + lo * (2.0 * ne - 1.0 - lo) * 0.5 + (hi - lo - 1.0)

    _pack_token_tiles(h_out, h, wa, wb)

    onehot = lane == combo
    r_i = lax.broadcasted_iota(jnp.int32, (t, t), 0)
    c_i = lax.broadcasted_iota(jnp.int32, (t, t), 1)
    tri = jnp.where(c_i < r_i, 1.0, 0.0).astype(BF16)
    oh = jnp.where(onehot, 1.0, 0.0)
    before = jnp.dot(tri, oh.astype(BF16), preferred_element_type=F32) + run_ref[...]
    rank = jnp.sum(jnp.where(onehot, before, 0.0), axis=1, keepdims=True)
    run = run_ref[...] + jnp.sum(oh, axis=0, keepdims=True)
    run_ref[...] = run
    cnt_out[...] = run
    route_out[0] = jnp.where(lane_i == 0, combo, jnp.where(lane_i == 1, rank, 0.0))


def _post(o, w_o, x, mod, norm_g, w_group, b_group, w_router, b_router):
    b, s, d = x.shape
    t = min(TOK_TILE, s)
    do = o.shape[2]
    ng, ne = N_GROUPS, EXPERTS_PER_GROUP
    wr = jnp.pad(jnp.concatenate([w_group, w_router], axis=1), ((0, 0), (0, LANES - ng - ng * ne)))
    wr_hi = wr.astype(BF16)
    wr_lo = (wr - wr_hi.astype(F32)).astype(BF16)
    wr2 = jnp.stack([wr_hi, wr_lo])
    bsel = jnp.pad(jnp.concatenate([b_group, b_router.reshape(-1)]), (0, LANES - ng - ng * ne)).reshape(1, LANES)
    row = lambda bi, ti: (0, 0)
    tok = lambda bi, ti: (bi, ti, 0)
    return pl.pallas_call(
        functools.partial(_post_kernel, d=d),
        grid=(b, s // t),
        in_specs=[pl.BlockSpec((1, t, do), tok),
                  pl.BlockSpec((do, d), row),
                  pl.BlockSpec((1, t, d), tok),
                  pl.BlockSpec((1, 6, d), lambda bi, ti: (bi, 0, 0)),
                  pl.BlockSpec((1, d), row),
                  pl.BlockSpec((2, d, LANES), lambda bi, ti: (0, 0, 0)),
                  pl.BlockSpec((1, LANES), row)],
        out_specs=[pl.BlockSpec((1, t, d), tok),
                   pl.BlockSpec((1, t * SUBLANES, LANES), tok),
                   pl.BlockSpec((1, t, LANES), tok),
                   pl.BlockSpec((1, LANES), row)],
        out_shape=[jax.ShapeDtypeStruct((b, s, d), F32),
                   jax.ShapeDtypeStruct((b, s * SUBLANES, LANES), jnp.uint32),
                   jax.ShapeDtypeStruct((b, s, LANES), F32),
                   jax.ShapeDtypeStruct((1, LANES), F32)],
        scratch_shapes=[pltpu.VMEM((1, LANES), F32)],
        compiler_params=_cparams(("arbitrary", "arbitrary")),
    )(o, w_o.astype(BF16), x, mod, norm_g.reshape(1, d), wr2, bsel)


def _tok_rows(s, n):
    return pl.ds(s, n, stride=SUBLANES)


def _tok_tile(tok):
    return pl.ds(pl.multiple_of(tok * SUBLANES, SUBLANES), SUBLANES)


def _pack_token_tiles(p_out, h, wa, wb):
    t, d = h.shape
    half = d // 2
    bits = pltpu.bitcast(h.astype(BF16).astype(F32), jnp.uint32)
    word = bits[:, :half] | (bits[:, half:] >> 16)
    nw = half // LANES
    for s in range(nw):
        p_out[0, _tok_rows(s, t), :] = word[:, s * LANES:(s + 1) * LANES]
    lane = lax.broadcasted_iota(jnp.int32, (t, LANES), 1)
    wrow = jnp.where(lane == 0, wa, jnp.where(lane == 1, wb, 0.0))
    p_out[0, _tok_rows(nw, t), :] = pltpu.bitcast(wrow, jnp.uint32)
    for s in range(nw + 1, SUBLANES):
        p_out[0, _tok_rows(s, t), :] = jnp.zeros((t, LANES), jnp.uint32)


def _unpack_token_tiles(hs_ref, d):
    nw = d // 2 // LANES
    tm = hs_ref.shape[0] // SUBLANES
    words = [hs_ref[_tok_rows(s, tm), :] for s in range(nw)]
    hi = [pltpu.bitcast(w & jnp.uint32(0xFFFF0000), F32).astype(BF16) for w in words]
    lo = [pltpu.bitcast(w << 16, F32).astype(BF16) for w in words]
    wrow = pltpu.bitcast(hs_ref[_tok_rows(nw, tm), :], F32)
    return jnp.concatenate(hi + lo, axis=1), wrow[:, 0:1], wrow[:, 1:2]


def _row_copy(src_ref, src_row, dst_ref, dst_row, sem):
    return pltpu.make_async_copy(src_ref.at[_tok_tile(src_row), :], dst_ref.at[_tok_tile(dst_row), :], sem)


def _row_copies(rows, make_copy):
    def start(i, c):
        for u in range(ROW_UNROLL):
            make_copy(i * ROW_UNROLL + u).start(priority=u % 2)
        return c

    def wait(i, c):
        for u in range(ROW_UNROLL):
            make_copy(0).wait()
        return c

    lax.fori_loop(0, rows // ROW_UNROLL, start, 0)
    lax.fori_loop(0, rows // ROW_UNROLL, wait, 0)


def _dispatch_kernel(dest_ref, h_ref, init_ref, hs_ref, sem, *, rows):
    del init_ref
    _row_copies(rows, lambda r: _row_copy(h_ref, r, hs_ref, dest_ref[r], sem))


def _dispatch(dest, h2, n_rows):
    n = h2.shape[0] // SUBLANES
    rows = min(ROW_TILE, n)
    return pl.pallas_call(
        functools.partial(_dispatch_kernel, rows=rows),
        grid=(n // rows,),
        in_specs=[pl.BlockSpec((rows,), lambda i: (i,), memory_space=pltpu.SMEM),
                  pl.BlockSpec((rows * SUBLANES, LANES), lambda i: (i, 0)),
                  pl.BlockSpec(memory_space=pl.ANY)],
        out_specs=pl.BlockSpec(memory_space=pl.ANY),
        out_shape=jax.ShapeDtypeStruct((n_rows * SUBLANES, LANES), jnp.uint32),
        scratch_shapes=[pltpu.SemaphoreType.DMA(())],
        input_output_aliases={2: 0},
        compiler_params=_cparams(("arbitrary",)),
    )(dest, h2, jnp.zeros((n_rows * SUBLANES, LANES), jnp.uint32))


def _moe_kernel(elo_ref, ehi_ref, blk_ref, nact_ref, hs_ref, w1a_ref, w3a_ref, w2a_ref,
                w1b_ref, w3b_ref, w2b_ref, y_ref, *, d):
    del elo_ref, ehi_ref, blk_ref
    active = pl.program_id(0) < nact_ref[0]

    @pl.when(jnp.logical_not(active))
    def _():
        y_ref[...] = jnp.zeros_like(y_ref)

    @pl.when(active)
    def _():
        x, wa, wb = _unpack_token_tiles(hs_ref, d)

        def expert(w1_ref, w3_ref, wgt):
            a = jnp.dot(x, w1_ref[0], preferred_element_type=F32)
            g = jnp.dot(x, w3_ref[0], preferred_element_type=F32)
            return (a * jax.nn.sigmoid(a) * g * wgt).astype(BF16)

        y = (jnp.dot(expert(w1a_ref, w3a_ref, wa), w2a_ref[0], preferred_element_type=F32)
             + jnp.dot(expert(w1b_ref, w3b_ref, wb), w2b_ref[0], preferred_element_type=F32))
        for s in range(d // LANES):
            y_ref[_tok_rows(s, y.shape[0]), :] = y[:, s * LANES:(s + 1) * LANES]


def _moe(hs, e_lo, e_hi, blk, nact, w1, w3, w2):
    n_rows = hs.shape[0] // SUBLANES
    d = w1.shape[1]
    ff = w1.shape[2]
    tm = MOE_TILE
    n_tiles = n_rows // tm
    assert d == SUBLANES * LANES
    wspec = lambda shape, which: pl.BlockSpec(shape, (lambda j, lo, hi, bk, na: (lo[j], 0, 0)) if which == 0
                                              else (lambda j, lo, hi, bk, na: (hi[j], 0, 0)))
    grid_spec = pltpu.PrefetchScalarGridSpec(
        num_scalar_prefetch=4,
        grid=(n_tiles,),
        in_specs=[pl.BlockSpec((tm * SUBLANES, LANES), lambda j, lo, hi, bk, na: (bk[j], 0)),
                  wspec((1, d, ff), 0), wspec((1, d, ff), 0), wspec((1, ff, d), 0),
                  wspec((1, d, ff), 1), wspec((1, d, ff), 1), wspec((1, ff, d), 1)],
        out_specs=pl.BlockSpec((tm * SUBLANES, LANES), lambda j, lo, hi, bk, na: (j, 0)),
    )
    return pl.pallas_call(
        functools.partial(_moe_kernel, d=d),
        grid_spec=grid_spec,
        out_shape=jax.ShapeDtypeStruct((n_rows * SUBLANES, LANES), F32),
        compiler_params=_cparams(("arbitrary",)),
    )(e_lo, e_hi, blk, nact, hs, w1, w3, w2, w1, w3, w2)


def _combine_kernel(dest_ref, x_ref, gate_ref, y_ref, o_ref, buf_ref, sem, *, rows):
    _row_copies(rows, lambda r: _row_copy(y_ref, dest_ref[r], buf_ref, r, sem))
    for s in range(SUBLANES):
        cols = slice(s * LANES, (s + 1) * LANES)
        o_ref[:, cols] = x_ref[:, cols] + gate_ref[0, :, cols] * buf_ref[_tok_rows(s, rows), :]


def _combine(dest, x2, gate, y, seq):
    n, d = x2.shape
    rows = min(ROW_TILE, seq)
    per_seq = seq // rows
    return pl.pallas_call(
        functools.partial(_combine_kernel, rows=rows),
        grid=(n // rows,),
        in_specs=[pl.BlockSpec((rows,), lambda i: (i,), memory_space=pltpu.SMEM),
                  pl.BlockSpec((rows, d), lambda i: (i, 0)),
                  pl.BlockSpec((1, 1, d), lambda i: (i // per_seq, 0, 0)),
                  pl.BlockSpec(memory_space=pl.ANY)],
        out_specs=pl.BlockSpec((rows, d), lambda i: (i, 0)),
        out_shape=jax.ShapeDtypeStruct((n, d), F32),
        scratch_shapes=[pltpu.VMEM((rows * SUBLANES, LANES), F32), pltpu.SemaphoreType.DMA(())],
        compiler_params=_cparams(("arbitrary",)),
    )(dest, x2, gate, y)


def _moe_layer(x_new, h2, route, counts, gate_f, w1, w3, w2):
    b, s, d = x_new.shape
    n = b * s
    tm = MOE_TILE
    n_tiles = n // tm + N_COMBOS
    combo = route[:, :, 0].reshape(n).astype(jnp.int32)
    rank = route[:, :, 1].reshape(n).astype(jnp.int32)
    cnt = counts[0, :N_COMBOS].astype(jnp.int32)
    tiles_per = (cnt + tm - 1) // tm
    tile_end = jnp.cumsum(tiles_per)
    row_off = (tile_end - tiles_per) * tm
    ids = jnp.arange(N_COMBOS, dtype=jnp.int32)

    def lookup(table, idx):
        return jnp.sum(jnp.where(idx[:, None] == ids[None, :], table[None, :], 0), axis=1)

    dest = lookup(row_off, combo) + rank
    nact = tile_end[-1:]
    blk = jnp.minimum(jnp.arange(n_tiles, dtype=jnp.int32), nact[0] - 1)
    tile_combo = jnp.sum((tile_end[None, :] <= blk[:, None]).astype(jnp.int32), axis=1)
    e_lo = lookup(jnp.asarray(_COMBO_LO), tile_combo)
    e_hi = lookup(jnp.asarray(_COMBO_HI), tile_combo)

    hs = _dispatch(dest, h2.reshape(n * SUBLANES, LANES), n_tiles * tm)
    ne = w1.shape[0] * w1.shape[1]
    y = _moe(hs, e_lo, e_hi, blk, nact.astype(jnp.int32),
             w1.reshape(ne, d, -1).astype(BF16), w3.reshape(ne, d, -1).astype(BF16),
             w2.reshape(ne, -1, d).astype(BF16))
    return _combine(dest, x_new.reshape(n, d), gate_f, y, s).reshape(b, s, d)


def kernel(x, c, positions, ada_w, ada_b, norm_mix, norm_ffn, da_w_qkv, da_q_norm, da_k_norm, da_lambda_q1, da_lambda_k1, da_lambda_q2, da_lambda_k2, da_subln, da_w_o, mla_w_in, mla_q_a_norm, mla_kv_a_norm, mla_w_uq, mla_w_ukv, mla_q_nope_norm, mla_q_pe_norm, mla_k_nope_norm, mla_k_pe_norm, mla_w_o, moe_w_group, moe_b_group, moe_w_router, moe_b_router, moe_w1, moe_w3, moe_w2):
    b, s, d = x.shape
    depth = ada_w.shape[0]
    mod_all = _modulation(c, ada_w, ada_b).reshape(depth, b, 6, d)
    posf = positions.astype(F32).reshape(b, s, 1)
    for i in range(depth):
        mod = mod_all[i]
        j = i // 2
        if i % 2 == 0:
            lam_init = 0.8 - 0.6 * math.exp(-0.3 * i)
            q, k, v = _da_pre(x, mod, posf, norm_mix[i], da_w_qkv[j], da_q_norm[j], da_k_norm[j])
            o = _da_attn(q, k, v, da_lambda_q1[j], da_lambda_k1[j], da_lambda_q2[j], da_lambda_k2[j],
                         da_subln[j], lam_init)
            w_o = da_w_o[j]
        else:
            q, k, v = _mla_pre(x, mod, posf, norm_mix[i], mla_w_in[j], mla_q_a_norm[j], mla_kv_a_norm[j],
                               mla_w_uq[j], mla_w_ukv[j], mla_q_nope_norm[j], mla_q_pe_norm[j],
                               mla_k_nope_norm[j], mla_k_pe_norm[j])
            o = _mla_attn(q, k, v)
            w_o = mla_w_o[j]
        x_new, h2, route, counts = _post(o, w_o, x, mod, norm_ffn[i], moe_w_group[i], moe_b_group[i],
                                         moe_w_router[i], moe_b_router[i])
        x = _moe_layer(x_new, h2, route, counts, mod[:, 5:6, :], moe_w1[i], moe_w3[i], moe_w2[i])
    return x
```

```python
import functools
import math

import numpy as np
import jax
import jax.numpy as jnp
from jax import lax
from jax.experimental import pallas as pl
from jax.experimental.pallas import tpu as pltpu

F32 = jnp.float32
BF16 = jnp.bfloat16

ROPE_THETA = 10000.0
EPS = 1e-6
NEG_INF = -1e30
LOG2E = math.log2(math.e)
DA_HEAD_DIM = 64
MLA_HEADS = 16
MLA_NOPE = 64
MLA_ROPE = 32
MLA_V = 64
MLA_Q_RANK = 384
MLA_KV_RANK = 256
N_GROUPS = 4
EXPERTS_PER_GROUP = 8
EXPERT_FF = 256

LANES = 128
VMEM_LIMIT = 56 * 1024 * 1024
TOK_TILE = 512
ATT_TILE = 256
MOE_TILE = 256
ROW_TILE = 1024
ROW_UNROLL = 8
MLA_IN_PAD = 768
SUBLANES = 8

N_PAIRS = EXPERTS_PER_GROUP * (EXPERTS_PER_GROUP - 1) // 2
N_COMBOS = N_GROUPS * N_PAIRS


def _combo_tables():
    lo_t, hi_t = [], []
    for g in range(N_GROUPS):
        for lo in range(EXPERTS_PER_GROUP):
            for hi in range(lo + 1, EXPERTS_PER_GROUP):
                lo_t.append(g * EXPERTS_PER_GROUP + lo)
                hi_t.append(g * EXPERTS_PER_GROUP + hi)
    return np.asarray(lo_t, np.int32), np.asarray(hi_t, np.int32)


_COMBO_LO, _COMBO_HI = _combo_tables()


def _cparams(sem):
    return pltpu.CompilerParams(dimension_semantics=sem, vmem_limit_bytes=VMEM_LIMIT)


def _split_bf16(x):
    hi = x.astype(BF16)
    lo = (x - hi.astype(F32)).astype(BF16)
    return jnp.concatenate([hi, lo], axis=1)


def _seg_rsqrt_bcast(t, seg_ref, exp_ref):
    ss = jnp.dot((t * t).astype(BF16), seg_ref[...], preferred_element_type=F32)
    r = lax.rsqrt(ss + EPS)
    return jnp.dot(_split_bf16(r), exp_ref[...], preferred_element_type=F32)


def _rope(t, cosf, sinf, low_mask, half, width):
    partner = jnp.where(low_mask, pltpu.roll(t, width - half, 1), pltpu.roll(t, half, 1))
    return t * cosf + partner * sinf


def _store_vt(v_out, v):
    vt = v.T
    tk = v_out.shape[3]
    for j in range(v_out.shape[1]):
        v_out[0, j] = vt[:, j * tk:(j + 1) * tk].astype(BF16)


def _adaln_h(x, mod_ref, g_ref, shift_row, scale_row):
    ms = jnp.mean(x * x, axis=-1, keepdims=True)
    h = x * lax.rsqrt(ms + EPS) * g_ref[...]
    return h * (1.0 + mod_ref[0, scale_row:scale_row + 1, :]) + mod_ref[0, shift_row:shift_row + 1, :]


def _mod_kernel(c_ref, w_ref, b_ref, o_ref):
    c = c_ref[...]
    cond = (c * jax.nn.sigmoid(c)).astype(BF16)
    o_ref[0] = jnp.dot(cond, w_ref[0].astype(BF16), preferred_element_type=F32) + b_ref[0]


def _modulation(c, ada_w, ada_b):
    depth, d, n6 = ada_w.shape
    b = c.shape[0]
    tn = 1536
    return pl.pallas_call(
        _mod_kernel,
        grid=(depth, n6 // tn),
        in_specs=[pl.BlockSpec((b, d), lambda i, j: (0, 0)),
                  pl.BlockSpec((1, d, tn), lambda i, j: (i, 0, j)),
                  pl.BlockSpec((1, 1, tn), lambda i, j: (i, 0, j))],
        out_specs=pl.BlockSpec((1, b, tn), lambda i, j: (i, 0, j)),
        out_shape=jax.ShapeDtypeStruct((depth, b, n6), F32),
        compiler_params=_cparams(("arbitrary", "arbitrary")),
    )(c, ada_w, ada_b.reshape(depth, 1, n6))


def _da_pre_kernel(x_ref, mod_ref, pos_ref, g_ref, w_ref, qg_ref, kg_ref, invf_ref, sgn_ref,
                   seg_ref, exp_ref, q_out, k_out, v_out, *, d, scale):
    h = _adaln_h(x_ref[0], mod_ref, g_ref, 0, 1)
    qkv = jnp.dot(h.astype(BF16), w_ref[...], preferred_element_type=F32)
    q = qkv[:, :d]
    k = qkv[:, d:2 * d]
    _store_vt(v_out, qkv[:, 2 * d:])

    ang = pos_ref[0] * invf_ref[...]
    reps = d // LANES
    cosf = jnp.tile(jnp.cos(ang), (1, reps))
    sinf = jnp.tile(jnp.sin(ang) * sgn_ref[...], (1, reps))
    lane = lax.broadcasted_iota(jnp.int32, (1, d), 1)
    half = DA_HEAD_DIM // 2
    low = (lane & (DA_HEAD_DIM - 1)) < half

    qn = q * _seg_rsqrt_bcast(q, seg_ref, exp_ref) * qg_ref[...]
    q_out[0] = (_rope(qn, cosf, sinf, low, half, d) * scale).T.astype(BF16)
    kn = k * _seg_rsqrt_bcast(k, seg_ref, exp_ref) * kg_ref[...]
    k_out[0] = _rope(kn, cosf, sinf, low, half, d).astype(BF16)


def _da_pre(x, mod, posf, norm_g, w_qkv, q_norm, k_norm):
    b, s, d = x.shape
    t = min(TOK_TILE, s)
    dh = DA_HEAD_DIM
    nseg = d // dh
    lane = np.arange(LANES)
    invf = (ROPE_THETA ** (-np.arange(0, dh, 2, dtype=np.float32) / dh)).astype(np.float32)
    invf_row = invf[lane % (dh // 2)][None, :]
    sgn_row = np.where((lane % dh) < dh // 2, -1.0, 1.0).astype(np.float32)[None, :]
    seg = np.zeros((d, LANES), np.float32)
    seg[np.arange(d), np.arange(d) // dh] = 1.0 / dh
    expm = np.zeros((2 * LANES, d), np.float32)
    expm[np.arange(d) // dh, np.arange(d)] = 1.0
    expm[LANES + np.arange(d) // dh, np.arange(d)] = 1.0
    assert nseg <= LANES
    row = lambda bi, ti: (0, 0)
    tok = lambda bi, ti: (bi, ti, 0)
    tk = min(ATT_TILE, t)
    return pl.pallas_call(
        functools.partial(_da_pre_kernel, d=d, scale=dh ** -0.5 * LOG2E),
        grid=(b, s // t),
        in_specs=[pl.BlockSpec((1, t, d), tok),
                  pl.BlockSpec((1, 6, d), lambda bi, ti: (bi, 0, 0)),
                  pl.BlockSpec((1, t, 1), tok),
                  pl.BlockSpec((1, d), row),
                  pl.BlockSpec((d, 3 * d), row),
                  pl.BlockSpec((1, d), row),
                  pl.BlockSpec((1, d), row),
                  pl.BlockSpec((1, LANES), row),
                  pl.BlockSpec((1, LANES), row),
                  pl.BlockSpec((d, LANES), row),
                  pl.BlockSpec((2 * LANES, d), row)],
        out_specs=[pl.BlockSpec((1, d, t), lambda bi, ti: (bi, 0, ti)),
                   pl.BlockSpec((1, t, d), tok),
                   pl.BlockSpec((1, t // tk, d, tk), lambda bi, ti: (bi, ti, 0, 0))],
        out_shape=[jax.ShapeDtypeStruct((b, d, s), BF16),
                   jax.ShapeDtypeStruct((b, s, d), BF16),
                   jax.ShapeDtypeStruct((b, s // tk, d, tk), BF16)],
        compiler_params=_cparams(("arbitrary", "arbitrary")),
    )(x, mod, posf, norm_g.reshape(1, d), w_qkv.astype(BF16),
      jnp.tile(q_norm, nseg).reshape(1, d), jnp.tile(k_norm, nseg).reshape(1, d),
      jnp.asarray(invf_row), jnp.asarray(sgn_row), jnp.asarray(seg, BF16), jnp.asarray(expm, BF16))


def _softmax_block(s, vt_blk, g, m_ref, l_ref, acc_ref, mask):
    if mask is not None:
        s = jnp.where(mask, s, NEG_INF)
    m_prev = m_ref[g]
    m_new = jnp.maximum(m_prev, jnp.max(s, axis=0, keepdims=True))
    alpha = jnp.exp2(m_prev - m_new)
    p = jnp.exp2(s - m_new)
    l_ref[g] = alpha * l_ref[g] + jnp.sum(p, axis=0, keepdims=True)
    acc_ref[g] = alpha * acc_ref[g] + jnp.dot(vt_blk, p.astype(BF16), preferred_element_type=F32)
    m_ref[g] = m_new


def _causal_sweep(qi, scores, softmax, sa_ref, sb_ref, diag_mask):
    def fill(ref, kb):
        for g, s in enumerate(scores(kb)):
            ref[g] = s

    fill(sa_ref, 0)

    def body(j, c):
        kb = 2 * j
        fill(sb_ref, kb + 1)
        softmax(sa_ref, kb, None)
        fill(sa_ref, kb + 2)
        softmax(sb_ref, kb + 1, None)
        return c

    lax.fori_loop(0, lax.shift_right_logical(qi, 1), body, 0)
    odd = (qi & 1) == 1

    @pl.when(jnp.logical_not(odd))
    def _():
        softmax(sa_ref, qi, diag_mask)

    @pl.when(odd)
    def _():
        fill(sb_ref, qi)
        softmax(sa_ref, qi - 1, None)
        softmax(sb_ref, qi, diag_mask)


def _init_stats(m_ref, l_ref, acc_ref):
    m_ref[...] = jnp.full(m_ref.shape, NEG_INF, F32)
    l_ref[...] = jnp.zeros(l_ref.shape, F32)
    acc_ref[...] = jnp.zeros(acc_ref.shape, F32)


def _causal_mask_t(tk, cols, tq):
    c = lax.broadcasted_iota(jnp.int32, (tk, cols), 0)
    r = lax.broadcasted_iota(jnp.int32, (tk, cols), 1)
    r = jnp.where(r >= tq, r - tq, r)
    return c <= r


def _da_attn_kernel(lq1_ref, lk1_ref, lq2_ref, lk2_ref, sub_ref, qt_ref, k_ref, vt_ref, o_ref,
                    qs_ref, m_ref, l_ref, acc_ref, sa_ref, sb_ref, *, tq, heads, lam_init):
    qi = pl.program_id(2)
    sub = lax.broadcasted_iota(jnp.int32, (LANES, tq), 0)
    for g in range(heads):
        qt = qt_ref[0, g * LANES:(g + 1) * LANES, :]
        zero = jnp.zeros_like(qt)
        qs_ref[g] = jnp.concatenate([jnp.where(sub < DA_HEAD_DIM, qt, zero),
                                     jnp.where(sub >= DA_HEAD_DIM, qt, zero)], axis=1)
    _init_stats(m_ref, l_ref, acc_ref)

    def scores(kb):
        off = pl.multiple_of(kb * tq, tq)
        return [jnp.dot(k_ref[0, pl.ds(off, tq), g * LANES:(g + 1) * LANES], qs_ref[g],
                        preferred_element_type=F32) for g in range(heads)]

    def softmax(s, kb, mask):
        for g in range(heads):
            _softmax_block(s[g], vt_ref[0, kb, g * LANES:(g + 1) * LANES, :], g, m_ref, l_ref, acc_ref, mask)

    _causal_sweep(qi, scores, softmax, sa_ref, sb_ref, _causal_mask_t(tq, 2 * tq, tq))

    lam = (jnp.exp(jnp.sum(lq1_ref[...] * lk1_ref[...], axis=1, keepdims=True))
           - jnp.exp(jnp.sum(lq2_ref[...] * lk2_ref[...], axis=1, keepdims=True)) + lam_init)
    for g in range(heads):
        ot = acc_ref[g] * (1.0 / l_ref[g])
        dd = (ot[:, :tq] - lam * ot[:, tq:]).T
        ms = jnp.mean(dd * dd, axis=-1, keepdims=True)
        o_ref[0, :, g * LANES:(g + 1) * LANES] = (
            dd * lax.rsqrt(ms + EPS) * sub_ref[...] * (1.0 - lam_init)).astype(BF16)


def _da_attn(qt, k, vt, lq1, lk1, lq2, lk2, subln, lam_init):
    b, s, d = k.shape
    heads = 4
    wb = heads * LANES
    tq = vt.shape[3]
    vec = lambda bi, hi, qi: (0, 0)
    return pl.pallas_call(
        functools.partial(_da_attn_kernel, tq=tq, heads=heads, lam_init=lam_init),
        grid=(b, d // wb, s // tq),
        in_specs=[pl.BlockSpec((1, DA_HEAD_DIM), vec)] * 4 + [
            pl.BlockSpec((1, LANES), vec),
            pl.BlockSpec((1, wb, tq), lambda bi, hi, qi: (bi, hi, qi)),
            pl.BlockSpec((1, s, wb), lambda bi, hi, qi: (bi, 0, hi)),
            pl.BlockSpec((1, s // tq, wb, tq), lambda bi, hi, qi: (bi, 0, hi, 0))],
        out_specs=pl.BlockSpec((1, tq, wb), lambda bi, hi, qi: (bi, qi, hi)),
        out_shape=jax.ShapeDtypeStruct((b, s, d), BF16),
        scratch_shapes=[pltpu.VMEM((heads, LANES, 2 * tq), BF16),
                        pltpu.VMEM((heads, 1, 2 * tq), F32),
                        pltpu.VMEM((heads, 1, 2 * tq), F32),
                        pltpu.VMEM((heads, LANES, 2 * tq), F32),
                        pltpu.VMEM((heads, tq, 2 * tq), F32),
                        pltpu.VMEM((heads, tq, 2 * tq), F32)],
        compiler_params=_cparams(("arbitrary", "arbitrary", "arbitrary")),
    )(lq1.reshape(1, -1), lk1.reshape(1, -1), lq2.reshape(1, -1), lk2.reshape(1, -1),
      subln.reshape(1, -1), qt, k, vt)


def _mla_attn_kernel(qt_ref, k_ref, vt_ref, o_ref, m_ref, l_ref, acc_ref, sa_ref, sb_ref, *, tq, heads):
    qi = pl.program_id(2)
    _init_stats(m_ref, l_ref, acc_ref)

    def scores(kb):
        off = pl.multiple_of(kb * tq, tq)
        return [jnp.dot(k_ref[0, pl.ds(off, tq), g * LANES:(g + 1) * LANES],
                        qt_ref[0, g * LANES:(g + 1) * LANES, :],
                        preferred_element_type=F32) for g in range(heads)]

    def softmax(s, kb, mask):
        for g in range(heads):
            _softmax_block(s[g], vt_ref[0, kb, g * MLA_V:(g + 1) * MLA_V, :], g, m_ref, l_ref, acc_ref, mask)

    _causal_sweep(qi, scores, softmax, sa_ref, sb_ref, _causal_mask_t(tq, tq, tq))
    for p in range(heads // 2):
        ot = jnp.concatenate([acc_ref[2 * p] * (1.0 / l_ref[2 * p]),
                              acc_ref[2 * p + 1] * (1.0 / l_ref[2 * p + 1])], axis=0)
        o_ref[0, :, p * LANES:(p + 1) * LANES] = ot.T.astype(BF16)


def _mla_attn(qt, k, vt):
    b, s, dk = k.shape
    heads = 8
    tq = vt.shape[3]
    return pl.pallas_call(
        functools.partial(_mla_attn_kernel, tq=tq, heads=heads),
        grid=(b, dk // (heads * LANES), s // tq),
        in_specs=[pl.BlockSpec((1, heads * LANES, tq), lambda bi, hi, qi: (bi, hi, qi)),
                  pl.BlockSpec((1, s, heads * LANES), lambda bi, hi, qi: (bi, 0, hi)),
                  pl.BlockSpec((1, s // tq, heads * MLA_V, tq), lambda bi, hi, qi: (bi, 0, hi, 0))],
        out_specs=pl.BlockSpec((1, tq, heads * MLA_V), lambda bi, hi, qi: (bi, qi, hi)),
        out_shape=jax.ShapeDtypeStruct((b, s, dk // LANES * MLA_V), BF16),
        scratch_shapes=[pltpu.VMEM((heads, 1, tq), F32),
                        pltpu.VMEM((heads, 1, tq), F32),
                        pltpu.VMEM((heads, MLA_V, tq), F32),
                        pltpu.VMEM((heads, tq, tq), F32),
                        pltpu.VMEM((heads, tq, tq), F32)],
        compiler_params=_cparams(("arbitrary", "arbitrary", "arbitrary")),
    )(qt, k, vt)


def _mla_pre_kernel(x_ref, mod_ref, pos_ref, g_ref, win_ref, qag_ref, kvag_ref, kpeg_ref,
                    wuq_ref, wuk_ref, wuv_ref, qg_ref, kg_ref, invf_ref, sgn_ref,
                    segq_ref, expq_ref, segk_ref, expk_ref, place_ref,
                    q_out, k_out, v_out, *, scale):
    h = _adaln_h(x_ref[0], mod_ref, g_ref, 0, 1)
    lat = jnp.dot(h.astype(BF16), win_ref[...], preferred_element_type=F32)
    cq = lat[:, :MLA_Q_RANK]
    ckv = lat[:, MLA_Q_RANK:MLA_Q_RANK + MLA_KV_RANK]
    kpe = lat[:, MLA_Q_RANK + MLA_KV_RANK:]
    cqn = (cq * lax.rsqrt(jnp.mean(cq * cq, axis=-1, keepdims=True) + EPS) * qag_ref[...]).astype(BF16)
    ckvn = (ckv * lax.rsqrt(jnp.mean(ckv * ckv, axis=-1, keepdims=True) + EPS) * kvag_ref[...]).astype(BF16)
    kpen = kpe * lax.rsqrt(jnp.sum(kpe * kpe, axis=-1, keepdims=True) * (1.0 / MLA_ROPE) + EPS) * kpeg_ref[...]

    q = jnp.dot(cqn, wuq_ref[...], preferred_element_type=F32)
    kn = jnp.dot(ckvn, wuk_ref[...], preferred_element_type=F32)
    _store_vt(v_out, jnp.dot(ckvn, wuv_ref[...], preferred_element_type=F32))

    w = q.shape[1]
    reps = w // LANES
    ang = pos_ref[0] * invf_ref[...]
    cosf = jnp.tile(jnp.cos(ang), (1, reps))
    sinf = jnp.tile(jnp.sin(ang) * sgn_ref[...], (1, reps))
    lane = lax.broadcasted_iota(jnp.int32, (1, w), 1)
    half = MLA_ROPE // 2
    low = (lane & (MLA_ROPE - 1)) < half

    qn = q * _seg_rsqrt_bcast(q, segq_ref, expq_ref) * qg_ref[...]
    q_out[0] = (_rope(qn, cosf, sinf, low, half, w) * scale).T.astype(BF16)
    knn = kn * _seg_rsqrt_bcast(kn, segk_ref, expk_ref) * kg_ref[...]
    kfull = knn + jnp.dot(_split_bf16(kpen), place_ref[...], preferred_element_type=F32)
    k_out[0] = _rope(kfull, cosf, sinf, low, half, w).astype(BF16)


def _mla_pre(x, mod, posf, norm_g, w_in, q_a_norm, kv_a_norm, w_uq, w_ukv,
             q_nope_norm, q_pe_norm, k_nope_norm, k_pe_norm):
    b, s, d = x.shape
    t = min(TOK_TILE, s)
    nh, nope, rope, vd = MLA_HEADS, MLA_NOPE, MLA_ROPE, MLA_V
    w = nh * LANES
    in_w = MLA_Q_RANK + MLA_KV_RANK + rope
    w_in_p = jnp.pad(w_in, ((0, 0), (0, MLA_IN_PAD - in_w))).astype(BF16)
    wuq = w_uq.reshape(MLA_Q_RANK, nh, nope + rope)
    wuq_p = jnp.pad(wuq, ((0, 0), (0, 0), (0, LANES - nope - rope))).reshape(MLA_Q_RANK, w).astype(BF16)
    wukv = w_ukv.reshape(MLA_KV_RANK, nh, nope + vd)
    wuk_p = jnp.pad(wukv[:, :, :nope], ((0, 0), (0, 0), (0, LANES - nope))).reshape(MLA_KV_RANK, w).astype(BF16)
    wuv = wukv[:, :, nope:].reshape(MLA_KV_RANK, nh * vd).astype(BF16)
    zpad = jnp.zeros((LANES - nope - rope,), F32)
    qg = jnp.tile(jnp.concatenate([q_nope_norm, q_pe_norm, zpad]), nh).reshape(1, w)
    kg = jnp.tile(jnp.concatenate([k_nope_norm, jnp.zeros((LANES - nope,), F32)]), nh).reshape(1, w)
    kpeg = jnp.concatenate([k_pe_norm, jnp.zeros((LANES - rope,), F32)]).reshape(1, LANES)

    lane = np.arange(LANES)
    in_rope = (lane >= nope) & (lane < nope + rope)
    invf = (ROPE_THETA ** (-np.arange(0, rope, 2, dtype=np.float32) / rope)).astype(np.float32)
    invf_row = np.where(in_rope, invf[(lane - nope) % (rope // 2)], 0.0).astype(np.float32)[None, :]
    sgn_row = np.where(in_rope, np.where((lane - nope) < rope // 2, -1.0, 1.0), 0.0).astype(np.float32)[None, :]

    col = np.arange(w)
    hd, off = col // LANES, col % LANES
    segq = np.zeros((w, LANES), np.float32)
    expq = np.zeros((2 * LANES, w), np.float32)
    is_nope, is_pe = off < nope, (off >= nope) & (off < nope + rope)
    segq[col[is_nope], 2 * hd[is_nope]] = 1.0 / nope
    segq[col[is_pe], 2 * hd[is_pe] + 1] = 1.0 / rope
    for base in (0, LANES):
        expq[base + 2 * hd[is_nope], col[is_nope]] = 1.0
        expq[base + 2 * hd[is_pe] + 1, col[is_pe]] = 1.0
    segk = np.zeros((w, LANES), np.float32)
    expk = np.zeros((2 * LANES, w), np.float32)
    segk[col[is_nope], hd[is_nope]] = 1.0 / nope
    place = np.zeros((2 * LANES, w), np.float32)
    for base in (0, LANES):
        expk[base + hd[is_nope], col[is_nope]] = 1.0
        place[base + off[is_pe] - nope, col[is_pe]] = 1.0
    assert 2 * nh <= LANES

    row = lambda bi, ti: (0, 0)
    tok = lambda bi, ti: (bi, ti, 0)
    full = lambda a: pl.BlockSpec(a.shape, row)
    consts = [norm_g.reshape(1, d), w_in_p, q_a_norm.reshape(1, -1), kv_a_norm.reshape(1, -1), kpeg,
              wuq_p, wuk_p, wuv, qg, kg, jnp.asarray(invf_row), jnp.asarray(sgn_row),
              jnp.asarray(segq, BF16), jnp.asarray(expq, BF16), jnp.asarray(segk, BF16),
              jnp.asarray(expk, BF16), jnp.asarray(place, BF16)]
    tk = min(ATT_TILE, t)
    return pl.pallas_call(
        functools.partial(_mla_pre_kernel, scale=(nope + rope) ** -0.5 * LOG2E),
        grid=(b, s // t),
        in_specs=[pl.BlockSpec((1, t, d), tok),
                  pl.BlockSpec((1, 6, d), lambda bi, ti: (bi, 0, 0)),
                  pl.BlockSpec((1, t, 1), tok)] + [full(a) for a in consts],
        out_specs=[pl.BlockSpec((1, w, t), lambda bi, ti: (bi, 0, ti)),
                   pl.BlockSpec((1, t, w), tok),
                   pl.BlockSpec((1, t // tk, nh * vd, tk), lambda bi, ti: (bi, ti, 0, 0))],
        out_shape=[jax.ShapeDtypeStruct((b, w, s), BF16),
                   jax.ShapeDtypeStruct((b, s, w), BF16),
                   jax.ShapeDtypeStruct((b, s // tk, nh * vd, tk), BF16)],
        compiler_params=_cparams(("arbitrary", "arbitrary")),
    )(x, mod, posf, *consts)


def _post_kernel(o_ref, wo_ref, x_ref, mod_ref, g_ref, wr_ref, bsel_ref,
                 x_out, h_out, route_out, cnt_out, run_ref, *, d):
    first = (pl.program_id(0) == 0) & (pl.program_id(1) == 0)

    @pl.when(first)
    def _():
        run_ref[...] = jnp.zeros_like(run_ref)

    y = jnp.dot(o_ref[0], wo_ref[...], preferred_element_type=F32)
    x = x_ref[0] + mod_ref[0, 2:3, :] * y
    x_out[0] = x
    h = _adaln_h(x, mod_ref, g_ref, 3, 4)
    t = h.shape[0]

    hh = h.astype(BF16)
    hl = (h - hh.astype(F32)).astype(BF16)
    logits = (jnp.dot(hh, wr_ref[0], preferred_element_type=F32)
              + jnp.dot(hl, wr_ref[0], preferred_element_type=F32)
              + jnp.dot(hh, wr_ref[1], preferred_element_type=F32))

    lane_i = lax.broadcasted_iota(jnp.int32, (t, LANES), 1)
    lane = lane_i.astype(F32)
    big = float(LANES)
    ng, ne = N_GROUPS, EXPERTS_PER_GROUP

    def first_argmax(val):
        mx = jnp.max(val, axis=1, keepdims=True)
        return jnp.min(jnp.where(val == mx, lane, big), axis=1, keepdims=True)

    def pick(val, idx):
        return jnp.sum(jnp.where(lane == idx, val, 0.0), axis=1, keepdims=True)

    gmask = lane_i < ng
    gl = jnp.where(gmask, logits, NEG_INF)
    ge = jnp.exp(gl - jnp.max(gl, axis=1, keepdims=True))
    gprob = ge / jnp.sum(ge, axis=1, keepdims=True)
    gidx = first_argmax(jnp.where(gmask, gprob + bsel_ref[...], NEG_INF))
    ggate = pick(gprob, gidx)

    base = ng + ne * gidx
    emask = (lane >= base) & (lane < base + ne)
    el = jnp.where(emask, logits, NEG_INF)
    ee = jnp.exp(el - jnp.max(el, axis=1, keepdims=True))
    eprob = ee / jnp.sum(ee, axis=1, keepdims=True)
    sel = jnp.where(emask, eprob + bsel_ref[...], NEG_INF)
    i1 = first_argmax(sel)
    i2 = first_argmax(jnp.where(lane == i1, NEG_INF, sel))
    p1 = pick(eprob, i1)
    p2 = pick(eprob, i2)
    psum = p1 + p2
    w1 = p1 / psum * ggate
    w2 = p2 / psum * ggate
    e1 = i1 - base
    e2 = i2 - base
    swap = e2 < e1
    lo = jnp.where(swap, e2, e1)
    hi = jnp.where(swap, e1, e2)
    wa = jnp.where(swap, w2, w1)
    wb = jnp.where(swap, w1, w2)
    combo = gidx * float(N_PAIRS) + lo * (2.0 * ne - 1.0 - lo) * 0.5 + (hi - lo - 1.0)

    _pack_token_tiles(h_out, h, wa, wb)

    onehot = lane == combo
    r_i = lax.broadcasted_iota(jnp.int32, (t, t), 0)
    c_i = lax.broadcasted_iota(jnp.int32, (t, t), 1)
    tri = jnp.where(c_i < r_i, 1.0, 0.0).astype(BF16)
    oh = jnp.where(onehot, 1.0, 0.0)
    before = jnp.dot(tri, oh.astype(BF16), preferred_element_type=F32) + run_ref[...]
    rank = jnp.sum(jnp.where(onehot, before, 0.0), axis=1, keepdims=True)
    run = run_ref[...] + jnp.sum(oh, axis=0, keepdims=True)
    run_ref[...] = run
    cnt_out[...] = run
    route_out[0] = jnp.where(lane_i == 0, combo, jnp.where(lane_i == 1, rank, 0.0))


def _post(o, w_o, x, mod, norm_g, w_group, b_group, w_router, b_router):
    b, s, d = x.shape
    t = min(TOK_TILE, s)
    do = o.shape[2]
    ng, ne = N_GROUPS, EXPERTS_PER_GROUP
    wr = jnp.pad(jnp.concatenate([w_group, w_router], axis=1), ((0, 0), (0, LANES - ng - ng * ne)))
    wr_hi = wr.astype(BF16)
    wr_lo = (wr - wr_hi.astype(F32)).astype(BF16)
    wr2 = jnp.stack([wr_hi, wr_lo])
    bsel = jnp.pad(jnp.concatenate([b_group, b_router.reshape(-1)]), (0, LANES - ng - ng * ne)).reshape(1, LANES)
    row = lambda bi, ti: (0, 0)
    tok = lambda bi, ti: (bi, ti, 0)
    return pl.pallas_call(
        functools.partial(_post_kernel, d=d),
        grid=(b, s // t),
        in_specs=[pl.BlockSpec((1, t, do), tok),
                  pl.BlockSpec((do, d), row),
                  pl.BlockSpec((1, t, d), tok),
                  pl.BlockSpec((1, 6, d), lambda bi, ti: (bi, 0, 0)),
                  pl.BlockSpec((1, d), row),
                  pl.BlockSpec((2, d, LANES), lambda bi, ti: (0, 0, 0)),
                  pl.BlockSpec((1, LANES), row)],
        out_specs=[pl.BlockSpec((1, t, d), tok),
                   pl.BlockSpec((1, t * SUBLANES, LANES), tok),
                   pl.BlockSpec((1, t, LANES), tok),
                   pl.BlockSpec((1, LANES), row)],
        out_shape=[jax.ShapeDtypeStruct((b, s, d), F32),
                   jax.ShapeDtypeStruct((b, s * SUBLANES, LANES), jnp.uint32),
                   jax.ShapeDtypeStruct((b, s, LANES), F32),
                   jax.ShapeDtypeStruct((1, LANES), F32)],
        scratch_shapes=[pltpu.VMEM((1, LANES), F32)],
        compiler_params=_cparams(("arbitrary", "arbitrary")),
    )(o, w_o.astype(BF16), x, mod, norm_g.reshape(1, d), wr2, bsel)


def _tok_rows(s, n):
    return pl.ds(s, n, stride=SUBLANES)


def _tok_tile(tok):
    return pl.ds(pl.multiple_of(tok * SUBLANES, SUBLANES), SUBLANES)


def _pack_token_tiles(p_out, h, wa, wb):
    t, d = h.shape
    half = d // 2
    bits = pltpu.bitcast(h.astype(BF16).astype(F32), jnp.uint32)
    word = bits[:, :half] | (bits[:, half:] >> 16)
    nw = half // LANES
    for s in range(nw):
        p_out[0, _tok_rows(s, t), :] = word[:, s * LANES:(s + 1) * LANES]
    lane = lax.broadcasted_iota(jnp.int32, (t, LANES), 1)
    wrow = jnp.where(lane == 0, wa, jnp.where(lane == 1, wb, 0.0))
    p_out[0, _tok_rows(nw, t), :] = pltpu.bitcast(wrow, jnp.uint32)
    for s in range(nw + 1, SUBLANES):
        p_out[0, _tok_rows(s, t), :] = jnp.zeros((t, LANES), jnp.uint32)


def _unpack_token_tiles(hs_ref, d):
    nw = d // 2 // LANES
    tm = hs_ref.shape[0] // SUBLANES
    words = [hs_ref[_tok_rows(s, tm), :] for s in range(nw)]
    hi = [pltpu.bitcast(w & jnp.uint32(0xFFFF0000), F32).astype(BF16) for w in words]
    lo = [pltpu.bitcast(w << 16, F32).astype(BF16) for w in words]
    wrow = pltpu.bitcast(hs_ref[_tok_rows(nw, tm), :], F32)
    return jnp.concatenate(hi + lo, axis=1), wrow[:, 0:1], wrow[:, 1:2]


def _row_copy(src_ref, src_row, dst_ref, dst_row, sem):
    return pltpu.make_async_copy(src_ref.at[_tok_tile(src_row), :], dst_ref.at[_tok_tile(dst_row), :], sem)


def _start_rows(rows, make_copy):
    def start(i, c):
        for u in range(ROW_UNROLL):
            make_copy(i * ROW_UNROLL + u).start(priority=u % 2)
        return c

    lax.fori_loop(0, rows // ROW_UNROLL, start, 0)


def _wait_rows(rows, make_copy):
    def wait(i, c):
        for u in range(ROW_UNROLL):
            make_copy(0).wait()
        return c

    lax.fori_loop(0, rows // ROW_UNROLL, wait, 0)


def _dispatch_kernel(dest_ref, h_ref, init_ref, hs_ref, sem, *, rows):
    del init_ref
    copy = lambda r: _row_copy(h_ref, r, hs_ref, dest_ref[r], sem)
    _start_rows(rows, copy)
    _wait_rows(rows, copy)


def _dispatch(dest, h2, n_rows):
    n = h2.shape[0] // SUBLANES
    rows = min(ROW_TILE, n)
    return pl.pallas_call(
        functools.partial(_dispatch_kernel, rows=rows),
        grid=(n // rows,),
        in_specs=[pl.BlockSpec((rows,), lambda i: (i,), memory_space=pltpu.SMEM),
                  pl.BlockSpec((rows * SUBLANES, LANES), lambda i: (i, 0)),
                  pl.BlockSpec(memory_space=pl.ANY)],
        out_specs=pl.BlockSpec(memory_space=pl.ANY),
        out_shape=jax.ShapeDtypeStruct((n_rows * SUBLANES, LANES), jnp.uint32),
        scratch_shapes=[pltpu.SemaphoreType.DMA(())],
        input_output_aliases={2: 0},
        compiler_params=_cparams(("arbitrary",)),
    )(dest, h2, jnp.zeros((n_rows * SUBLANES, LANES), jnp.uint32))


def _moe_kernel(elo_ref, ehi_ref, blk_ref, nact_ref, hs_ref, w1a_ref, w3a_ref, w2a_ref,
                w1b_ref, w3b_ref, w2b_ref, y_ref, *, d):
    del elo_ref, ehi_ref, blk_ref
    active = pl.program_id(0) < nact_ref[0]

    @pl.when(jnp.logical_not(active))
    def _():
        y_ref[...] = jnp.zeros_like(y_ref)

    @pl.when(active)
    def _():
        x, wa, wb = _unpack_token_tiles(hs_ref, d)

        def expert(w1_ref, w3_ref, wgt):
            a = jnp.dot(x, w1_ref[0], preferred_element_type=F32)
            g = jnp.dot(x, w3_ref[0], preferred_element_type=F32)
            return (a * jax.nn.sigmoid(a) * g * wgt).astype(BF16)

        y = (jnp.dot(expert(w1a_ref, w3a_ref, wa), w2a_ref[0], preferred_element_type=F32)
             + jnp.dot(expert(w1b_ref, w3b_ref, wb), w2b_ref[0], preferred_element_type=F32))
        for s in range(d // LANES):
            y_ref[_tok_rows(s, y.shape[0]), :] = y[:, s * LANES:(s + 1) * LANES]


def _moe(hs, e_lo, e_hi, blk, nact, w1, w3, w2):
    n_rows = hs.shape[0] // SUBLANES
    d = w1.shape[1]
    ff = w1.shape[2]
    tm = MOE_TILE
    n_tiles = n_rows // tm
    assert d == SUBLANES * LANES
    wspec = lambda shape, which: pl.BlockSpec(shape, (lambda j, lo, hi, bk, na: (lo[j], 0, 0)) if which == 0
                                              else (lambda j, lo, hi, bk, na: (hi[j], 0, 0)))
    grid_spec = pltpu.PrefetchScalarGridSpec(
        num_scalar_prefetch=4,
        grid=(n_tiles,),
        in_specs=[pl.BlockSpec((tm * SUBLANES, LANES), lambda j, lo, hi, bk, na: (bk[j], 0)),
                  wspec((1, d, ff), 0), wspec((1, d, ff), 0), wspec((1, ff, d), 0),
                  wspec((1, d, ff), 1), wspec((1, d, ff), 1), wspec((1, ff, d), 1)],
        out_specs=pl.BlockSpec((tm * SUBLANES, LANES), lambda j, lo, hi, bk, na: (j, 0)),
    )
    return pl.pallas_call(
        functools.partial(_moe_kernel, d=d),
        grid_spec=grid_spec,
        out_shape=jax.ShapeDtypeStruct((n_rows * SUBLANES, LANES), F32),
        compiler_params=_cparams(("arbitrary",)),
    )(e_lo, e_hi, blk, nact, hs, w1, w3, w2, w1, w3, w2)


def _combine_kernel(dest_ref, next_ref, x_ref, gate_ref, y_ref, o_ref, buf_ref, sem, *, rows):
    i = pl.program_id(0)
    slot = i & 1

    def gather(d_ref, sl):
        return lambda r: _row_copy(y_ref, d_ref[r], buf_ref.at[sl], r, sem.at[sl])

    @pl.when(i == 0)
    def _():
        _start_rows(rows, gather(dest_ref, 0))

    @pl.when(i + 1 < pl.num_programs(0))
    def _():
        _start_rows(rows, gather(next_ref, 1 - slot))

    _wait_rows(rows, gather(dest_ref, slot))
    for s in range(SUBLANES):
        cols = slice(s * LANES, (s + 1) * LANES)
        o_ref[:, cols] = x_ref[:, cols] + gate_ref[0, :, cols] * buf_ref[slot, _tok_rows(s, rows), :]


def _combine(dest, x2, gate, y, seq):
    n, d = x2.shape
    rows = min(ROW_TILE, seq)
    per_seq = seq // rows
    steps = n // rows
    return pl.pallas_call(
        functools.partial(_combine_kernel, rows=rows),
        grid=(steps,),
        in_specs=[pl.BlockSpec((rows,), lambda i: (i,), memory_space=pltpu.SMEM),
                  pl.BlockSpec((rows,), lambda i: (jnp.minimum(i + 1, steps - 1),), memory_space=pltpu.SMEM),
                  pl.BlockSpec((rows, d), lambda i: (i, 0)),
                  pl.BlockSpec((1, 1, d), lambda i: (i // per_seq, 0, 0)),
                  pl.BlockSpec(memory_space=pl.ANY)],
        out_specs=pl.BlockSpec((rows, d), lambda i: (i, 0)),
        out_shape=jax.ShapeDtypeStruct((n, d), F32),
        scratch_shapes=[pltpu.VMEM((2, rows * SUBLANES, LANES), F32), pltpu.SemaphoreType.DMA((2,))],
        compiler_params=_cparams(("arbitrary",)),
    )(dest, dest, x2, gate, y)


def _moe_layer(x_new, h2, route, counts, gate_f, w1, w3, w2):
    b, s, d = x_new.shape
    n = b * s
    tm = MOE_TILE
    n_tiles = n // tm + N_COMBOS
    combo = route[:, :, 0].reshape(n).astype(jnp.int32)
    rank = route[:, :, 1].reshape(n).astype(jnp.int32)
    cnt = counts[0, :N_COMBOS].astype(jnp.int32)
    tiles_per = (cnt + tm - 1) // tm
    tile_end = jnp.cumsum(tiles_per)
    row_off = (tile_end - tiles_per) * tm
    ids = jnp.arange(N_COMBOS, dtype=jnp.int32)

    def lookup(table, idx):
        return jnp.sum(jnp.where(idx[:, None] == ids[None, :], table[None, :], 0), axis=1)

    dest = lookup(row_off, combo) + rank
    nact = tile_end[-1:]
    blk = jnp.minimum(jnp.arange(n_tiles, dtype=jnp.int32), nact[0] - 1)
    tile_combo = jnp.sum((tile_end[None, :] <= blk[:, None]).astype(jnp.int32), axis=1)
    e_lo = lookup(jnp.asarray(_COMBO_LO), tile_combo)
    e_hi = lookup(jnp.asarray(_COMBO_HI), tile_combo)

    hs = _dispatch(dest, h2.reshape(n * SUBLANES, LANES), n_tiles * tm)
    ne = w1.shape[0] * w1.shape[1]
    y = _moe(hs, e_lo, e_hi, blk, nact.astype(jnp.int32),
             w1.reshape(ne, d, -1).astype(BF16), w3.reshape(ne, d, -1).astype(BF16),
             w2.reshape(ne, -1, d).astype(BF16))
    return _combine(dest, x_new.reshape(n, d), gate_f, y, s).reshape(b, s, d)


def kernel(x, c, positions, ada_w, ada_b, norm_mix, norm_ffn, da_w_qkv, da_q_norm, da_k_norm, da_lambda_q1, da_lambda_k1, da_lambda_q2, da_lambda_k2, da_subln, da_w_o, mla_w_in, mla_q_a_norm, mla_kv_a_norm, mla_w_uq, mla_w_ukv, mla_q_nope_norm, mla_q_pe_norm, mla_k_nope_norm, mla_k_pe_norm, mla_w_o, moe_w_group, moe_b_group, moe_w_router, moe_b_router, moe_w1, moe_w3, moe_w2):
    b, s, d = x.shape
    depth = ada_w.shape[0]
    mod_all = _modulation(c, ada_w, ada_b).reshape(depth, b, 6, d)
    posf = positions.astype(F32).reshape(b, s, 1)
    for i in range(depth):
        mod = mod_all[i]
        j = i // 2
        if i % 2 == 0:
            lam_init = 0.8 - 0.6 * math.exp(-0.3 * i)
            q, k, v = _da_pre(x, mod, posf, norm_mix[i], da_w_qkv[j], da_q_norm[j], da_k_norm[j])
            o = _da_attn(q, k, v, da_lambda_q1[j], da_lambda_k1[j], da_lambda_q2[j], da_lambda_k2[j],
                         da_subln[j], lam_init)
            w_o = da_w_o[j]
        else:
            q, k, v = _mla_pre(x, mod, posf, norm_mix[i], mla_w_in[j], mla_q_a_norm[j], mla_kv_a_norm[j],
                               mla_w_uq[j], mla_w_ukv[j], mla_q_nope_norm[j], mla_q_pe_norm[j],
                               mla_k_nope_norm[j], mla_k_pe_norm[j])
            o = _mla_attn(q, k, v)
            w_o = mla_w_o[j]
        x_new, h2, route, counts = _post(o, w_o, x, mod, norm_ffn[i], moe_w_group[i], moe_b_group[i],
                                         moe_w_router[i], moe_b_router[i])
        x = _moe_layer(x_new, h2, route, counts, mod[:, 5:6, :], moe_w1[i], moe_w3[i], moe_w2[i])
    return x
```

```python
import functools
import math

import numpy as np
import jax
import jax.numpy as jnp
from jax import lax
from jax.experimental import pallas as pl
from jax.experimental.pallas import tpu as pltpu

F32 = jnp.float32
BF16 = jnp.bfloat16

ROPE_THETA = 10000.0
EPS = 1e-6
NEG_INF = -1e30
LOG2E = math.log2(math.e)
DA_HEAD_DIM = 64
MLA_HEADS = 16
MLA_NOPE = 64
MLA_ROPE = 32
MLA_V = 64
MLA_Q_RANK = 384
MLA_KV_RANK = 256
N_GROUPS = 4
EXPERTS_PER_GROUP = 8
EXPERT_FF = 256

LANES = 128
VMEM_LIMIT = 56 * 1024 * 1024
TOK_TILE = 512
ATT_TILE = 256
MOE_TILE = 256
ROW_TILE = 1024
ROW_UNROLL = 8
SUBLANES = 8

N_PAIRS = EXPERTS_PER_GROUP * (EXPERTS_PER_GROUP - 1) // 2
N_COMBOS = N_GROUPS * N_PAIRS


def _combo_tables():
    lo_t, hi_t = [], []
    for g in range(N_GROUPS):
        for lo in range(EXPERTS_PER_GROUP):
            for hi in range(lo + 1, EXPERTS_PER_GROUP):
                lo_t.append(g * EXPERTS_PER_GROUP + lo)
                hi_t.append(g * EXPERTS_PER_GROUP + hi)
    return np.asarray(lo_t, np.int32), np.asarray(hi_t, np.int32)


_COMBO_LO, _COMBO_HI = _combo_tables()


def _cparams(sem):
    return pltpu.CompilerParams(dimension_semantics=sem, vmem_limit_bytes=VMEM_LIMIT)


def _adaln_h(x, mod_ref, g_ref, shift_row, scale_row):
    ms = jnp.mean(x * x, axis=-1, keepdims=True)
    h = x * lax.rsqrt(ms + EPS) * g_ref[...]
    return h * (1.0 + mod_ref[0, scale_row:scale_row + 1, :]) + mod_ref[0, shift_row:shift_row + 1, :]


def _mod_kernel(c_ref, w_ref, b_ref, o_ref):
    c = c_ref[...]
    cond = (c * jax.nn.sigmoid(c)).astype(BF16)
    o_ref[0] = jnp.dot(cond, w_ref[0].astype(BF16), preferred_element_type=F32) + b_ref[0]


def _modulation(c, ada_w, ada_b):
    depth, d, n6 = ada_w.shape
    b = c.shape[0]
    tn = 1536
    return pl.pallas_call(
        _mod_kernel,
        grid=(depth, n6 // tn),
        in_specs=[pl.BlockSpec((b, d), lambda i, j: (0, 0)),
                  pl.BlockSpec((1, d, tn), lambda i, j: (i, 0, j)),
                  pl.BlockSpec((1, 1, tn), lambda i, j: (i, 0, j))],
        out_specs=pl.BlockSpec((1, b, tn), lambda i, j: (i, 0, j)),
        out_shape=jax.ShapeDtypeStruct((depth, b, n6), F32),
        compiler_params=_cparams(("arbitrary", "arbitrary")),
    )(c, ada_w, ada_b.reshape(depth, 1, n6))


def _da_pre_kernel(x_ref, mod_ref, pos_ref, g_ref, wt_ref, qg_ref, kg_ref, invf_ref,
                   q_out, k_out, v_out, *, d, scale):
    h = _adaln_h(x_ref[0], mod_ref, g_ref, 0, 1)
    t = h.shape[0]
    qkv = jnp.dot(wt_ref[...], h.T.astype(BF16), preferred_element_type=F32)
    vt = qkv[2 * d:]
    tk = v_out.shape[3]
    for j in range(v_out.shape[1]):
        v_out[0, j] = vt[:, j * tk:(j + 1) * tk].astype(BF16)

    dh = DA_HEAD_DIM
    half = dh // 2
    reps = t // LANES
    ang = jnp.tile(invf_ref[...], (1, reps)) * pos_ref[0]
    cos = jnp.cos(ang)[None, None]
    sin = jnp.sin(ang)[None, None]

    def norm_rope(z, gain_ref):
        z3 = z.reshape(d // dh, dh, t)
        r = lax.rsqrt(jnp.mean(z3 * z3, axis=1, keepdims=True) + EPS)
        z4 = (z3 * r).reshape(d // dh, 2, half, t) * jnp.tile(gain_ref[...], (1, reps)).reshape(1, 2, half, t)
        z1 = z4[:, 0:1]
        z2 = z4[:, 1:2]
        return jnp.concatenate([z1 * cos - z2 * sin, z2 * cos + z1 * sin], axis=1).reshape(d, t)

    q_out[0] = (norm_rope(qkv[:d], qg_ref) * scale).astype(BF16)
    k_out[0] = norm_rope(qkv[d:2 * d], kg_ref).T.astype(BF16)


def _da_pre(x, mod, posr, norm_g, w_qkv, q_norm, k_norm):
    b, s, d = x.shape
    t = min(TOK_TILE, s)
    dh = DA_HEAD_DIM
    invf = (ROPE_THETA ** (-np.arange(0, dh, 2, dtype=np.float32) / dh)).astype(np.float32)
    lanes = lambda v: jnp.broadcast_to(jnp.asarray(v, F32)[:, None], (v.shape[0], LANES))
    row = lambda bi, ti: (0, 0)
    tok = lambda bi, ti: (bi, ti, 0)
    tk = min(ATT_TILE, t)
    return pl.pallas_call(
        functools.partial(_da_pre_kernel, d=d, scale=dh ** -0.5 * LOG2E),
        grid=(b, s // t),
        in_specs=[pl.BlockSpec((1, t, d), tok),
                  pl.BlockSpec((1, 6, d), lambda bi, ti: (bi, 0, 0)),
                  pl.BlockSpec((1, 1, t), lambda bi, ti: (bi, 0, ti)),
                  pl.BlockSpec((1, d), row),
                  pl.BlockSpec((3 * d, d), row),
                  pl.BlockSpec((dh, LANES), row),
                  pl.BlockSpec((dh, LANES), row),
                  pl.BlockSpec((dh // 2, LANES), row)],
        out_specs=[pl.BlockSpec((1, d, t), lambda bi, ti: (bi, 0, ti)),
                   pl.BlockSpec((1, t, d), tok),
                   pl.BlockSpec((1, t // tk, d, tk), lambda bi, ti: (bi, ti, 0, 0))],
        out_shape=[jax.ShapeDtypeStruct((b, d, s), BF16),
                   jax.ShapeDtypeStruct((b, s, d), BF16),
                   jax.ShapeDtypeStruct((b, s // tk, d, tk), BF16)],
        compiler_params=_cparams(("arbitrary", "arbitrary")),
    )(x, mod, posr, norm_g.reshape(1, d), w_qkv.T.astype(BF16), lanes(q_norm), lanes(k_norm), lanes(invf))


def _softmax_block(s, vt_blk, g, m_ref, l_ref, acc_ref, mask):
    if mask is not None:
        s = jnp.where(mask, s, NEG_INF)
    m_prev = m_ref[g]
    m_new = jnp.maximum(m_prev, jnp.max(s, axis=0, keepdims=True))
    alpha = jnp.exp2(m_prev - m_new)
    p = jnp.exp2(s - m_new)
    l_ref[g] = alpha * l_ref[g] + jnp.sum(p, axis=0, keepdims=True)
    acc_ref[g] = alpha * acc_ref[g] + jnp.dot(vt_blk, p.astype(BF16), preferred_element_type=F32)
    m_ref[g] = m_new


def _causal_sweep(qi, scores, softmax, sa_ref, sb_ref, diag_mask):
    def fill(ref, kb):
        for g, s in enumerate(scores(kb)):
            ref[g] = s

    fill(sa_ref, 0)

    def body(j, c):
        kb = 2 * j
        fill(sb_ref, kb + 1)
        softmax(sa_ref, kb, None)
        fill(sa_ref, kb + 2)
        softmax(sb_ref, kb + 1, None)
        return c

    lax.fori_loop(0, lax.shift_right_logical(qi, 1), body, 0)
    odd = (qi & 1) == 1

    @pl.when(jnp.logical_not(odd))
    def _():
        softmax(sa_ref, qi, diag_mask)

    @pl.when(odd)
    def _():
        fill(sb_ref, qi)
        softmax(sa_ref, qi - 1, None)
        softmax(sb_ref, qi, diag_mask)


def _init_stats(m_ref, l_ref, acc_ref):
    m_ref[...] = jnp.full(m_ref.shape, NEG_INF, F32)
    l_ref[...] = jnp.zeros(l_ref.shape, F32)
    acc_ref[...] = jnp.zeros(acc_ref.shape, F32)


def _causal_mask_t(tk, cols, tq):
    c = lax.broadcasted_iota(jnp.int32, (tk, cols), 0)
    r = lax.broadcasted_iota(jnp.int32, (tk, cols), 1)
    r = jnp.where(r >= tq, r - tq, r)
    return c <= r


def _da_attn_kernel(lq1_ref, lk1_ref, lq2_ref, lk2_ref, sub_ref, qt_ref, k_ref, vt_ref, o_ref,
                    qs_ref, m_ref, l_ref, acc_ref, sa_ref, sb_ref, *, tq, heads, lam_init):
    qi = pl.program_id(2)
    sub = lax.broadcasted_iota(jnp.int32, (LANES, tq), 0)
    for g in range(heads):
        qt = qt_ref[0, g * LANES:(g + 1) * LANES, :]
        zero = jnp.zeros_like(qt)
        qs_ref[g] = jnp.concatenate([jnp.where(sub < DA_HEAD_DIM, qt, zero),
                                     jnp.where(sub >= DA_HEAD_DIM, qt, zero)], axis=1)
    _init_stats(m_ref, l_ref, acc_ref)

    def scores(kb):
        off = pl.multiple_of(kb * tq, tq)
        return [jnp.dot(k_ref[0, pl.ds(off, tq), g * LANES:(g + 1) * LANES], qs_ref[g],
                        preferred_element_type=F32) for g in range(heads)]

    def softmax(s, kb, mask):
        for g in range(heads):
            _softmax_block(s[g], vt_ref[0, kb, g * LANES:(g + 1) * LANES, :], g, m_ref, l_ref, acc_ref, mask)

    _causal_sweep(qi, scores, softmax, sa_ref, sb_ref, _causal_mask_t(tq, 2 * tq, tq))

    lam = (jnp.exp(jnp.sum(lq1_ref[...] * lk1_ref[...], axis=1, keepdims=True))
           - jnp.exp(jnp.sum(lq2_ref[...] * lk2_ref[...], axis=1, keepdims=True)) + lam_init)
    for g in range(heads):
        ot = acc_ref[g] * (1.0 / l_ref[g])
        dd = (ot[:, :tq] - lam * ot[:, tq:]).T
        ms = jnp.mean(dd * dd, axis=-1, keepdims=True)
        o_ref[0, :, g * LANES:(g + 1) * LANES] = (
            dd * lax.rsqrt(ms + EPS) * sub_ref[...] * (1.0 - lam_init)).astype(BF16)


def _da_attn(qt, k, vt, lq1, lk1, lq2, lk2, subln, lam_init):
    b, s, d = k.shape
    heads = 4
    wb = heads * LANES
    tq = vt.shape[3]
    vec = lambda bi, hi, qi: (0, 0)
    return pl.pallas_call(
        functools.partial(_da_attn_kernel, tq=tq, heads=heads, lam_init=lam_init),
        grid=(b, d // wb, s // tq),
        in_specs=[pl.BlockSpec((1, DA_HEAD_DIM), vec)] * 4 + [
            pl.BlockSpec((1, LANES), vec),
            pl.BlockSpec((1, wb, tq), lambda bi, hi, qi: (bi, hi, qi)),
            pl.BlockSpec((1, s, wb), lambda bi, hi, qi: (bi, 0, hi)),
            pl.BlockSpec((1, s // tq, wb, tq), lambda bi, hi, qi: (bi, 0, hi, 0))],
        out_specs=pl.BlockSpec((1, tq, wb), lambda bi, hi, qi: (bi, qi, hi)),
        out_shape=jax.ShapeDtypeStruct((b, s, d), BF16),
        scratch_shapes=[pltpu.VMEM((heads, LANES, 2 * tq), BF16),
                        pltpu.VMEM((heads, 1, 2 * tq), F32),
                        pltpu.VMEM((heads, 1, 2 * tq), F32),
                        pltpu.VMEM((heads, LANES, 2 * tq), F32),
                        pltpu.VMEM((heads, tq, 2 * tq), F32),
                        pltpu.VMEM((heads, tq, 2 * tq), F32)],
        compiler_params=_cparams(("arbitrary", "arbitrary", "arbitrary")),
    )(lq1.reshape(1, -1), lk1.reshape(1, -1), lq2.reshape(1, -1), lk2.reshape(1, -1),
      subln.reshape(1, -1), qt, k, vt)


def _mla_attn_kernel(qt_ref, k_ref, vt_ref, o_ref, m_ref, l_ref, acc_ref, sa_ref, sb_ref, *, tq, heads):
    qi = pl.program_id(2)
    _init_stats(m_ref, l_ref, acc_ref)

    def scores(kb):
        off = pl.multiple_of(kb * tq, tq)
        return [jnp.dot(k_ref[0, pl.ds(off, tq), g * LANES:(g + 1) * LANES],
                        qt_ref[0, g * LANES:(g + 1) * LANES, :],
                        preferred_element_type=F32) for g in range(heads)]

    def softmax(s, kb, mask):
        for g in range(heads):
            _softmax_block(s[g], vt_ref[0, kb, g * MLA_V:(g + 1) * MLA_V, :], g, m_ref, l_ref, acc_ref, mask)

    _causal_sweep(qi, scores, softmax, sa_ref, sb_ref, _causal_mask_t(tq, tq, tq))
    for p in range(heads // 2):
        ot = jnp.concatenate([acc_ref[2 * p] * (1.0 / l_ref[2 * p]),
                              acc_ref[2 * p + 1] * (1.0 / l_ref[2 * p + 1])], axis=0)
        o_ref[0, :, p * LANES:(p + 1) * LANES] = ot.T.astype(BF16)


def _mla_attn(qt, k, vt):
    b, s, dk = k.shape
    heads = 8
    tq = vt.shape[3]
    return pl.pallas_call(
        functools.partial(_mla_attn_kernel, tq=tq, heads=heads),
        grid=(b, dk // (heads * LANES), s // tq),
        in_specs=[pl.BlockSpec((1, heads * LANES, tq), lambda bi, hi, qi: (bi, hi, qi)),
                  pl.BlockSpec((1, s, heads * LANES), lambda bi, hi, qi: (bi, 0, hi)),
                  pl.BlockSpec((1, s // tq, heads * MLA_V, tq), lambda bi, hi, qi: (bi, 0, hi, 0))],
        out_specs=pl.BlockSpec((1, tq, heads * MLA_V), lambda bi, hi, qi: (bi, qi, hi)),
        out_shape=jax.ShapeDtypeStruct((b, s, dk // LANES * MLA_V), BF16),
        scratch_shapes=[pltpu.VMEM((heads, 1, tq), F32),
                        pltpu.VMEM((heads, 1, tq), F32),
                        pltpu.VMEM((heads, MLA_V, tq), F32),
                        pltpu.VMEM((heads, tq, tq), F32),
                        pltpu.VMEM((heads, tq, tq), F32)],
        compiler_params=_cparams(("arbitrary", "arbitrary", "arbitrary")),
    )(qt, k, vt)


def _mla_pre_kernel(x_ref, mod_ref, pos_ref, g_ref, wint_ref, qag_ref, kvag_ref, kpeg_ref,
                    wuqt_ref, wukvt_ref, qng_ref, qpg_ref, kng_ref, invf_ref,
                    q_out, k_out, v_out, *, scale):
    h = _adaln_h(x_ref[0], mod_ref, g_ref, 0, 1)
    t = h.shape[0]
    reps = t // LANES
    nh, nope, rope, vd = MLA_HEADS, MLA_NOPE, MLA_ROPE, MLA_V
    lanes = lambda ref: jnp.tile(ref[...], (1, reps))

    def rms(z, axis):
        return z * lax.rsqrt(jnp.mean(z * z, axis=axis, keepdims=True) + EPS)

    lat = jnp.dot(wint_ref[...], h.T.astype(BF16), preferred_element_type=F32)
    cqn = (rms(lat[:MLA_Q_RANK], 0) * lanes(qag_ref)).astype(BF16)
    ckvn = (rms(lat[MLA_Q_RANK:MLA_Q_RANK + MLA_KV_RANK], 0) * lanes(kvag_ref)).astype(BF16)
    kpen = rms(lat[MLA_Q_RANK + MLA_KV_RANK:], 0) * lanes(kpeg_ref)
    q3 = jnp.dot(wuqt_ref[...], cqn, preferred_element_type=F32).reshape(nh, nope + rope, t)
    kv3 = jnp.dot(wukvt_ref[...], ckvn, preferred_element_type=F32).reshape(nh, nope + vd, t)

    vt = kv3[:, nope:].reshape(nh * vd, t)
    tk = v_out.shape[3]
    for j in range(v_out.shape[1]):
        v_out[0, j] = vt[:, j * tk:(j + 1) * tk].astype(BF16)

    half = rope // 2
    ang = lanes(invf_ref) * pos_ref[0]
    cos = jnp.cos(ang)[None]
    sin = jnp.sin(ang)[None]

    def rotary(z):
        z1 = z[:, :half]
        z2 = z[:, half:]
        return jnp.concatenate([z1 * cos - z2 * sin, z2 * cos + z1 * sin], axis=1)

    pad = jnp.zeros((nh, LANES - nope - rope, t), F32)
    qn = rms(q3[:, :nope], 1) * lanes(qng_ref)[None]
    qp = rotary(rms(q3[:, nope:], 1) * lanes(qpg_ref)[None])
    q_out[0] = (jnp.concatenate([qn, qp, pad], axis=1).reshape(nh * LANES, t) * scale).astype(BF16)
    kn = rms(kv3[:, :nope], 1) * lanes(kng_ref)[None]
    kp = jnp.broadcast_to(rotary(kpen[None]), (nh, rope, t))
    k_out[0] = jnp.concatenate([kn, kp, pad], axis=1).reshape(nh * LANES, t).T.astype(BF16)


def _mla_pre(x, mod, posr, norm_g, w_in, q_a_norm, kv_a_norm, w_uq, w_ukv,
             q_nope_norm, q_pe_norm, k_nope_norm, k_pe_norm):
    b, s, d = x.shape
    t = min(TOK_TILE, s)
    nh, nope, rope, vd = MLA_HEADS, MLA_NOPE, MLA_ROPE, MLA_V
    w = nh * LANES
    invf = (ROPE_THETA ** (-np.arange(0, rope, 2, dtype=np.float32) / rope)).astype(np.float32)
    lanes = lambda v: jnp.broadcast_to(jnp.asarray(v, F32)[:, None], (v.shape[0], LANES))
    row = lambda bi, ti: (0, 0)
    tok = lambda bi, ti: (bi, ti, 0)
    full = lambda a: pl.BlockSpec(a.shape, row)
    consts = [norm_g.reshape(1, d), w_in.T.astype(BF16), lanes(q_a_norm), lanes(kv_a_norm), lanes(k_pe_norm),
              w_uq.T.astype(BF16), w_ukv.T.astype(BF16), lanes(q_nope_norm), lanes(q_pe_norm),
              lanes(k_nope_norm), lanes(invf)]
    tk = min(ATT_TILE, t)
    return pl.pallas_call(
        functools.partial(_mla_pre_kernel, scale=(nope + rope) ** -0.5 * LOG2E),
        grid=(b, s // t),
        in_specs=[pl.BlockSpec((1, t, d), tok),
                  pl.BlockSpec((1, 6, d), lambda bi, ti: (bi, 0, 0)),
                  pl.BlockSpec((1, 1, t), lambda bi, ti: (bi, 0, ti))] + [full(a) for a in consts],
        out_specs=[pl.BlockSpec((1, w, t), lambda bi, ti: (bi, 0, ti)),
                   pl.BlockSpec((1, t, w), tok),
                   pl.BlockSpec((1, t // tk, nh * vd, tk), lambda bi, ti: (bi, ti, 0, 0))],
        out_shape=[jax.ShapeDtypeStruct((b, w, s), BF16),
                   jax.ShapeDtypeStruct((b, s, w), BF16),
                   jax.ShapeDtypeStruct((b, s // tk, nh * vd, tk), BF16)],
        compiler_params=_cparams(("arbitrary", "arbitrary")),
    )(x, mod, posr, *consts)


def _post_kernel(o_ref, wo_ref, x_ref, mod_ref, g_ref, wr_ref, bsel_ref,
                 x_out, h_out, route_out, cnt_out, run_ref, *, d):
    first = (pl.program_id(0) == 0) & (pl.program_id(1) == 0)

    @pl.when(first)
    def _():
        run_ref[...] = jnp.zeros_like(run_ref)

    y = jnp.dot(o_ref[0], wo_ref[...], preferred_element_type=F32)
    x = x_ref[0] + mod_ref[0, 2:3, :] * y
    x_out[0] = x
    h = _adaln_h(x, mod_ref, g_ref, 3, 4)
    t = h.shape[0]

    hh = h.astype(BF16)
    hl = (h - hh.astype(F32)).astype(BF16)
    logits = (jnp.dot(hh, wr_ref[0], preferred_element_type=F32)
              + jnp.dot(hl, wr_ref[0], preferred_element_type=F32)
              + jnp.dot(hh, wr_ref[1], preferred_element_type=F32))

    ng, ne = N_GROUPS, EXPERTS_PER_GROUP
    lt = logits.T
    sub = lax.broadcasted_iota(jnp.int32, (SUBLANES, t), 0).astype(F32)

    def first_argmax(val):
        mx = jnp.max(val, axis=0, keepdims=True)
        return jnp.min(jnp.where(val == mx, sub, float(SUBLANES)), axis=0, keepdims=True)

    def pick(val, idx):
        return jnp.sum(jnp.where(sub == idx, val, 0.0), axis=0, keepdims=True)

    gmask = sub < ng
    gl = jnp.where(gmask, lt[:SUBLANES], NEG_INF)
    ge = jnp.exp(gl - jnp.max(gl, axis=0, keepdims=True))
    gprob = ge / jnp.sum(ge, axis=0, keepdims=True)
    gidx = first_argmax(jnp.where(gmask, gprob + bsel_ref[0], NEG_INF))
    ggate = pick(gprob, gidx)

    el = jnp.zeros((SUBLANES, t), F32)
    eb = jnp.zeros((SUBLANES, t), F32)
    for g in range(ng):
        chosen = gidx == float(g)
        el = jnp.where(chosen, lt[SUBLANES * (g + 1):SUBLANES * (g + 2)], el)
        eb = jnp.where(chosen, bsel_ref[g + 1], eb)
    ee = jnp.exp(el - jnp.max(el, axis=0, keepdims=True))
    eprob = ee / jnp.sum(ee, axis=0, keepdims=True)
    sel = eprob + eb
    e1 = first_argmax(sel)
    e2 = first_argmax(jnp.where(sub == e1, NEG_INF, sel))
    p1 = pick(eprob, e1)
    p2 = pick(eprob, e2)
    psum = p1 + p2
    w1 = p1 / psum * ggate
    w2 = p2 / psum * ggate
    swap = e2 < e1
    lo = jnp.where(swap, e2, e1)
    hi = jnp.where(swap, e1, e2)
    wa = jnp.where(swap, w2, w1)
    wb = jnp.where(swap, w1, w2)
    combo = gidx * float(N_PAIRS) + lo * (2.0 * ne - 1.0 - lo) * 0.5 + (hi - lo - 1.0)

    row = lax.broadcasted_iota(jnp.int32, (LANES, t), 0).astype(F32)
    wrow = jnp.where(row == 0.0, wa, jnp.where(row == 1.0, wb, 0.0)).T
    _pack_token_tiles(h_out, h, wrow)

    onehot = row == combo
    oh = jnp.where(onehot, 1.0, 0.0)
    r_i = lax.broadcasted_iota(jnp.int32, (t, t), 0)
    c_i = lax.broadcasted_iota(jnp.int32, (t, t), 1)
    earlier = jnp.where(r_i < c_i, 1.0, 0.0).astype(BF16)
    before = jnp.dot(oh.astype(BF16), earlier, preferred_element_type=F32) + run_ref[...]
    rank = jnp.sum(jnp.where(onehot, before, 0.0), axis=0, keepdims=True)
    run = run_ref[...] + jnp.sum(oh, axis=1, keepdims=True)
    run_ref[...] = run
    cnt_out[...] = run
    route_out[0] = jnp.where(sub == 0.0, combo, jnp.where(sub == 1.0, rank, 0.0))


def _post(o, w_o, x, mod, norm_g, w_group, b_group, w_router, b_router):
    b, s, d = x.shape
    t = min(TOK_TILE, s)
    do = o.shape[2]
    ng, ne = N_GROUPS, EXPERTS_PER_GROUP
    wr = jnp.concatenate([jnp.pad(w_group, ((0, 0), (0, SUBLANES - ng))),
                          jnp.pad(w_router, ((0, 0), (0, LANES - SUBLANES - ng * ne)))], axis=1)
    wr_hi = wr.astype(BF16)
    wr_lo = (wr - wr_hi.astype(F32)).astype(BF16)
    wr2 = jnp.stack([wr_hi, wr_lo])
    bsel = jnp.concatenate([jnp.pad(b_group, (0, SUBLANES - ng))[None, :], b_router], axis=0)[:, :, None]
    row = lambda bi, ti: (0, 0)
    tok = lambda bi, ti: (bi, ti, 0)
    return pl.pallas_call(
        functools.partial(_post_kernel, d=d),
        grid=(b, s // t),
        in_specs=[pl.BlockSpec((1, t, do), tok),
                  pl.BlockSpec((do, d), row),
                  pl.BlockSpec((1, t, d), tok),
                  pl.BlockSpec((1, 6, d), lambda bi, ti: (bi, 0, 0)),
                  pl.BlockSpec((1, d), row),
                  pl.BlockSpec((2, d, LANES), lambda bi, ti: (0, 0, 0)),
                  pl.BlockSpec((ng + 1, ne, 1), lambda bi, ti: (0, 0, 0))],
        out_specs=[pl.BlockSpec((1, t, d), tok),
                   pl.BlockSpec((1, t * SUBLANES, LANES), tok),
                   pl.BlockSpec((1, SUBLANES, t), lambda bi, ti: (bi * (s // t) + ti, 0, 0)),
                   pl.BlockSpec((LANES, 1), row)],
        out_shape=[jax.ShapeDtypeStruct((b, s, d), F32),
                   jax.ShapeDtypeStruct((b, s * SUBLANES, LANES), jnp.uint32),
                   jax.ShapeDtypeStruct((b * s // t, SUBLANES, t), F32),
                   jax.ShapeDtypeStruct((LANES, 1), F32)],
        scratch_shapes=[pltpu.VMEM((LANES, 1), F32)],
        compiler_params=_cparams(("arbitrary", "arbitrary")),
    )(o, w_o.astype(BF16), x, mod, norm_g.reshape(1, d), wr2, bsel)


def _tok_rows(s, n):
    return pl.ds(s, n, stride=SUBLANES)


def _tok_tile(tok):
    return pl.ds(pl.multiple_of(tok * SUBLANES, SUBLANES), SUBLANES)


def _pack_token_tiles(p_out, h, wrow):
    t, d = h.shape
    half = d // 2
    bits = pltpu.bitcast(h.astype(BF16).astype(F32), jnp.uint32)
    word = bits[:, :half] | (bits[:, half:] >> 16)
    nw = half // LANES
    for s in range(nw):
        p_out[0, _tok_rows(s, t), :] = word[:, s * LANES:(s + 1) * LANES]
    p_out[0, _tok_rows(nw, t), :] = pltpu.bitcast(wrow, jnp.uint32)
    for s in range(nw + 1, SUBLANES):
        p_out[0, _tok_rows(s, t), :] = jnp.zeros((t, LANES), jnp.uint32)


def _unpack_token_tiles(hs_ref, d):
    nw = d // 2 // LANES
    tm = hs_ref.shape[0] // SUBLANES
    words = [hs_ref[_tok_rows(s, tm), :] for s in range(nw)]
    hi = [pltpu.bitcast(w & jnp.uint32(0xFFFF0000), F32).astype(BF16) for w in words]
    lo = [pltpu.bitcast(w << 16, F32).astype(BF16) for w in words]
    wrow = pltpu.bitcast(hs_ref[_tok_rows(nw, tm), :], F32)
    return jnp.concatenate(hi + lo, axis=1), wrow[:, 0:1], wrow[:, 1:2]


def _row_copy(src_ref, src_row, dst_ref, dst_row, sem):
    return pltpu.make_async_copy(src_ref.at[_tok_tile(src_row), :], dst_ref.at[_tok_tile(dst_row), :], sem)


def _start_rows(rows, make_copy):
    def start(i, c):
        for u in range(ROW_UNROLL):
            make_copy(i * ROW_UNROLL + u).start(priority=u % 2)
        return c

    lax.fori_loop(0, rows // ROW_UNROLL, start, 0)


def _wait_rows(rows, make_copy):
    def wait(i, c):
        for u in range(ROW_UNROLL):
            make_copy(0).wait()
        return c

    lax.fori_loop(0, rows // ROW_UNROLL, wait, 0)


def _dispatch_kernel(dest_ref, h_ref, init_ref, hs_ref, sem, *, rows):
    del init_ref
    copy = lambda r: _row_copy(h_ref, r, hs_ref, dest_ref[r], sem)
    _start_rows(rows, copy)
    _wait_rows(rows, copy)


def _dispatch(dest, h2, n_rows):
    n = h2.shape[0] // SUBLANES
    rows = min(ROW_TILE, n)
    return pl.pallas_call(
        functools.partial(_dispatch_kernel, rows=rows),
        grid=(n // rows,),
        in_specs=[pl.BlockSpec((rows,), lambda i: (i,), memory_space=pltpu.SMEM),
                  pl.BlockSpec((rows * SUBLANES, LANES), lambda i: (i, 0)),
                  pl.BlockSpec(memory_space=pl.ANY)],
        out_specs=pl.BlockSpec(memory_space=pl.ANY),
        out_shape=jax.ShapeDtypeStruct((n_rows * SUBLANES, LANES), jnp.uint32),
        scratch_shapes=[pltpu.SemaphoreType.DMA(())],
        input_output_aliases={2: 0},
        compiler_params=_cparams(("arbitrary",)),
    )(dest, h2, jnp.zeros((n_rows * SUBLANES, LANES), jnp.uint32))


def _moe_kernel(elo_ref, ehi_ref, blk_ref, nact_ref, hs_ref, w1a_ref, w3a_ref, w2a_ref,
                w1b_ref, w3b_ref, w2b_ref, y_ref, *, d):
    del elo_ref, ehi_ref, blk_ref
    active = pl.program_id(0) < nact_ref[0]

    @pl.when(jnp.logical_not(active))
    def _():
        y_ref[...] = jnp.zeros_like(y_ref)

    @pl.when(active)
    def _():
        x, wa, wb = _unpack_token_tiles(hs_ref, d)

        def expert(w1_ref, w3_ref, wgt):
            a = jnp.dot(x, w1_ref[0], preferred_element_type=F32)
            g = jnp.dot(x, w3_ref[0], preferred_element_type=F32)
            return (a * jax.nn.sigmoid(a) * g * wgt).astype(BF16)

        y = (jnp.dot(expert(w1a_ref, w3a_ref, wa), w2a_ref[0], preferred_element_type=F32)
             + jnp.dot(expert(w1b_ref, w3b_ref, wb), w2b_ref[0], preferred_element_type=F32))
        for s in range(d // LANES):
            y_ref[_tok_rows(s, y.shape[0]), :] = y[:, s * LANES:(s + 1) * LANES]


def _moe(hs, e_lo, e_hi, blk, nact, w1, w3, w2):
    n_rows = hs.shape[0] // SUBLANES
    d = w1.shape[1]
    ff = w1.shape[2]
    tm = MOE_TILE
    n_tiles = n_rows // tm
    assert d == SUBLANES * LANES
    wspec = lambda shape, which: pl.BlockSpec(shape, (lambda j, lo, hi, bk, na: (lo[j], 0, 0)) if which == 0
                                              else (lambda j, lo, hi, bk, na: (hi[j], 0, 0)))
    grid_spec = pltpu.PrefetchScalarGridSpec(
        num_scalar_prefetch=4,
        grid=(n_tiles,),
        in_specs=[pl.BlockSpec((tm * SUBLANES, LANES), lambda j, lo, hi, bk, na: (bk[j], 0)),
                  wspec((1, d, ff), 0), wspec((1, d, ff), 0), wspec((1, ff, d), 0),
                  wspec((1, d, ff), 1), wspec((1, d, ff), 1), wspec((1, ff, d), 1)],
        out_specs=pl.BlockSpec((tm * SUBLANES, LANES), lambda j, lo, hi, bk, na: (j, 0)),
    )
    return pl.pallas_call(
        functools.partial(_moe_kernel, d=d),
        grid_spec=grid_spec,
        out_shape=jax.ShapeDtypeStruct((n_rows * SUBLANES, LANES), F32),
        compiler_params=_cparams(("arbitrary",)),
    )(e_lo, e_hi, blk, nact, hs, w1, w3, w2, w1, w3, w2)


def _combine_kernel(dest_ref, next_ref, x_ref, gate_ref, y_ref, o_ref, buf_ref, sem, *, rows):
    i = pl.program_id(0)
    slot = i & 1

    def gather(d_ref, sl):
        return lambda r: _row_copy(y_ref, d_ref[r], buf_ref.at[sl], r, sem.at[sl])

    @pl.when(i == 0)
    def _():
        _start_rows(rows, gather(dest_ref, 0))

    @pl.when(i + 1 < pl.num_programs(0))
    def _():
        _start_rows(rows, gather(next_ref, 1 - slot))

    _wait_rows(rows, gather(dest_ref, slot))
    for s in range(SUBLANES):
        cols = slice(s * LANES, (s + 1) * LANES)
        o_ref[:, cols] = x_ref[:, cols] + gate_ref[0, :, cols] * buf_ref[slot, _tok_rows(s, rows), :]


def _combine(dest, x2, gate, y, seq):
    n, d = x2.shape
    rows = min(ROW_TILE, seq)
    per_seq = seq // rows
    steps = n // rows
    return pl.pallas_call(
        functools.partial(_combine_kernel, rows=rows),
        grid=(steps,),
        in_specs=[pl.BlockSpec((rows,), lambda i: (i,), memory_space=pltpu.SMEM),
                  pl.BlockSpec((rows,), lambda i: (jnp.minimum(i + 1, steps - 1),), memory_space=pltpu.SMEM),
                  pl.BlockSpec((rows, d), lambda i: (i, 0)),
                  pl.BlockSpec((1, 1, d), lambda i: (i // per_seq, 0, 0)),
                  pl.BlockSpec(memory_space=pl.ANY)],
        out_specs=pl.BlockSpec((rows, d), lambda i: (i, 0)),
        out_shape=jax.ShapeDtypeStruct((n, d), F32),
        scratch_shapes=[pltpu.VMEM((2, rows * SUBLANES, LANES), F32), pltpu.SemaphoreType.DMA((2,))],
        compiler_params=_cparams(("arbitrary",)),
    )(dest, dest, x2, gate, y)


def _moe_layer(x_new, h2, route, counts, gate_f, w1, w3, w2):
    b, s, d = x_new.shape
    n = b * s
    tm = MOE_TILE
    n_tiles = n // tm + N_COMBOS
    combo = route[:, 0, :].reshape(n).astype(jnp.int32)
    rank = route[:, 1, :].reshape(n).astype(jnp.int32)
    cnt = counts[:N_COMBOS, 0].astype(jnp.int32)
    tiles_per = (cnt + tm - 1) // tm
    tile_end = jnp.cumsum(tiles_per)
    row_off = (tile_end - tiles_per) * tm
    ids = jnp.arange(N_COMBOS, dtype=jnp.int32)

    def lookup(table, idx):
        return jnp.sum(jnp.where(idx[:, None] == ids[None, :], table[None, :], 0), axis=1)

    dest = lookup(row_off, combo) + rank
    nact = tile_end[-1:]
    blk = jnp.minimum(jnp.arange(n_tiles, dtype=jnp.int32), nact[0] - 1)
    tile_combo = jnp.sum((tile_end[None, :] <= blk[:, None]).astype(jnp.int32), axis=1)
    e_lo = lookup(jnp.asarray(_COMBO_LO), tile_combo)
    e_hi = lookup(jnp.asarray(_COMBO_HI), tile_combo)

    hs = _dispatch(dest, h2.reshape(n * SUBLANES, LANES), n_tiles * tm)
    ne = w1.shape[0] * w1.shape[1]
    y = _moe(hs, e_lo, e_hi, blk, nact.astype(jnp.int32),
             w1.reshape(ne, d, -1).astype(BF16), w3.reshape(ne, d, -1).astype(BF16),
             w2.reshape(ne, -1, d).astype(BF16))
    return _combine(dest, x_new.reshape(n, d), gate_f, y, s).reshape(b, s, d)


def kernel(x, c, positions, ada_w, ada_b, norm_mix, norm_ffn, da_w_qkv, da_q_norm, da_k_norm, da_lambda_q1, da_lambda_k1, da_lambda_q2, da_lambda_k2, da_subln, da_w_o, mla_w_in, mla_q_a_norm, mla_kv_a_norm, mla_w_uq, mla_w_ukv, mla_q_nope_norm, mla_q_pe_norm, mla_k_nope_norm, mla_k_pe_norm, mla_w_o, moe_w_group, moe_b_group, moe_w_router, moe_b_router, moe_w1, moe_w3, moe_w2):
    b, s, d = x.shape
    depth = ada_w.shape[0]
    mod_all = _modulation(c, ada_w, ada_b).reshape(depth, b, 6, d)
    posr = positions.astype(F32).reshape(b, 1, s)
    for i in range(depth):
        mod = mod_all[i]
        j = i // 2
        if i % 2 == 0:
            lam_init = 0.8 - 0.6 * math.exp(-0.3 * i)
            q, k, v = _da_pre(x, mod, posr, norm_mix[i], da_w_qkv[j], da_q_norm[j], da_k_norm[j])
            o = _da_attn(q, k, v, da_lambda_q1[j], da_lambda_k1[j], da_lambda_q2[j], da_lambda_k2[j],
                         da_subln[j], lam_init)
            w_o = da_w_o[j]
        else:
            q, k, v = _mla_pre(x, mod, posr, norm_mix[i], mla_w_in[j], mla_q_a_norm[j], mla_kv_a_norm[j],
                               mla_w_uq[j], mla_w_ukv[j], mla_q_nope_norm[j], mla_q_pe_norm[j],
                               mla_k_nope_norm[j], mla_k_pe_norm[j])
            o = _mla_attn(q, k, v)
            w_o = mla_w_o[j]
        x_new, h2, route, counts = _post(o, w_o, x, mod, norm_ffn[i], moe_w_group[i], moe_b_group[i],
                                         moe_w_router[i], moe_b_router[i])
        x = _moe_layer(x_new, h2, route, counts, mod[:, 5:6, :], moe_w1[i], moe_w3[i], moe_w2[i])
    return x
```

```python
import functools
import math

import numpy as np
import jax
import jax.numpy as jnp
from jax import lax
from jax.experimental import pallas as pl
from jax.experimental.pallas import tpu as pltpu

F32 = jnp.float32
BF16 = jnp.bfloat16

ROPE_THETA = 10000.0
EPS = 1e-6
NEG_INF = -1e30
LOG2E = math.log2(math.e)
DA_HEAD_DIM = 64
MLA_HEADS = 16
MLA_NOPE = 64
MLA_ROPE = 32
MLA_V = 64
MLA_Q_RANK = 384
MLA_KV_RANK = 256
N_GROUPS = 4
EXPERTS_PER_GROUP = 8
EXPERT_FF = 256

LANES = 128
VMEM_LIMIT = 56 * 1024 * 1024
TOK_TILE = 512
ATT_TILE = 256
MOE_TILE = 256
ROW_TILE = 1024
ROW_UNROLL = 8
SUBLANES = 8

N_PAIRS = EXPERTS_PER_GROUP * (EXPERTS_PER_GROUP - 1) // 2
N_COMBOS = N_GROUPS * N_PAIRS


def _combo_tables():
    lo_t, hi_t = [], []
    for g in range(N_GROUPS):
        for lo in range(EXPERTS_PER_GROUP):
            for hi in range(lo + 1, EXPERTS_PER_GROUP):
                lo_t.append(g * EXPERTS_PER_GROUP + lo)
                hi_t.append(g * EXPERTS_PER_GROUP + hi)
    return np.asarray(lo_t, np.int32), np.asarray(hi_t, np.int32)


_COMBO_LO, _COMBO_HI = _combo_tables()


def _cparams(sem):
    return pltpu.CompilerParams(dimension_semantics=sem, vmem_limit_bytes=VMEM_LIMIT)


def _adaln_h(x, mod_ref, g_ref, shift_row, scale_row):
    ms = jnp.mean(x * x, axis=-1, keepdims=True)
    h = x * lax.rsqrt(ms + EPS) * g_ref[...]
    return h * (1.0 + mod_ref[0, scale_row:scale_row + 1, :]) + mod_ref[0, shift_row:shift_row + 1, :]


def _mod_kernel(c_ref, w_ref, b_ref, o_ref):
    c = c_ref[...]
    cond = (c * jax.nn.sigmoid(c)).astype(BF16)
    o_ref[0] = jnp.dot(cond, w_ref[0].astype(BF16), preferred_element_type=F32) + b_ref[0]


def _modulation(c, ada_w, ada_b):
    depth, d, n6 = ada_w.shape
    b = c.shape[0]
    tn = 1536
    return pl.pallas_call(
        _mod_kernel,
        grid=(depth, n6 // tn),
        in_specs=[pl.BlockSpec((b, d), lambda i, j: (0, 0)),
                  pl.BlockSpec((1, d, tn), lambda i, j: (i, 0, j)),
                  pl.BlockSpec((1, 1, tn), lambda i, j: (i, 0, j))],
        out_specs=pl.BlockSpec((1, b, tn), lambda i, j: (i, 0, j)),
        out_shape=jax.ShapeDtypeStruct((depth, b, n6), F32),
        compiler_params=_cparams(("arbitrary", "arbitrary")),
    )(c, ada_w, ada_b.reshape(depth, 1, n6))


def _da_pre_kernel(x_ref, mod_ref, pos_ref, g_ref, wt_ref, qg_ref, kg_ref, invf_ref,
                   q_out, k_out, v_out, *, d, scale):
    h = _adaln_h(x_ref[0], mod_ref, g_ref, 0, 1)
    t = h.shape[0]
    qkv = jnp.dot(wt_ref[...], h.T.astype(BF16), preferred_element_type=F32)
    vt = qkv[2 * d:]
    tk = v_out.shape[3]
    for j in range(v_out.shape[1]):
        v_out[0, j] = vt[:, j * tk:(j + 1) * tk].astype(BF16)

    dh = DA_HEAD_DIM
    half = dh // 2
    reps = t // LANES
    ang = jnp.tile(invf_ref[...], (1, reps)) * pos_ref[0]
    cos = jnp.cos(ang)[None, None]
    sin = jnp.sin(ang)[None, None]

    def norm_rope(z, gain_ref):
        z3 = z.reshape(d // dh, dh, t)
        r = lax.rsqrt(jnp.mean(z3 * z3, axis=1, keepdims=True) + EPS)
        z4 = (z3 * r).reshape(d // dh, 2, half, t) * jnp.tile(gain_ref[...], (1, reps)).reshape(1, 2, half, t)
        z1 = z4[:, 0:1]
        z2 = z4[:, 1:2]
        return jnp.concatenate([z1 * cos - z2 * sin, z2 * cos + z1 * sin], axis=1).reshape(d, t)

    q_out[0] = (norm_rope(qkv[:d], qg_ref) * scale).astype(BF16)
    k_out[0] = norm_rope(qkv[d:2 * d], kg_ref).T.astype(BF16)


def _da_pre(x, mod, posr, norm_g, w_qkv, q_norm, k_norm):
    b, s, d = x.shape
    t = min(TOK_TILE, s)
    dh = DA_HEAD_DIM
    invf = (ROPE_THETA ** (-np.arange(0, dh, 2, dtype=np.float32) / dh)).astype(np.float32)
    lanes = lambda v: jnp.broadcast_to(jnp.asarray(v, F32)[:, None], (v.shape[0], LANES))
    row = lambda bi, ti: (0, 0)
    tok = lambda bi, ti: (bi, ti, 0)
    tk = min(ATT_TILE, t)
    return pl.pallas_call(
        functools.partial(_da_pre_kernel, d=d, scale=dh ** -0.5 * LOG2E),
        grid=(b, s // t),
        in_specs=[pl.BlockSpec((1, t, d), tok),
                  pl.BlockSpec((1, 6, d), lambda bi, ti: (bi, 0, 0)),
                  pl.BlockSpec((1, 1, t), lambda bi, ti: (bi, 0, ti)),
                  pl.BlockSpec((1, d), row),
                  pl.BlockSpec((3 * d, d), row),
                  pl.BlockSpec((dh, LANES), row),
                  pl.BlockSpec((dh, LANES), row),
                  pl.BlockSpec((dh // 2, LANES), row)],
        out_specs=[pl.BlockSpec((1, d, t), lambda bi, ti: (bi, 0, ti)),
                   pl.BlockSpec((1, t, d), tok),
                   pl.BlockSpec((1, t // tk, d, tk), lambda bi, ti: (bi, ti, 0, 0))],
        out_shape=[jax.ShapeDtypeStruct((b, d, s), BF16),
                   jax.ShapeDtypeStruct((b, s, d), BF16),
                   jax.ShapeDtypeStruct((b, s // tk, d, tk), BF16)],
        compiler_params=_cparams(("arbitrary", "arbitrary")),
    )(x, mod, posr, norm_g.reshape(1, d), w_qkv.T.astype(BF16), lanes(q_norm), lanes(k_norm), lanes(invf))


def _softmax_block(buf, vt_blk, g, m_ref, l_ref, acc_ref, mask):
    s = buf[0][g]
    if mask is None:
        bmax = buf[1][g]
    else:
        s = jnp.where(mask, s, NEG_INF)
        bmax = jnp.max(s, axis=0, keepdims=True)
    m_prev = m_ref[g]
    m_new = jnp.maximum(m_prev, bmax)
    alpha = jnp.exp2(m_prev - m_new)
    p = jnp.exp2(s - m_new)
    l_ref[g] = alpha * l_ref[g] + jnp.sum(p, axis=0, keepdims=True)
    acc_ref[g] = alpha * acc_ref[g] + jnp.dot(vt_blk, p.astype(BF16), preferred_element_type=F32)
    m_ref[g] = m_new


def _causal_sweep(qi, scores, softmax, sa_ref, sb_ref, diag_mask):
    def fill(buf, kb):
        for g, s in enumerate(scores(kb)):
            buf[0][g] = s
            buf[1][g] = jnp.max(s, axis=0, keepdims=True)

    fill(sa_ref, 0)

    def body(j, c):
        kb = 2 * j
        fill(sb_ref, kb + 1)
        softmax(sa_ref, kb, None)
        fill(sa_ref, kb + 2)
        softmax(sb_ref, kb + 1, None)
        return c

    lax.fori_loop(0, lax.shift_right_logical(qi, 1), body, 0)
    odd = (qi & 1) == 1

    @pl.when(jnp.logical_not(odd))
    def _():
        softmax(sa_ref, qi, diag_mask)

    @pl.when(odd)
    def _():
        fill(sb_ref, qi)
        softmax(sa_ref, qi - 1, None)
        softmax(sb_ref, qi, diag_mask)


def _init_stats(m_ref, l_ref, acc_ref):
    m_ref[...] = jnp.full(m_ref.shape, NEG_INF, F32)
    l_ref[...] = jnp.zeros(l_ref.shape, F32)
    acc_ref[...] = jnp.zeros(acc_ref.shape, F32)


def _causal_mask_t(tk, cols, tq):
    c = lax.broadcasted_iota(jnp.int32, (tk, cols), 0)
    r = lax.broadcasted_iota(jnp.int32, (tk, cols), 1)
    r = jnp.where(r >= tq, r - tq, r)
    return c <= r


def _da_attn_kernel(lq1_ref, lk1_ref, lq2_ref, lk2_ref, sub_ref, qt_ref, k_ref, vt_ref, o_ref,
                    qs_ref, m_ref, l_ref, acc_ref, sa_ref, sb_ref, ma_ref, mb_ref, *, tq, heads, lam_init):
    qi = pl.program_id(2)
    sub = lax.broadcasted_iota(jnp.int32, (LANES, tq), 0)
    for g in range(heads):
        qt = qt_ref[0, g * LANES:(g + 1) * LANES, :]
        zero = jnp.zeros_like(qt)
        qs_ref[g] = jnp.concatenate([jnp.where(sub < DA_HEAD_DIM, qt, zero),
                                     jnp.where(sub >= DA_HEAD_DIM, qt, zero)], axis=1)
    _init_stats(m_ref, l_ref, acc_ref)

    def scores(kb):
        off = pl.multiple_of(kb * tq, tq)
        return [jnp.dot(k_ref[0, pl.ds(off, tq), g * LANES:(g + 1) * LANES], qs_ref[g],
                        preferred_element_type=F32) for g in range(heads)]

    def softmax(s, kb, mask):
        for g in range(heads):
            _softmax_block(s, vt_ref[0, kb, g * LANES:(g + 1) * LANES, :], g, m_ref, l_ref, acc_ref, mask)

    _causal_sweep(qi, scores, softmax, (sa_ref, ma_ref), (sb_ref, mb_ref), _causal_mask_t(tq, 2 * tq, tq))

    lam = (jnp.exp(jnp.sum(lq1_ref[...] * lk1_ref[...], axis=1, keepdims=True))
           - jnp.exp(jnp.sum(lq2_ref[...] * lk2_ref[...], axis=1, keepdims=True)) + lam_init)
    for g in range(heads):
        ot = acc_ref[g] * (1.0 / l_ref[g])
        dd = (ot[:, :tq] - lam * ot[:, tq:]).T
        ms = jnp.mean(dd * dd, axis=-1, keepdims=True)
        o_ref[0, :, g * LANES:(g + 1) * LANES] = (
            dd * lax.rsqrt(ms + EPS) * sub_ref[...] * (1.0 - lam_init)).astype(BF16)


def _da_attn(qt, k, vt, lq1, lk1, lq2, lk2, subln, lam_init):
    b, s, d = k.shape
    heads = 4
    wb = heads * LANES
    tq = vt.shape[3]
    vec = lambda bi, hi, qi: (0, 0)
    return pl.pallas_call(
        functools.partial(_da_attn_kernel, tq=tq, heads=heads, lam_init=lam_init),
        grid=(b, d // wb, s // tq),
        in_specs=[pl.BlockSpec((1, DA_HEAD_DIM), vec)] * 4 + [
            pl.BlockSpec((1, LANES), vec),
            pl.BlockSpec((1, wb, tq), lambda bi, hi, qi: (bi, hi, qi)),
            pl.BlockSpec((1, s, wb), lambda bi, hi, qi: (bi, 0, hi)),
            pl.BlockSpec((1, s // tq, wb, tq), lambda bi, hi, qi: (bi, 0, hi, 0))],
        out_specs=pl.BlockSpec((1, tq, wb), lambda bi, hi, qi: (bi, qi, hi)),
        out_shape=jax.ShapeDtypeStruct((b, s, d), BF16),
        scratch_shapes=[pltpu.VMEM((heads, LANES, 2 * tq), BF16),
                        pltpu.VMEM((heads, 1, 2 * tq), F32),
                        pltpu.VMEM((heads, 1, 2 * tq), F32),
                        pltpu.VMEM((heads, LANES, 2 * tq), F32),
                        pltpu.VMEM((heads, tq, 2 * tq), F32),
                        pltpu.VMEM((heads, tq, 2 * tq), F32),
                        pltpu.VMEM((heads, 1, 2 * tq), F32),
                        pltpu.VMEM((heads, 1, 2 * tq), F32)],
        compiler_params=_cparams(("arbitrary", "arbitrary", "arbitrary")),
    )(lq1.reshape(1, -1), lk1.reshape(1, -1), lq2.reshape(1, -1), lk2.reshape(1, -1),
      subln.reshape(1, -1), qt, k, vt)


def _mla_attn_kernel(qt_ref, k_ref, vt_ref, o_ref, m_ref, l_ref, acc_ref, sa_ref, sb_ref, ma_ref, mb_ref,
                     *, tq, heads):
    qi = pl.program_id(2)
    _init_stats(m_ref, l_ref, acc_ref)

    def scores(kb):
        off = pl.multiple_of(kb * tq, tq)
        return [jnp.dot(k_ref[0, pl.ds(off, tq), g * LANES:(g + 1) * LANES],
                        qt_ref[0, g * LANES:(g + 1) * LANES, :],
                        preferred_element_type=F32) for g in range(heads)]

    def softmax(s, kb, mask):
        for g in range(heads):
            _softmax_block(s, vt_ref[0, kb, g * MLA_V:(g + 1) * MLA_V, :], g, m_ref, l_ref, acc_ref, mask)

    _causal_sweep(qi, scores, softmax, (sa_ref, ma_ref), (sb_ref, mb_ref), _causal_mask_t(tq, tq, tq))
    for p in range(heads // 2):
        ot = jnp.concatenate([acc_ref[2 * p] * (1.0 / l_ref[2 * p]),
                              acc_ref[2 * p + 1] * (1.0 / l_ref[2 * p + 1])], axis=0)
        o_ref[0, :, p * LANES:(p + 1) * LANES] = ot.T.astype(BF16)


def _mla_attn(qt, k, vt):
    b, s, dk = k.shape
    heads = 8
    tq = vt.shape[3]
    return pl.pallas_call(
        functools.partial(_mla_attn_kernel, tq=tq, heads=heads),
        grid=(b, dk // (heads * LANES), s // tq),
        in_specs=[pl.BlockSpec((1, heads * LANES, tq), lambda bi, hi, qi: (bi, hi, qi)),
                  pl.BlockSpec((1, s, heads * LANES), lambda bi, hi, qi: (bi, 0, hi)),
                  pl.BlockSpec((1, s // tq, heads * MLA_V, tq), lambda bi, hi, qi: (bi, 0, hi, 0))],
        out_specs=pl.BlockSpec((1, tq, heads * MLA_V), lambda bi, hi, qi: (bi, qi, hi)),
        out_shape=jax.ShapeDtypeStruct((b, s, dk // LANES * MLA_V), BF16),
        scratch_shapes=[pltpu.VMEM((heads, 1, tq), F32),
                        pltpu.VMEM((heads, 1, tq), F32),
                        pltpu.VMEM((heads, MLA_V, tq), F32),
                        pltpu.VMEM((heads, tq, tq), F32),
                        pltpu.VMEM((heads, tq, tq), F32),
                        pltpu.VMEM((heads, 1, tq), F32),
                        pltpu.VMEM((heads, 1, tq), F32)],
        compiler_params=_cparams(("arbitrary", "arbitrary", "arbitrary")),
    )(qt, k, vt)


def _mla_pre_kernel(x_ref, mod_ref, pos_ref, g_ref, wint_ref, qag_ref, kvag_ref, kpeg_ref,
                    wuqt_ref, wukvt_ref, qng_ref, qpg_ref, kng_ref, invf_ref,
                    q_out, k_out, v_out, *, scale):
    h = _adaln_h(x_ref[0], mod_ref, g_ref, 0, 1)
    t = h.shape[0]
    reps = t // LANES
    nh, nope, rope, vd = MLA_HEADS, MLA_NOPE, MLA_ROPE, MLA_V
    lanes = lambda ref: jnp.tile(ref[...], (1, reps))

    def rms(z, axis):
        return z * lax.rsqrt(jnp.mean(z * z, axis=axis, keepdims=True) + EPS)

    lat = jnp.dot(wint_ref[...], h.T.astype(BF16), preferred_element_type=F32)
    cqn = (rms(lat[:MLA_Q_RANK], 0) * lanes(qag_ref)).astype(BF16)
    ckvn = (rms(lat[MLA_Q_RANK:MLA_Q_RANK + MLA_KV_RANK], 0) * lanes(kvag_ref)).astype(BF16)
    kpen = rms(lat[MLA_Q_RANK + MLA_KV_RANK:], 0) * lanes(kpeg_ref)
    q3 = jnp.dot(wuqt_ref[...], cqn, preferred_element_type=F32).reshape(nh, nope + rope, t)
    kv3 = jnp.dot(wukvt_ref[...], ckvn, preferred_element_type=F32).reshape(nh, nope + vd, t)

    vt = kv3[:, nope:].reshape(nh * vd, t)
    tk = v_out.shape[3]
    for j in range(v_out.shape[1]):
        v_out[0, j] = vt[:, j * tk:(j + 1) * tk].astype(BF16)

    half = rope // 2
    ang = lanes(invf_ref) * pos_ref[0]
    cos = jnp.cos(ang)[None]
    sin = jnp.sin(ang)[None]

    def rotary(z):
        z1 = z[:, :half]
        z2 = z[:, half:]
        return jnp.concatenate([z1 * cos - z2 * sin, z2 * cos + z1 * sin], axis=1)

    pad = jnp.zeros((nh, LANES - nope - rope, t), F32)
    qn = rms(q3[:, :nope], 1) * lanes(qng_ref)[None]
    qp = rotary(rms(q3[:, nope:], 1) * lanes(qpg_ref)[None])
    q_out[0] = (jnp.concatenate([qn, qp, pad], axis=1).reshape(nh * LANES, t) * scale).astype(BF16)
    kn = rms(kv3[:, :nope], 1) * lanes(kng_ref)[None]
    kp = jnp.broadcast_to(rotary(kpen[None]), (nh, rope, t))
    k_out[0] = jnp.concatenate([kn, kp, pad], axis=1).reshape(nh * LANES, t).T.astype(BF16)


def _mla_pre(x, mod, posr, norm_g, w_in, q_a_norm, kv_a_norm, w_uq, w_ukv,
             q_nope_norm, q_pe_norm, k_nope_norm, k_pe_norm):
    b, s, d = x.shape
    t = min(TOK_TILE, s)
    nh, nope, rope, vd = MLA_HEADS, MLA_NOPE, MLA_ROPE, MLA_V
    w = nh * LANES
    invf = (ROPE_THETA ** (-np.arange(0, rope, 2, dtype=np.float32) / rope)).astype(np.float32)
    lanes = lambda v: jnp.broadcast_to(jnp.asarray(v, F32)[:, None], (v.shape[0], LANES))
    row = lambda bi, ti: (0, 0)
    tok = lambda bi, ti: (bi, ti, 0)
    full = lambda a: pl.BlockSpec(a.shape, row)
    consts = [norm_g.reshape(1, d), w_in.T.astype(BF16), lanes(q_a_norm), lanes(kv_a_norm), lanes(k_pe_norm),
              w_uq.T.astype(BF16), w_ukv.T.astype(BF16), lanes(q_nope_norm), lanes(q_pe_norm),
              lanes(k_nope_norm), lanes(invf)]
    tk = min(ATT_TILE, t)
    return pl.pallas_call(
        functools.partial(_mla_pre_kernel, scale=(nope + rope) ** -0.5 * LOG2E),
        grid=(b, s // t),
        in_specs=[pl.BlockSpec((1, t, d), tok),
                  pl.BlockSpec((1, 6, d), lambda bi, ti: (bi, 0, 0)),
                  pl.BlockSpec((1, 1, t), lambda bi, ti: (bi, 0, ti))] + [full(a) for a in consts],
        out_specs=[pl.BlockSpec((1, w, t), lambda bi, ti: (bi, 0, ti)),
                   pl.BlockSpec((1, t, w), tok),
                   pl.BlockSpec((1, t // tk, nh * vd, tk), lambda bi, ti: (bi, ti, 0, 0))],
        out_shape=[jax.ShapeDtypeStruct((b, w, s), BF16),
                   jax.ShapeDtypeStruct((b, s, w), BF16),
                   jax.ShapeDtypeStruct((b, s // tk, nh * vd, tk), BF16)],
        compiler_params=_cparams(("arbitrary", "arbitrary")),
    )(x, mod, posr, *consts)


def _post_kernel(o_ref, wo_ref, x_ref, mod_ref, g_ref, wr_ref, bsel_ref,
                 x_out, h_out, route_out, cnt_out, run_ref, *, d):
    first = (pl.program_id(0) == 0) & (pl.program_id(1) == 0)

    @pl.when(first)
    def _():
        run_ref[...] = jnp.zeros_like(run_ref)

    y = jnp.dot(o_ref[0], wo_ref[...], preferred_element_type=F32)
    x = x_ref[0] + mod_ref[0, 2:3, :] * y
    x_out[0] = x
    h = _adaln_h(x, mod_ref, g_ref, 3, 4)
    t = h.shape[0]

    hh = h.astype(BF16)
    hl = (h - hh.astype(F32)).astype(BF16)
    logits = (jnp.dot(hh, wr_ref[0], preferred_element_type=F32)
              + jnp.dot(hl, wr_ref[0], preferred_element_type=F32)
              + jnp.dot(hh, wr_ref[1], preferred_element_type=F32))

    ng, ne = N_GROUPS, EXPERTS_PER_GROUP

    def route(lt):
        n = lt.shape[1]
        sub = lax.broadcasted_iota(jnp.int32, (SUBLANES, n), 0).astype(F32)

        def first_argmax(val):
            mx = jnp.max(val, axis=0, keepdims=True)
            return jnp.min(jnp.where(val == mx, sub, float(SUBLANES)), axis=0, keepdims=True)

        def pick(val, idx):
            return jnp.sum(jnp.where(sub == idx, val, 0.0), axis=0, keepdims=True)

        gmask = sub < ng
        gl = jnp.where(gmask, lt[:SUBLANES], NEG_INF)
        ge = jnp.exp(gl - jnp.max(gl, axis=0, keepdims=True))
        gprob = ge / jnp.sum(ge, axis=0, keepdims=True)
        gidx = first_argmax(jnp.where(gmask, gprob + bsel_ref[0], NEG_INF))
        ggate = pick(gprob, gidx)

        el = jnp.zeros((SUBLANES, n), F32)
        eb = jnp.zeros((SUBLANES, n), F32)
        for g in range(ng):
            chosen = gidx == float(g)
            el = jnp.where(chosen, lt[SUBLANES * (g + 1):SUBLANES * (g + 2)], el)
            eb = jnp.where(chosen, bsel_ref[g + 1], eb)
        ee = jnp.exp(el - jnp.max(el, axis=0, keepdims=True))
        eprob = ee / jnp.sum(ee, axis=0, keepdims=True)
        sel = eprob + eb
        e1 = first_argmax(sel)
        e2 = first_argmax(jnp.where(sub == e1, NEG_INF, sel))
        p1 = pick(eprob, e1)
        p2 = pick(eprob, e2)
        psum = p1 + p2
        w1 = p1 / psum * ggate
        w2 = p2 / psum * ggate
        swap = e2 < e1
        lo = jnp.where(swap, e2, e1)
        hi = jnp.where(swap, e1, e2)
        combo = gidx * float(N_PAIRS) + lo * (2.0 * ne - 1.0 - lo) * 0.5 + (hi - lo - 1.0)
        return jnp.where(swap, w2, w1), jnp.where(swap, w1, w2), combo

    lt = logits.T
    parts = [route(lt[:, j * (t // 2):(j + 1) * (t // 2)]) for j in range(2)]
    wa, wb, combo = [jnp.concatenate([p[i] for p in parts], axis=1) for i in range(3)]
    sub = lax.broadcasted_iota(jnp.int32, (SUBLANES, t), 0).astype(F32)

    row = lax.broadcasted_iota(jnp.int32, (LANES, t), 0).astype(F32)
    wrow = jnp.where(row == 0.0, wa, jnp.where(row == 1.0, wb, 0.0)).T
    _pack_token_tiles(h_out, h, wrow)

    onehot = row == combo
    oh = jnp.where(onehot, 1.0, 0.0)
    r_i = lax.broadcasted_iota(jnp.int32, (t, t), 0)
    c_i = lax.broadcasted_iota(jnp.int32, (t, t), 1)
    earlier = jnp.where(r_i < c_i, 1.0, 0.0).astype(BF16)
    before = jnp.dot(oh.astype(BF16), earlier, preferred_element_type=F32) + run_ref[...]
    rank = jnp.sum(jnp.where(onehot, before, 0.0), axis=0, keepdims=True)
    run = run_ref[...] + jnp.sum(oh, axis=1, keepdims=True)
    run_ref[...] = run
    cnt_out[...] = run
    route_out[0] = jnp.where(sub == 0.0, combo, jnp.where(sub == 1.0, rank, 0.0))


def _post(o, w_o, x, mod, norm_g, w_group, b_group, w_router, b_router):
    b, s, d = x.shape
    t = min(TOK_TILE, s)
    do = o.shape[2]
    ng, ne = N_GROUPS, EXPERTS_PER_GROUP
    wr = jnp.concatenate([jnp.pad(w_group, ((0, 0), (0, SUBLANES - ng))),
                          jnp.pad(w_router, ((0, 0), (0, LANES - SUBLANES - ng * ne)))], axis=1)
    wr_hi = wr.astype(BF16)
    wr_lo = (wr - wr_hi.astype(F32)).astype(BF16)
    wr2 = jnp.stack([wr_hi, wr_lo])
    bsel = jnp.concatenate([jnp.pad(b_group, (0, SUBLANES - ng))[None, :], b_router], axis=0)[:, :, None]
    row = lambda bi, ti: (0, 0)
    tok = lambda bi, ti: (bi, ti, 0)
    return pl.pallas_call(
        functools.partial(_post_kernel, d=d),
        grid=(b, s // t),
        in_specs=[pl.BlockSpec((1, t, do), tok),
                  pl.BlockSpec((do, d), row),
                  pl.BlockSpec((1, t, d), tok),
                  pl.BlockSpec((1, 6, d), lambda bi, ti: (bi, 0, 0)),
                  pl.BlockSpec((1, d), row),
                  pl.BlockSpec((2, d, LANES), lambda bi, ti: (0, 0, 0)),
                  pl.BlockSpec((ng + 1, ne, 1), lambda bi, ti: (0, 0, 0))],
        out_specs=[pl.BlockSpec((1, t, d), tok),
                   pl.BlockSpec((1, t * SUBLANES, LANES), tok),
                   pl.BlockSpec((1, SUBLANES, t), lambda bi, ti: (bi * (s // t) + ti, 0, 0)),
                   pl.BlockSpec((LANES, 1), row)],
        out_shape=[jax.ShapeDtypeStruct((b, s, d), F32),
                   jax.ShapeDtypeStruct((b, s * SUBLANES, LANES), jnp.uint32),
                   jax.ShapeDtypeStruct((b * s // t, SUBLANES, t), F32),
                   jax.ShapeDtypeStruct((LANES, 1), F32)],
        scratch_shapes=[pltpu.VMEM((LANES, 1), F32)],
        compiler_params=_cparams(("arbitrary", "arbitrary")),
    )(o, w_o.astype(BF16), x, mod, norm_g.reshape(1, d), wr2, bsel)


IN_ROWS = SUBLANES
OUT_ROWS = SUBLANES // 2


def _tok_rows(s, n, per):
    return pl.ds(s, n, stride=per)


def _tok_span(tok, per):
    return pl.ds(pl.multiple_of(tok * per, per), per)


def _pack_words(x):
    half = x.shape[1] // 2
    bits = pltpu.bitcast(x.astype(BF16).astype(F32), jnp.uint32)
    return bits[:, :half] | (bits[:, half:] >> 16)


def _unpack_words(w):
    return pltpu.bitcast(w & jnp.uint32(0xFFFF0000), F32), pltpu.bitcast(w << 16, F32)


def _pack_token_tiles(p_out, h, wrow):
    t, d = h.shape
    word = _pack_words(h)
    nw = d // 2 // LANES
    for s in range(nw):
        p_out[0, _tok_rows(s, t, IN_ROWS), :] = word[:, s * LANES:(s + 1) * LANES]
    p_out[0, _tok_rows(nw, t, IN_ROWS), :] = pltpu.bitcast(wrow, jnp.uint32)
    for s in range(nw + 1, IN_ROWS):
        p_out[0, _tok_rows(s, t, IN_ROWS), :] = jnp.zeros((t, LANES), jnp.uint32)


def _unpack_token_tiles(hs_ref, d):
    nw = d // 2 // LANES
    tm = hs_ref.shape[0] // IN_ROWS
    pairs = [_unpack_words(hs_ref[_tok_rows(s, tm, IN_ROWS), :]) for s in range(nw)]
    x = jnp.concatenate([p[0].astype(BF16) for p in pairs] + [p[1].astype(BF16) for p in pairs], axis=1)
    wrow = pltpu.bitcast(hs_ref[_tok_rows(nw, tm, IN_ROWS), :], F32)
    return x, wrow[:, 0:1], wrow[:, 1:2]


def _row_copy(src_ref, src_row, dst_ref, dst_row, sem, per):
    return pltpu.make_async_copy(src_ref.at[_tok_span(src_row, per), :], dst_ref.at[_tok_span(dst_row, per), :], sem)


def _start_rows(rows, make_copy):
    def start(i, c):
        for u in range(ROW_UNROLL):
            make_copy(i * ROW_UNROLL + u).start(priority=u % 2)
        return c

    lax.fori_loop(0, rows // ROW_UNROLL, start, 0)


def _wait_rows(rows, make_copy):
    def wait(i, c):
        for u in range(ROW_UNROLL):
            make_copy(0).wait()
        return c

    lax.fori_loop(0, rows // ROW_UNROLL, wait, 0)


def _dispatch_kernel(dest_ref, h_ref, init_ref, hs_ref, sem, *, rows):
    del init_ref
    copy = lambda r: _row_copy(h_ref, r, hs_ref, dest_ref[r], sem, IN_ROWS)
    _start_rows(rows, copy)
    _wait_rows(rows, copy)


def _dispatch(dest, h2, n_rows):
    n = h2.shape[0] // IN_ROWS
    rows = min(ROW_TILE, n)
    return pl.pallas_call(
        functools.partial(_dispatch_kernel, rows=rows),
        grid=(n // rows,),
        in_specs=[pl.BlockSpec((rows,), lambda i: (i,), memory_space=pltpu.SMEM),
                  pl.BlockSpec((rows * IN_ROWS, LANES), lambda i: (i, 0)),
                  pl.BlockSpec(memory_space=pl.ANY)],
        out_specs=pl.BlockSpec(memory_space=pl.ANY),
        out_shape=jax.ShapeDtypeStruct((n_rows * IN_ROWS, LANES), jnp.uint32),
        scratch_shapes=[pltpu.SemaphoreType.DMA(())],
        input_output_aliases={2: 0},
        compiler_params=_cparams(("arbitrary",)),
    )(dest, h2, jnp.zeros((n_rows * IN_ROWS, LANES), jnp.uint32))


def _moe_kernel(elo_ref, ehi_ref, blk_ref, nact_ref, hs_ref, w1a_ref, w3a_ref, w2a_ref,
                w1b_ref, w3b_ref, w2b_ref, y_ref, *, d):
    del elo_ref, ehi_ref, blk_ref
    active = pl.program_id(0) < nact_ref[0]

    @pl.when(jnp.logical_not(active))
    def _():
        y_ref[...] = jnp.zeros_like(y_ref)

    @pl.when(active)
    def _():
        x, wa, wb = _unpack_token_tiles(hs_ref, d)

        def expert(w1_ref, w3_ref, wgt):
            a = jnp.dot(x, w1_ref[0], preferred_element_type=F32)
            g = jnp.dot(x, w3_ref[0], preferred_element_type=F32)
            return (a * jax.nn.sigmoid(a) * g * wgt).astype(BF16)

        y = (jnp.dot(expert(w1a_ref, w3a_ref, wa), w2a_ref[0], preferred_element_type=F32)
             + jnp.dot(expert(w1b_ref, w3b_ref, wb), w2b_ref[0], preferred_element_type=F32))
        word = _pack_words(y)
        for s in range(OUT_ROWS):
            y_ref[_tok_rows(s, y.shape[0], OUT_ROWS), :] = word[:, s * LANES:(s + 1) * LANES]


def _moe(hs, e_lo, e_hi, blk, nact, w1, w3, w2):
    n_rows = hs.shape[0] // IN_ROWS
    d = w1.shape[1]
    ff = w1.shape[2]
    tm = MOE_TILE
    n_tiles = n_rows // tm
    assert d == 2 * OUT_ROWS * LANES
    wspec = lambda shape, which: pl.BlockSpec(shape, (lambda j, lo, hi, bk, na: (lo[j], 0, 0)) if which == 0
                                              else (lambda j, lo, hi, bk, na: (hi[j], 0, 0)))
    grid_spec = pltpu.PrefetchScalarGridSpec(
        num_scalar_prefetch=4,
        grid=(n_tiles,),
        in_specs=[pl.BlockSpec((tm * IN_ROWS, LANES), lambda j, lo, hi, bk, na: (bk[j], 0)),
                  wspec((1, d, ff), 0), wspec((1, d, ff), 0), wspec((1, ff, d), 0),
                  wspec((1, d, ff), 1), wspec((1, d, ff), 1), wspec((1, ff, d), 1)],
        out_specs=pl.BlockSpec((tm * OUT_ROWS, LANES), lambda j, lo, hi, bk, na: (j, 0)),
    )
    return pl.pallas_call(
        functools.partial(_moe_kernel, d=d),
        grid_spec=grid_spec,
        out_shape=jax.ShapeDtypeStruct((n_rows * OUT_ROWS, LANES), jnp.uint32),
        compiler_params=_cparams(("arbitrary",)),
    )(e_lo, e_hi, blk, nact, hs, w1, w3, w2, w1, w3, w2)


def _combine_kernel(dest_ref, next_ref, x_ref, gate_ref, y_ref, o_ref, buf_ref, sem, *, rows):
    i = pl.program_id(0)
    slot = i & 1

    def gather(d_ref, sl):
        return lambda r: _row_copy(y_ref, d_ref[r], buf_ref.at[sl], r, sem.at[sl], OUT_ROWS)

    @pl.when(i == 0)
    def _():
        _start_rows(rows, gather(dest_ref, 0))

    @pl.when(i + 1 < pl.num_programs(0))
    def _():
        _start_rows(rows, gather(next_ref, 1 - slot))

    _wait_rows(rows, gather(dest_ref, slot))
    half = OUT_ROWS * LANES
    for s in range(OUT_ROWS):
        hi, lo = _unpack_words(buf_ref[slot, _tok_rows(s, rows, OUT_ROWS), :])
        for off, val in ((s * LANES, hi), (half + s * LANES, lo)):
            cols = slice(off, off + LANES)
            o_ref[:, cols] = x_ref[:, cols] + gate_ref[0, :, cols] * val


def _combine(dest, x2, gate, y, seq):
    n, d = x2.shape
    rows = min(ROW_TILE, seq)
    per_seq = seq // rows
    steps = n // rows
    return pl.pallas_call(
        functools.partial(_combine_kernel, rows=rows),
        grid=(steps,),
        in_specs=[pl.BlockSpec((rows,), lambda i: (i,), memory_space=pltpu.SMEM),
                  pl.BlockSpec((rows,), lambda i: (jnp.minimum(i + 1, steps - 1),), memory_space=pltpu.SMEM),
                  pl.BlockSpec((rows, d), lambda i: (i, 0)),
                  pl.BlockSpec((1, 1, d), lambda i: (i // per_seq, 0, 0)),
                  pl.BlockSpec(memory_space=pl.ANY)],
        out_specs=pl.BlockSpec((rows, d), lambda i: (i, 0)),
        out_shape=jax.ShapeDtypeStruct((n, d), F32),
        scratch_shapes=[pltpu.VMEM((2, rows * OUT_ROWS, LANES), jnp.uint32), pltpu.SemaphoreType.DMA((2,))],
        compiler_params=_cparams(("arbitrary",)),
    )(dest, dest, x2, gate, y)


def _moe_layer(x_new, h2, route, counts, gate_f, w1, w3, w2):
    b, s, d = x_new.shape
    n = b * s
    tm = MOE_TILE
    n_tiles = n // tm + N_COMBOS
    combo = route[:, 0, :].reshape(n).astype(jnp.int32)
    rank = route[:, 1, :].reshape(n).astype(jnp.int32)
    cnt = counts[:N_COMBOS, 0].astype(jnp.int32)
    tiles_per = (cnt + tm - 1) // tm
    tile_end = jnp.cumsum(tiles_per)
    row_off = (tile_end - tiles_per) * tm
    ids = jnp.arange(N_COMBOS, dtype=jnp.int32)

    def lookup(table, idx):
        return jnp.sum(jnp.where(idx[:, None] == ids[None, :], table[None, :], 0), axis=1)

    dest = lookup(row_off, combo) + rank
    nact = tile_end[-1:]
    blk = jnp.minimum(jnp.arange(n_tiles, dtype=jnp.int32), nact[0] - 1)
    tile_combo = jnp.sum((tile_end[None, :] <= blk[:, None]).astype(jnp.int32), axis=1)
    e_lo = lookup(jnp.asarray(_COMBO_LO), tile_combo)
    e_hi = lookup(jnp.asarray(_COMBO_HI), tile_combo)

    hs = _dispatch(dest, h2.reshape(n * SUBLANES, LANES), n_tiles * tm)
    ne = w1.shape[0] * w1.shape[1]
    y = _moe(hs, e_lo, e_hi, blk, nact.astype(jnp.int32),
             w1.reshape(ne, d, -1).astype(BF16), w3.reshape(ne, d, -1).astype(BF16),
             w2.reshape(ne, -1, d).astype(BF16))
    return _combine(dest, x_new.reshape(n, d), gate_f, y, s).reshape(b, s, d)


def kernel(x, c, positions, ada_w, ada_b, norm_mix, norm_ffn, da_w_qkv, da_q_norm, da_k_norm, da_lambda_q1, da_lambda_k1, da_lambda_q2, da_lambda_k2, da_subln, da_w_o, mla_w_in, mla_q_a_norm, mla_kv_a_norm, mla_w_uq, mla_w_ukv, mla_q_nope_norm, mla_q_pe_norm, mla_k_nope_norm, mla_k_pe_norm, mla_w_o, moe_w_group, moe_b_group, moe_w_router, moe_b_router, moe_w1, moe_w3, moe_w2):
    b, s, d = x.shape
    depth = ada_w.shape[0]
    mod_all = _modulation(c, ada_w, ada_b).reshape(depth, b, 6, d)
    posr = positions.astype(F32).reshape(b, 1, s)
    for i in range(depth):
        mod = mod_all[i]
        j = i // 2
        if i % 2 == 0:
            lam_init = 0.8 - 0.6 * math.exp(-0.3 * i)
            q, k, v = _da_pre(x, mod, posr, norm_mix[i], da_w_qkv[j], da_q_norm[j], da_k_norm[j])
            o = _da_attn(q, k, v, da_lambda_q1[j], da_lambda_k1[j], da_lambda_q2[j], da_lambda_k2[j],
                         da_subln[j], lam_init)
            w_o = da_w_o[j]
        else:
            q, k, v = _mla_pre(x, mod, posr, norm_mix[i], mla_w_in[j], mla_q_a_norm[j], mla_kv_a_norm[j],
                               mla_w_uq[j], mla_w_ukv[j], mla_q_nope_norm[j], mla_q_pe_norm[j],
                               mla_k_nope_norm[j], mla_k_pe_norm[j])
            o = _mla_attn(q, k, v)
            w_o = mla_w_o[j]
        x_new, h2, route, counts = _post(o, w_o, x, mod, norm_ffn[i], moe_w_group[i], moe_b_group[i],
                                         moe_w_router[i], moe_b_router[i])
        x = _moe_layer(x_new, h2, route, counts, mod[:, 5:6, :], moe_w1[i], moe_w3[i], moe_w2[i])
    return x
```

```python
import functools
import math

import numpy as np
import jax
import jax.numpy as jnp
from jax import lax
from jax.experimental import pallas as pl
from jax.experimental.pallas import tpu as pltpu

F32 = jnp.float32
BF16 = jnp.bfloat16

ROPE_THETA = 10000.0
EPS = 1e-6
NEG_INF = -1e30
LOG2E = math.log2(math.e)
DA_HEAD_DIM = 64
MLA_HEADS = 16
MLA_NOPE = 64
MLA_ROPE = 32
MLA_V = 64
MLA_Q_RANK = 384
MLA_KV_RANK = 256
N_GROUPS = 4
EXPERTS_PER_GROUP = 8
EXPERT_FF = 256

LANES = 128
VMEM_LIMIT = 56 * 1024 * 1024
TOK_TILE = 512
ATT_TILE = 256
MOE_TILE = 256
ROW_TILE = 1024
ROW_UNROLL = 8
SUBLANES = 8

N_PAIRS = EXPERTS_PER_GROUP * (EXPERTS_PER_GROUP - 1) // 2
N_COMBOS = N_GROUPS * N_PAIRS


def _combo_tables():
    lo_t, hi_t = [], []
    for g in range(N_GROUPS):
        for lo in range(EXPERTS_PER_GROUP):
            for hi in range(lo + 1, EXPERTS_PER_GROUP):
                lo_t.append(g * EXPERTS_PER_GROUP + lo)
                hi_t.append(g * EXPERTS_PER_GROUP + hi)
    return np.asarray(lo_t, np.int32), np.asarray(hi_t, np.int32)


_COMBO_LO, _COMBO_HI = _combo_tables()


def _cparams(sem):
    return pltpu.CompilerParams(dimension_semantics=sem, vmem_limit_bytes=VMEM_LIMIT)


def _adaln_h(x, mod_ref, g_ref, shift_row, scale_row):
    ms = jnp.mean(x * x, axis=-1, keepdims=True)
    h = x * lax.rsqrt(ms + EPS) * g_ref[...]
    return h * (1.0 + mod_ref[0, scale_row:scale_row + 1, :]) + mod_ref[0, shift_row:shift_row + 1, :]


def _mod_kernel(c_ref, w_ref, b_ref, o_ref):
    c = c_ref[...]
    cond = (c * jax.nn.sigmoid(c)).astype(BF16)
    o_ref[0] = jnp.dot(cond, w_ref[0].astype(BF16), preferred_element_type=F32) + b_ref[0]


def _modulation(c, ada_w, ada_b):
    depth, d, n6 = ada_w.shape
    b = c.shape[0]
    tn = 1536
    return pl.pallas_call(
        _mod_kernel,
        grid=(depth, n6 // tn),
        in_specs=[pl.BlockSpec((b, d), lambda i, j: (0, 0)),
                  pl.BlockSpec((1, d, tn), lambda i, j: (i, 0, j)),
                  pl.BlockSpec((1, 1, tn), lambda i, j: (i, 0, j))],
        out_specs=pl.BlockSpec((1, b, tn), lambda i, j: (i, 0, j)),
        out_shape=jax.ShapeDtypeStruct((depth, b, n6), F32),
        compiler_params=_cparams(("arbitrary", "arbitrary")),
    )(c, ada_w, ada_b.reshape(depth, 1, n6))


def _da_pre_kernel(x_ref, mod_ref, pos_ref, g_ref, wt_ref, qg_ref, kg_ref, invf_ref,
                   q_out, k_out, v_out, *, d, scale):
    h = _adaln_h(x_ref[0], mod_ref, g_ref, 0, 1)
    t = h.shape[0]
    qkv = jnp.dot(wt_ref[...], h.T.astype(BF16), preferred_element_type=F32)
    vt = qkv[2 * d:]
    tk = v_out.shape[3]
    for j in range(v_out.shape[1]):
        v_out[0, j] = vt[:, j * tk:(j + 1) * tk].astype(BF16)

    dh = DA_HEAD_DIM
    half = dh // 2
    reps = t // LANES
    ang = jnp.tile(invf_ref[...], (1, reps)) * pos_ref[0]
    cos = jnp.cos(ang)[None, None]
    sin = jnp.sin(ang)[None, None]

    def norm_rope(z, gain_ref):
        z3 = z.reshape(d // dh, dh, t)
        r = lax.rsqrt(jnp.mean(z3 * z3, axis=1, keepdims=True) + EPS)
        z4 = (z3 * r).reshape(d // dh, 2, half, t) * jnp.tile(gain_ref[...], (1, reps)).reshape(1, 2, half, t)
        z1 = z4[:, 0:1]
        z2 = z4[:, 1:2]
        return jnp.concatenate([z1 * cos - z2 * sin, z2 * cos + z1 * sin], axis=1).reshape(d, t)

    q_out[0] = (norm_rope(qkv[:d], qg_ref) * scale).astype(BF16)
    k_out[0] = norm_rope(qkv[d:2 * d], kg_ref).T.astype(BF16)


def _da_pre(x, mod, posr, norm_g, w_qkv, q_norm, k_norm):
    b, s, d = x.shape
    t = min(TOK_TILE, s)
    dh = DA_HEAD_DIM
    invf = (ROPE_THETA ** (-np.arange(0, dh, 2, dtype=np.float32) / dh)).astype(np.float32)
    lanes = lambda v: jnp.broadcast_to(jnp.asarray(v, F32)[:, None], (v.shape[0], LANES))
    row = lambda bi, ti: (0, 0)
    tok = lambda bi, ti: (bi, ti, 0)
    tk = min(ATT_TILE, t)
    return pl.pallas_call(
        functools.partial(_da_pre_kernel, d=d, scale=dh ** -0.5 * LOG2E),
        grid=(b, s // t),
        in_specs=[pl.BlockSpec((1, t, d), tok),
                  pl.BlockSpec((1, 6, d), lambda bi, ti: (bi, 0, 0)),
                  pl.BlockSpec((1, 1, t), lambda bi, ti: (bi, 0, ti)),
                  pl.BlockSpec((1, d), row),
                  pl.BlockSpec((3 * d, d), row),
                  pl.BlockSpec((dh, LANES), row),
                  pl.BlockSpec((dh, LANES), row),
                  pl.BlockSpec((dh // 2, LANES), row)],
        out_specs=[pl.BlockSpec((1, d, t), lambda bi, ti: (bi, 0, ti)),
                   pl.BlockSpec((1, t, d), tok),
                   pl.BlockSpec((1, t // tk, d, tk), lambda bi, ti: (bi, ti, 0, 0))],
        out_shape=[jax.ShapeDtypeStruct((b, d, s), BF16),
                   jax.ShapeDtypeStruct((b, s, d), BF16),
                   jax.ShapeDtypeStruct((b, s // tk, d, tk), BF16)],
        compiler_params=_cparams(("arbitrary", "arbitrary")),
    )(x, mod, posr, norm_g.reshape(1, d), w_qkv.T.astype(BF16), lanes(q_norm), lanes(k_norm), lanes(invf))


def _softmax_block(buf, vt_blk, g, m_ref, l_ref, acc_ref, mask):
    s = buf[0][g]
    if mask is None:
        bmax = buf[1][g]
    else:
        s = jnp.where(mask, s, NEG_INF)
        bmax = jnp.max(s, axis=0, keepdims=True)
    m_prev = m_ref[g]
    m_new = jnp.maximum(m_prev, bmax)
    alpha = jnp.exp2(m_prev - m_new)
    p = jnp.exp2(s - m_new)
    l_ref[g] = alpha * l_ref[g] + jnp.sum(p, axis=0, keepdims=True)
    acc_ref[g] = alpha * acc_ref[g] + jnp.dot(vt_blk, p.astype(BF16), preferred_element_type=F32)
    m_ref[g] = m_new


def _causal_sweep(qi, scores, softmax, sa_ref, sb_ref, diag_mask):
    def fill(buf, kb):
        for g, s in enumerate(scores(kb)):
            buf[0][g] = s
            buf[1][g] = jnp.max(s, axis=0, keepdims=True)

    fill(sa_ref, 0)

    def body(j, c):
        kb = 2 * j
        fill(sb_ref, kb + 1)
        softmax(sa_ref, kb, None)
        fill(sa_ref, kb + 2)
        softmax(sb_ref, kb + 1, None)
        return c

    lax.fori_loop(0, lax.shift_right_logical(qi, 1), body, 0)
    odd = (qi & 1) == 1

    @pl.when(jnp.logical_not(odd))
    def _():
        softmax(sa_ref, qi, diag_mask)

    @pl.when(odd)
    def _():
        fill(sb_ref, qi)
        softmax(sa_ref, qi - 1, None)
        softmax(sb_ref, qi, diag_mask)


def _init_stats(m_ref, l_ref, acc_ref):
    m_ref[...] = jnp.full(m_ref.shape, NEG_INF, F32)
    l_ref[...] = jnp.zeros(l_ref.shape, F32)
    acc_ref[...] = jnp.zeros(acc_ref.shape, F32)


def _causal_mask_t(tk, cols, tq):
    c = lax.broadcasted_iota(jnp.int32, (tk, cols), 0)
    r = lax.broadcasted_iota(jnp.int32, (tk, cols), 1)
    r = jnp.where(r >= tq, r - tq, r)
    return c <= r


def _da_attn_kernel(lq1_ref, lk1_ref, lq2_ref, lk2_ref, sub_ref, qt_ref, k_ref, vt_ref, o_ref,
                    qs_ref, m_ref, l_ref, acc_ref, sa_ref, sb_ref, ma_ref, mb_ref, *, tq, heads, lam_init):
    qi = pl.program_id(2)
    sub = lax.broadcasted_iota(jnp.int32, (LANES, tq), 0)
    for g in range(heads):
        qt = qt_ref[0, g * LANES:(g + 1) * LANES, :]
        zero = jnp.zeros_like(qt)
        qs_ref[g] = jnp.concatenate([jnp.where(sub < DA_HEAD_DIM, qt, zero),
                                     jnp.where(sub >= DA_HEAD_DIM, qt, zero)], axis=1)
    _init_stats(m_ref, l_ref, acc_ref)

    def scores(kb):
        off = pl.multiple_of(kb * tq, tq)
        return [jnp.dot(k_ref[0, pl.ds(off, tq), g * LANES:(g + 1) * LANES], qs_ref[g],
                        preferred_element_type=F32) for g in range(heads)]

    def softmax(s, kb, mask):
        for g in range(heads):
            _softmax_block(s, vt_ref[0, kb, g * LANES:(g + 1) * LANES, :], g, m_ref, l_ref, acc_ref, mask)

    _causal_sweep(qi, scores, softmax, (sa_ref, ma_ref), (sb_ref, mb_ref), _causal_mask_t(tq, 2 * tq, tq))

    lam = (jnp.exp(jnp.sum(lq1_ref[...] * lk1_ref[...], axis=1, keepdims=True))
           - jnp.exp(jnp.sum(lq2_ref[...] * lk2_ref[...], axis=1, keepdims=True)) + lam_init)
    for g in range(heads):
        ot = acc_ref[g] * (1.0 / l_ref[g])
        dd = (ot[:, :tq] - lam * ot[:, tq:]).T
        ms = jnp.mean(dd * dd, axis=-1, keepdims=True)
        o_ref[0, :, g * LANES:(g + 1) * LANES] = (
            dd * lax.rsqrt(ms + EPS) * sub_ref[...] * (1.0 - lam_init)).astype(BF16)


def _da_attn(qt, k, vt, lq1, lk1, lq2, lk2, subln, lam_init):
    b, s, d = k.shape
    heads = 8
    wb = heads * LANES
    tq = vt.shape[3]
    vec = lambda bi, hi, qi: (0, 0)
    return pl.pallas_call(
        functools.partial(_da_attn_kernel, tq=tq, heads=heads, lam_init=lam_init),
        grid=(b, d // wb, s // tq),
        in_specs=[pl.BlockSpec((1, DA_HEAD_DIM), vec)] * 4 + [
            pl.BlockSpec((1, LANES), vec),
            pl.BlockSpec((1, wb, tq), lambda bi, hi, qi: (bi, hi, qi)),
            pl.BlockSpec((1, s, wb), lambda bi, hi, qi: (bi, 0, hi)),
            pl.BlockSpec((1, s // tq, wb, tq), lambda bi, hi, qi: (bi, 0, hi, 0))],
        out_specs=pl.BlockSpec((1, tq, wb), lambda bi, hi, qi: (bi, qi, hi)),
        out_shape=jax.ShapeDtypeStruct((b, s, d), BF16),
        scratch_shapes=[pltpu.VMEM((heads, LANES, 2 * tq), BF16),
                        pltpu.VMEM((heads, 1, 2 * tq), F32),
                        pltpu.VMEM((heads, 1, 2 * tq), F32),
                        pltpu.VMEM((heads, LANES, 2 * tq), F32),
                        pltpu.VMEM((heads, tq, 2 * tq), F32),
                        pltpu.VMEM((heads, tq, 2 * tq), F32),
                        pltpu.VMEM((heads, 1, 2 * tq), F32),
                        pltpu.VMEM((heads, 1, 2 * tq), F32)],
        compiler_params=_cparams(("arbitrary", "arbitrary", "arbitrary")),
    )(lq1.reshape(1, -1), lk1.reshape(1, -1), lq2.reshape(1, -1), lk2.reshape(1, -1),
      subln.reshape(1, -1), qt, k, vt)


def _mla_attn_kernel(qt_ref, k_ref, vt_ref, o_ref, m_ref, l_ref, acc_ref, sa_ref, sb_ref, ma_ref, mb_ref,
                     *, tq, heads):
    qi = pl.program_id(2)
    _init_stats(m_ref, l_ref, acc_ref)

    def scores(kb):
        off = pl.multiple_of(kb * tq, tq)
        return [jnp.dot(k_ref[0, pl.ds(off, tq), g * LANES:(g + 1) * LANES],
                        qt_ref[0, g * LANES:(g + 1) * LANES, :],
                        preferred_element_type=F32) for g in range(heads)]

    def softmax(s, kb, mask):
        for g in range(heads):
            _softmax_block(s, vt_ref[0, kb, g * MLA_V:(g + 1) * MLA_V, :], g, m_ref, l_ref, acc_ref, mask)

    _causal_sweep(qi, scores, softmax, (sa_ref, ma_ref), (sb_ref, mb_ref), _causal_mask_t(tq, tq, tq))
    for p in range(heads // 2):
        ot = jnp.concatenate([acc_ref[2 * p] * (1.0 / l_ref[2 * p]),
                              acc_ref[2 * p + 1] * (1.0 / l_ref[2 * p + 1])], axis=0)
        o_ref[0, :, p * LANES:(p + 1) * LANES] = ot.T.astype(BF16)


def _mla_attn(qt, k, vt):
    b, s, dk = k.shape
    heads = 16
    tq = vt.shape[3]
    return pl.pallas_call(
        functools.partial(_mla_attn_kernel, tq=tq, heads=heads),
        grid=(b, dk // (heads * LANES), s // tq),
        in_specs=[pl.BlockSpec((1, heads * LANES, tq), lambda bi, hi, qi: (bi, hi, qi)),
                  pl.BlockSpec((1, s, heads * LANES), lambda bi, hi, qi: (bi, 0, hi)),
                  pl.BlockSpec((1, s // tq, heads * MLA_V, tq), lambda bi, hi, qi: (bi, 0, hi, 0))],
        out_specs=pl.BlockSpec((1, tq, heads * MLA_V), lambda bi, hi, qi: (bi, qi, hi)),
        out_shape=jax.ShapeDtypeStruct((b, s, dk // LANES * MLA_V), BF16),
        scratch_shapes=[pltpu.VMEM((heads, 1, tq), F32),
                        pltpu.VMEM((heads, 1, tq), F32),
                        pltpu.VMEM((heads, MLA_V, tq), F32),
                        pltpu.VMEM((heads, tq, tq), F32),
                        pltpu.VMEM((heads, tq, tq), F32),
                        pltpu.VMEM((heads, 1, tq), F32),
                        pltpu.VMEM((heads, 1, tq), F32)],
        compiler_params=_cparams(("arbitrary", "arbitrary", "arbitrary")),
    )(qt, k, vt)


def _mla_pre_kernel(x_ref, mod_ref, pos_ref, g_ref, wint_ref, qag_ref, kvag_ref, kpeg_ref,
                    wuqt_ref, wukvt_ref, qng_ref, qpg_ref, kng_ref, invf_ref,
                    q_out, k_out, v_out, *, scale):
    h = _adaln_h(x_ref[0], mod_ref, g_ref, 0, 1)
    t = h.shape[0]
    reps = t // LANES
    nh, nope, rope, vd = MLA_HEADS, MLA_NOPE, MLA_ROPE, MLA_V
    lanes = lambda ref: jnp.tile(ref[...], (1, reps))

    def rms(z, axis):
        return z * lax.rsqrt(jnp.mean(z * z, axis=axis, keepdims=True) + EPS)

    lat = jnp.dot(wint_ref[...], h.T.astype(BF16), preferred_element_type=F32)
    cqn = (rms(lat[:MLA_Q_RANK], 0) * lanes(qag_ref)).astype(BF16)
    ckvn = (rms(lat[MLA_Q_RANK:MLA_Q_RANK + MLA_KV_RANK], 0) * lanes(kvag_ref)).astype(BF16)
    kpen = rms(lat[MLA_Q_RANK + MLA_KV_RANK:], 0) * lanes(kpeg_ref)
    q3 = jnp.dot(wuqt_ref[...], cqn, preferred_element_type=F32).reshape(nh, nope + rope, t)
    kv3 = jnp.dot(wukvt_ref[...], ckvn, preferred_element_type=F32).reshape(nh, nope + vd, t)

    vt = kv3[:, nope:].reshape(nh * vd, t)
    tk = v_out.shape[3]
    for j in range(v_out.shape[1]):
        v_out[0, j] = vt[:, j * tk:(j + 1) * tk].astype(BF16)

    half = rope // 2
    ang = lanes(invf_ref) * pos_ref[0]
    cos = jnp.cos(ang)[None]
    sin = jnp.sin(ang)[None]

    def rotary(z):
        z1 = z[:, :half]
        z2 = z[:, half:]
        return jnp.concatenate([z1 * cos - z2 * sin, z2 * cos + z1 * sin], axis=1)

    pad = jnp.zeros((nh, LANES - nope - rope, t), F32)
    qn = rms(q3[:, :nope], 1) * lanes(qng_ref)[None]
    qp = rotary(rms(q3[:, nope:], 1) * lanes(qpg_ref)[None])
    q_out[0] = (jnp.concatenate([qn, qp, pad], axis=1).reshape(nh * LANES, t) * scale).astype(BF16)
    kn = rms(kv3[:, :nope], 1) * lanes(kng_ref)[None]
    kp = jnp.broadcast_to(rotary(kpen[None]), (nh, rope, t))
    k_out[0] = jnp.concatenate([kn, kp, pad], axis=1).reshape(nh * LANES, t).T.astype(BF16)


def _mla_pre(x, mod, posr, norm_g, w_in, q_a_norm, kv_a_norm, w_uq, w_ukv,
             q_nope_norm, q_pe_norm, k_nope_norm, k_pe_norm):
    b, s, d = x.shape
    t = min(TOK_TILE, s)
    nh, nope, rope, vd = MLA_HEADS, MLA_NOPE, MLA_ROPE, MLA_V
    w = nh * LANES
    invf = (ROPE_THETA ** (-np.arange(0, rope, 2, dtype=np.float32) / rope)).astype(np.float32)
    lanes = lambda v: jnp.broadcast_to(jnp.asarray(v, F32)[:, None], (v.shape[0], LANES))
    row = lambda bi, ti: (0, 0)
    tok = lambda bi, ti: (bi, ti, 0)
    full = lambda a: pl.BlockSpec(a.shape, row)
    consts = [norm_g.reshape(1, d), w_in.T.astype(BF16), lanes(q_a_norm), lanes(kv_a_norm), lanes(k_pe_norm),
              w_uq.T.astype(BF16), w_ukv.T.astype(BF16), lanes(q_nope_norm), lanes(q_pe_norm),
              lanes(k_nope_norm), lanes(invf)]
    tk = min(ATT_TILE, t)
    return pl.pallas_call(
        functools.partial(_mla_pre_kernel, scale=(nope + rope) ** -0.5 * LOG2E),
        grid=(b, s // t),
        in_specs=[pl.BlockSpec((1, t, d), tok),
                  pl.BlockSpec((1, 6, d), lambda bi, ti: (bi, 0, 0)),
                  pl.BlockSpec((1, 1, t), lambda bi, ti: (bi, 0, ti))] + [full(a) for a in consts],
        out_specs=[pl.BlockSpec((1, w, t), lambda bi, ti: (bi, 0, ti)),
                   pl.BlockSpec((1, t, w), tok),
                   pl.BlockSpec((1, t // tk, nh * vd, tk), lambda bi, ti: (bi, ti, 0, 0))],
        out_shape=[jax.ShapeDtypeStruct((b, w, s), BF16),
                   jax.ShapeDtypeStruct((b, s, w), BF16),
                   jax.ShapeDtypeStruct((b, s // tk, nh * vd, tk), BF16)],
        compiler_params=_cparams(("arbitrary", "arbitrary")),
    )(x, mod, posr, *consts)


def _post_kernel(o_ref, wo_ref, x_ref, mod_ref, g_ref, wr_ref, bsel_ref,
                 x_out, h_out, route_out, cnt_out, run_ref, *, d):
    first = (pl.program_id(0) == 0) & (pl.program_id(1) == 0)

    @pl.when(first)
    def _():
        run_ref[...] = jnp.zeros_like(run_ref)

    y = jnp.dot(o_ref[0], wo_ref[...], preferred_element_type=F32)
    x = x_ref[0] + mod_ref[0, 2:3, :] * y
    x_out[0] = x
    h = _adaln_h(x, mod_ref, g_ref, 3, 4)
    t = h.shape[0]

    hh = h.astype(BF16)
    hl = (h - hh.astype(F32)).astype(BF16)
    logits = (jnp.dot(hh, wr_ref[0], preferred_element_type=F32)
              + jnp.dot(hl, wr_ref[0], preferred_element_type=F32)
              + jnp.dot(hh, wr_ref[1], preferred_element_type=F32))

    ng, ne = N_GROUPS, EXPERTS_PER_GROUP

    def route(lt):
        n = lt.shape[1]
        sub = lax.broadcasted_iota(jnp.int32, (SUBLANES, n), 0).astype(F32)

        def first_argmax(val):
            mx = jnp.max(val, axis=0, keepdims=True)
            return jnp.min(jnp.where(val == mx, sub, float(SUBLANES)), axis=0, keepdims=True)

        def pick(val, idx):
            return jnp.sum(jnp.where(sub == idx, val, 0.0), axis=0, keepdims=True)

        gmask = sub < ng
        gl = jnp.where(gmask, lt[:SUBLANES], NEG_INF)
        ge = jnp.exp(gl - jnp.max(gl, axis=0, keepdims=True))
        gprob = ge / jnp.sum(ge, axis=0, keepdims=True)
        gidx = first_argmax(jnp.where(gmask, gprob + bsel_ref[0], NEG_INF))
        ggate = pick(gprob, gidx)

        el = jnp.zeros((SUBLANES, n), F32)
        eb = jnp.zeros((SUBLANES, n), F32)
        for g in range(ng):
            chosen = gidx == float(g)
            el = jnp.where(chosen, lt[SUBLANES * (g + 1):SUBLANES * (g + 2)], el)
            eb = jnp.where(chosen, bsel_ref[g + 1], eb)
        ee = jnp.exp(el - jnp.max(el, axis=0, keepdims=True))
        eprob = ee / jnp.sum(ee, axis=0, keepdims=True)
        sel = eprob + eb
        e1 = first_argmax(sel)
        e2 = first_argmax(jnp.where(sub == e1, NEG_INF, sel))
        p1 = pick(eprob, e1)
        p2 = pick(eprob, e2)
        psum = p1 + p2
        w1 = p1 / psum * ggate
        w2 = p2 / psum * ggate
        swap = e2 < e1
        lo = jnp.where(swap, e2, e1)
        hi = jnp.where(swap, e1, e2)
        combo = gidx * float(N_PAIRS) + lo * (2.0 * ne - 1.0 - lo) * 0.5 + (hi - lo - 1.0)
        return jnp.where(swap, w2, w1), jnp.where(swap, w1, w2), combo

    lt = logits.T
    parts = [route(lt[:, j * (t // 2):(j + 1) * (t // 2)]) for j in range(2)]
    wa, wb, combo = [jnp.concatenate([p[i] for p in parts], axis=1) for i in range(3)]
    sub = lax.broadcasted_iota(jnp.int32, (SUBLANES, t), 0).astype(F32)

    row = lax.broadcasted_iota(jnp.int32, (LANES, t), 0).astype(F32)
    wrow = jnp.where(row == 0.0, wa, jnp.where(row == 1.0, wb, 0.0)).T
    _pack_token_tiles(h_out, h, wrow)

    onehot = row == combo
    oh = jnp.where(onehot, 1.0, 0.0)
    r_i = lax.broadcasted_iota(jnp.int32, (t, t), 0)
    c_i = lax.broadcasted_iota(jnp.int32, (t, t), 1)
    earlier = jnp.where(r_i < c_i, 1.0, 0.0).astype(BF16)
    before = jnp.dot(oh.astype(BF16), earlier, preferred_element_type=F32) + run_ref[...]
    rank = jnp.sum(jnp.where(onehot, before, 0.0), axis=0, keepdims=True)
    run = run_ref[...] + jnp.sum(oh, axis=1, keepdims=True)
    run_ref[...] = run
    cnt_out[...] = run
    route_out[0] = jnp.where(sub == 0.0, combo, jnp.where(sub == 1.0, rank, 0.0))


def _post(o, w_o, x, mod, norm_g, w_group, b_group, w_router, b_router):
    b, s, d = x.shape
    t = min(TOK_TILE, s)
    do = o.shape[2]
    ng, ne = N_GROUPS, EXPERTS_PER_GROUP
    wr = jnp.concatenate([jnp.pad(w_group, ((0, 0), (0, SUBLANES - ng))),
                          jnp.pad(w_router, ((0, 0), (0, LANES - SUBLANES - ng * ne)))], axis=1)
    wr_hi = wr.astype(BF16)
    wr_lo = (wr - wr_hi.astype(F32)).astype(BF16)
    wr2 = jnp.stack([wr_hi, wr_lo])
    bsel = jnp.concatenate([jnp.pad(b_group, (0, SUBLANES - ng))[None, :], b_router], axis=0)[:, :, None]
    row = lambda bi, ti: (0, 0)
    tok = lambda bi, ti: (bi, ti, 0)
    return pl.pallas_call(
        functools.partial(_post_kernel, d=d),
        grid=(b, s // t),
        in_specs=[pl.BlockSpec((1, t, do), tok),
                  pl.BlockSpec((do, d), row),
                  pl.BlockSpec((1, t, d), tok),
                  pl.BlockSpec((1, 6, d), lambda bi, ti: (bi, 0, 0)),
                  pl.BlockSpec((1, d), row),
                  pl.BlockSpec((2, d, LANES), lambda bi, ti: (0, 0, 0)),
                  pl.BlockSpec((ng + 1, ne, 1), lambda bi, ti: (0, 0, 0))],
        out_specs=[pl.BlockSpec((1, t, d), tok),
                   pl.BlockSpec((1, t * SUBLANES, LANES), tok),
                   pl.BlockSpec((1, SUBLANES, t), lambda bi, ti: (bi * (s // t) + ti, 0, 0)),
                   pl.BlockSpec((LANES, 1), row)],
        out_shape=[jax.ShapeDtypeStruct((b, s, d), F32),
                   jax.ShapeDtypeStruct((b, s * SUBLANES, LANES), jnp.uint32),
                   jax.ShapeDtypeStruct((b * s // t, SUBLANES, t), F32),
                   jax.ShapeDtypeStruct((LANES, 1), F32)],
        scratch_shapes=[pltpu.VMEM((LANES, 1), F32)],
        compiler_params=_cparams(("arbitrary", "arbitrary")),
    )(o, w_o.astype(BF16), x, mod, norm_g.reshape(1, d), wr2, bsel)


IN_ROWS = SUBLANES
OUT_ROWS = SUBLANES // 2


def _tok_rows(s, n, per):
    return pl.ds(s, n, stride=per)


def _tok_span(tok, per):
    return pl.ds(pl.multiple_of(tok * per, per), per)


def _pack_words(x):
    half = x.shape[1] // 2
    bits = pltpu.bitcast(x.astype(BF16).astype(F32), jnp.uint32)
    return bits[:, :half] | (bits[:, half:] >> 16)


def _unpack_words(w):
    return pltpu.bitcast(w & jnp.uint32(0xFFFF0000), F32), pltpu.bitcast(w << 16, F32)


def _pack_token_tiles(p_out, h, wrow):
    t, d = h.shape
    word = _pack_words(h)
    nw = d // 2 // LANES
    for s in range(nw):
        p_out[0, _tok_rows(s, t, IN_ROWS), :] = word[:, s * LANES:(s + 1) * LANES]
    p_out[0, _tok_rows(nw, t, IN_ROWS), :] = pltpu.bitcast(wrow, jnp.uint32)
    for s in range(nw + 1, IN_ROWS):
        p_out[0, _tok_rows(s, t, IN_ROWS), :] = jnp.zeros((t, LANES), jnp.uint32)


def _unpack_token_tiles(hs_ref, d):
    nw = d // 2 // LANES
    tm = hs_ref.shape[0] // IN_ROWS
    pairs = [_unpack_words(hs_ref[_tok_rows(s, tm, IN_ROWS), :]) for s in range(nw)]
    x = jnp.concatenate([p[0].astype(BF16) for p in pairs] + [p[1].astype(BF16) for p in pairs], axis=1)
    wrow = pltpu.bitcast(hs_ref[_tok_rows(nw, tm, IN_ROWS), :], F32)
    return x, wrow[:, 0:1], wrow[:, 1:2]


def _row_copy(src_ref, src_row, dst_ref, dst_row, sem, per):
    return pltpu.make_async_copy(src_ref.at[_tok_span(src_row, per), :], dst_ref.at[_tok_span(dst_row, per), :], sem)


def _start_rows(rows, make_copy):
    def start(i, c):
        for u in range(ROW_UNROLL):
            make_copy(i * ROW_UNROLL + u).start(priority=u % 2)
        return c

    lax.fori_loop(0, rows // ROW_UNROLL, start, 0)


def _wait_rows(rows, make_copy):
    def wait(i, c):
        for u in range(ROW_UNROLL):
            make_copy(0).wait()
        return c

    lax.fori_loop(0, rows // ROW_UNROLL, wait, 0)


def _dispatch_kernel(dest_ref, h_ref, init_ref, hs_ref, sem, *, rows):
    del init_ref
    copy = lambda r: _row_copy(h_ref, r, hs_ref, dest_ref[r], sem, IN_ROWS)
    _start_rows(rows, copy)
    _wait_rows(rows, copy)


def _dispatch(dest, h2, n_rows):
    n = h2.shape[0] // IN_ROWS
    rows = min(ROW_TILE, n)
    return pl.pallas_call(
        functools.partial(_dispatch_kernel, rows=rows),
        grid=(n // rows,),
        in_specs=[pl.BlockSpec((rows,), lambda i: (i,), memory_space=pltpu.SMEM),
                  pl.BlockSpec((rows * IN_ROWS, LANES), lambda i: (i, 0)),
                  pl.BlockSpec(memory_space=pl.ANY)],
        out_specs=pl.BlockSpec(memory_space=pl.ANY),
        out_shape=jax.ShapeDtypeStruct((n_rows * IN_ROWS, LANES), jnp.uint32),
        scratch_shapes=[pltpu.SemaphoreType.DMA(())],
        input_output_aliases={2: 0},
        compiler_params=_cparams(("arbitrary",)),
    )(dest, h2, jnp.zeros((n_rows * IN_ROWS, LANES), jnp.uint32))


def _moe_kernel(elo_ref, ehi_ref, blk_ref, nact_ref, hs_ref, w1a_ref, w3a_ref, w2a_ref,
                w1b_ref, w3b_ref, w2b_ref, y_ref, *, d):
    del elo_ref, ehi_ref, blk_ref
    active = pl.program_id(0) < nact_ref[0]

    @pl.when(jnp.logical_not(active))
    def _():
        y_ref[...] = jnp.zeros_like(y_ref)

    @pl.when(active)
    def _():
        x, wa, wb = _unpack_token_tiles(hs_ref, d)

        def expert(w1_ref, w3_ref, wgt):
            a = jnp.dot(x, w1_ref[0], preferred_element_type=F32)
            g = jnp.dot(x, w3_ref[0], preferred_element_type=F32)
            return (a * jax.nn.sigmoid(a) * g * wgt).astype(BF16)

        y = (jnp.dot(expert(w1a_ref, w3a_ref, wa), w2a_ref[0], preferred_element_type=F32)
             + jnp.dot(expert(w1b_ref, w3b_ref, wb), w2b_ref[0], preferred_element_type=F32))
        word = _pack_words(y)
        for s in range(OUT_ROWS):
            y_ref[_tok_rows(s, y.shape[0], OUT_ROWS), :] = word[:, s * LANES:(s + 1) * LANES]


def _moe(hs, e_lo, e_hi, blk, nact, w1, w3, w2):
    n_rows = hs.shape[0] // IN_ROWS
    d = w1.shape[1]
    ff = w1.shape[2]
    tm = MOE_TILE
    n_tiles = n_rows // tm
    assert d == 2 * OUT_ROWS * LANES
    wspec = lambda shape, which: pl.BlockSpec(shape, (lambda j, lo, hi, bk, na: (lo[j], 0, 0)) if which == 0
                                              else (lambda j, lo, hi, bk, na: (hi[j], 0, 0)))
    grid_spec = pltpu.PrefetchScalarGridSpec(
        num_scalar_prefetch=4,
        grid=(n_tiles,),
        in_specs=[pl.BlockSpec((tm * IN_ROWS, LANES), lambda j, lo, hi, bk, na: (bk[j], 0)),
                  wspec((1, d, ff), 0), wspec((1, d, ff), 0), wspec((1, ff, d), 0),
                  wspec((1, d, ff), 1), wspec((1, d, ff), 1), wspec((1, ff, d), 1)],
        out_specs=pl.BlockSpec((tm * OUT_ROWS, LANES), lambda j, lo, hi, bk, na: (j, 0)),
    )
    return pl.pallas_call(
        functools.partial(_moe_kernel, d=d),
        grid_spec=grid_spec,
        out_shape=jax.ShapeDtypeStruct((n_rows * OUT_ROWS, LANES), jnp.uint32),
        compiler_params=_cparams(("arbitrary",)),
    )(e_lo, e_hi, blk, nact, hs, w1, w3, w2, w1, w3, w2)


def _combine_kernel(dest_ref, next_ref, x_ref, gate_ref, y_ref, o_ref, buf_ref, sem, *, rows):
    i = pl.program_id(0)
    slot = i & 1

    def gather(d_ref, sl):
        return lambda r: _row_copy(y_ref, d_ref[r], buf_ref.at[sl], r, sem.at[sl], OUT_ROWS)

    @pl.when(i == 0)
    def _():
        _start_rows(rows, gather(dest_ref, 0))

    @pl.when(i + 1 < pl.num_programs(0))
    def _():
        _start_rows(rows, gather(next_ref, 1 - slot))

    _wait_rows(rows, gather(dest_ref, slot))
    half = OUT_ROWS * LANES
    for s in range(OUT_ROWS):
        hi, lo = _unpack_words(buf_ref[slot, _tok_rows(s, rows, OUT_ROWS), :])
        for off, val in ((s * LANES, hi), (half + s * LANES, lo)):
            cols = slice(off, off + LANES)
            o_ref[:, cols] = x_ref[:, cols] + gate_ref[0, :, cols] * val


def _combine(dest, x2, gate, y, seq):
    n, d = x2.shape
    rows = min(ROW_TILE, seq)
    per_seq = seq // rows
    steps = n // rows
    return pl.pallas_call(
        functools.partial(_combine_kernel, rows=rows),
        grid=(steps,),
        in_specs=[pl.BlockSpec((rows,), lambda i: (i,), memory_space=pltpu.SMEM),
                  pl.BlockSpec((rows,), lambda i: (jnp.minimum(i + 1, steps - 1),), memory_space=pltpu.SMEM),
                  pl.BlockSpec((rows, d), lambda i: (i, 0)),
                  pl.BlockSpec((1, 1, d), lambda i: (i // per_seq, 0, 0)),
                  pl.BlockSpec(memory_space=pl.ANY)],
        out_specs=pl.BlockSpec((rows, d), lambda i: (i, 0)),
        out_shape=jax.ShapeDtypeStruct((n, d), F32),
        scratch_shapes=[pltpu.VMEM((2, rows * OUT_ROWS, LANES), jnp.uint32), pltpu.SemaphoreType.DMA((2,))],
        compiler_params=_cparams(("arbitrary",)),
    )(dest, dest, x2, gate, y)


def _moe_layer(x_new, h2, route, counts, gate_f, w1, w3, w2):
    b, s, d = x_new.shape
    n = b * s
    tm = MOE_TILE
    n_tiles = n // tm + N_COMBOS
    combo = route[:, 0, :].reshape(n).astype(jnp.int32)
    rank = route[:, 1, :].reshape(n).astype(jnp.int32)
    cnt = counts[:N_COMBOS, 0].astype(jnp.int32)
    tiles_per = (cnt + tm - 1) // tm
    tile_end = jnp.cumsum(tiles_per)
    row_off = (tile_end - tiles_per) * tm
    ids = jnp.arange(N_COMBOS, dtype=jnp.int32)

    def lookup(table, idx):
        return jnp.sum(jnp.where(idx[:, None] == ids[None, :], table[None, :], 0), axis=1)

    dest = lookup(row_off, combo) + rank
    nact = tile_end[-1:]
    blk = jnp.minimum(jnp.arange(n_tiles, dtype=jnp.int32), nact[0] - 1)
    tile_combo = jnp.sum((tile_end[None, :] <= blk[:, None]).astype(jnp.int32), axis=1)
    e_lo = lookup(jnp.asarray(_COMBO_LO), tile_combo)
    e_hi = lookup(jnp.asarray(_COMBO_HI), tile_combo)

    hs = _dispatch(dest, h2.reshape(n * SUBLANES, LANES), n_tiles * tm)
    ne = w1.shape[0] * w1.shape[1]
    y = _moe(hs, e_lo, e_hi, blk, nact.astype(jnp.int32),
             w1.reshape(ne, d, -1).astype(BF16), w3.reshape(ne, d, -1).astype(BF16),
             w2.reshape(ne, -1, d).astype(BF16))
    return _combine(dest, x_new.reshape(n, d), gate_f, y, s).reshape(b, s, d)


def kernel(x, c, positions, ada_w, ada_b, norm_mix, norm_ffn, da_w_qkv, da_q_norm, da_k_norm, da_lambda_q1, da_lambda_k1, da_lambda_q2, da_lambda_k2, da_subln, da_w_o, mla_w_in, mla_q_a_norm, mla_kv_a_norm, mla_w_uq, mla_w_ukv, mla_q_nope_norm, mla_q_pe_norm, mla_k_nope_norm, mla_k_pe_norm, mla_w_o, moe_w_group, moe_b_group, moe_w_router, moe_b_router, moe_w1, moe_w3, moe_w2):
    b, s, d = x.shape
    depth = ada_w.shape[0]
    mod_all = _modulation(c, ada_w, ada_b).reshape(depth, b, 6, d)
    posr = positions.astype(F32).reshape(b, 1, s)
    for i in range(depth):
        mod = mod_all[i]
        j = i // 2
        if i % 2 == 0:
            lam_init = 0.8 - 0.6 * math.exp(-0.3 * i)
            q, k, v = _da_pre(x, mod, posr, norm_mix[i], da_w_qkv[j], da_q_norm[j], da_k_norm[j])
            o = _da_attn(q, k, v, da_lambda_q1[j], da_lambda_k1[j], da_lambda_q2[j], da_lambda_k2[j],
                         da_subln[j], lam_init)
            w_o = da_w_o[j]
        else:
            q, k, v = _mla_pre(x, mod, posr, norm_mix[i], mla_w_in[j], mla_q_a_norm[j], mla_kv_a_norm[j],
                               mla_w_uq[j], mla_w_ukv[j], mla_q_nope_norm[j], mla_q_pe_norm[j],
                               mla_k_nope_norm[j], mla_k_pe_norm[j])
            o = _mla_attn(q, k, v)
            w_o = mla_w_o[j]
        x_new, h2, route, counts = _post(o, w_o, x, mod, norm_ffn[i], moe_w_group[i], moe_b_group[i],
                                         moe_w_router[i], moe_b_router[i])
        x = _moe_layer(x_new, h2, route, counts, mod[:, 5:6, :], moe_w1[i], moe_w3[i], moe_w2[i])
    return x
```

```python
import functools
import math

import numpy as np
import jax
import jax.numpy as jnp
from jax import lax
from jax.experimental import pallas as pl
from jax.experimental.pallas import tpu as pltpu

F32 = jnp.float32
BF16 = jnp.bfloat16

ROPE_THETA = 10000.0
EPS = 1e-6
NEG_INF = -1e30
LOG2E = math.log2(math.e)
DA_HEAD_DIM = 64
MLA_HEADS = 16
MLA_NOPE = 64
MLA_ROPE = 32
MLA_V = 64
MLA_Q_RANK = 384
MLA_KV_RANK = 256
N_GROUPS = 4
EXPERTS_PER_GROUP = 8

LANES = 128
VMEM_LIMIT = 56 * 1024 * 1024
TOK_TILE = 512
POST_TILE = 1024
ATT_TILE = 256
MOE_TILE = 256
ROW_TILE = 2048
ROW_UNROLL = 8
SUBLANES = 8

N_PAIRS = EXPERTS_PER_GROUP * (EXPERTS_PER_GROUP - 1) // 2
N_COMBOS = N_GROUPS * N_PAIRS


def _combo_tables():
    lo_t, hi_t = [], []
    for g in range(N_GROUPS):
        for lo in range(EXPERTS_PER_GROUP):
            for hi in range(lo + 1, EXPERTS_PER_GROUP):
                lo_t.append(g * EXPERTS_PER_GROUP + lo)
                hi_t.append(g * EXPERTS_PER_GROUP + hi)
    return np.asarray(lo_t, np.int32), np.asarray(hi_t, np.int32)


_COMBO_LO, _COMBO_HI = _combo_tables()


def _cparams(sem):
    return pltpu.CompilerParams(dimension_semantics=sem, vmem_limit_bytes=VMEM_LIMIT)


def _adaln_h(x, mod_ref, g_ref, shift_row, scale_row):
    ms = jnp.mean(x * x, axis=-1, keepdims=True)
    h = x * lax.rsqrt(ms + EPS) * g_ref[...]
    return h * (1.0 + mod_ref[0, scale_row:scale_row + 1, :]) + mod_ref[0, shift_row:shift_row + 1, :]


def _mod_kernel(c_ref, w_ref, b_ref, o_ref):
    c = c_ref[...]
    cond = (c * jax.nn.sigmoid(c)).astype(BF16)
    o_ref[0] = jnp.dot(cond, w_ref[0].astype(BF16), preferred_element_type=F32) + b_ref[0]


def _modulation(c, ada_w, ada_b):
    depth, d, n6 = ada_w.shape
    b = c.shape[0]
    tn = 1536
    return pl.pallas_call(
        _mod_kernel,
        grid=(depth, n6 // tn),
        in_specs=[pl.BlockSpec((b, d), lambda i, j: (0, 0)),
                  pl.BlockSpec((1, d, tn), lambda i, j: (i, 0, j)),
                  pl.BlockSpec((1, 1, tn), lambda i, j: (i, 0, j))],
        out_specs=pl.BlockSpec((1, b, tn), lambda i, j: (i, 0, j)),
        out_shape=jax.ShapeDtypeStruct((depth, b, n6), F32),
        compiler_params=_cparams(("arbitrary", "arbitrary")),
    )(c, ada_w, ada_b.reshape(depth, 1, n6))


def _da_pre_kernel(x_ref, mod_ref, pos_ref, g_ref, wt_ref, qg_ref, kg_ref, invf_ref,
                   q_out, k_out, v_out, *, d, scale):
    h = _adaln_h(x_ref[0], mod_ref, g_ref, 0, 1)
    t = h.shape[0]
    qkv = jnp.dot(wt_ref[...], h.T.astype(BF16), preferred_element_type=F32)
    vt = qkv[2 * d:]
    tk = v_out.shape[3]
    for j in range(v_out.shape[1]):
        v_out[0, j] = vt[:, j * tk:(j + 1) * tk].astype(BF16)

    dh = DA_HEAD_DIM
    half = dh // 2
    reps = t // LANES
    ang = jnp.tile(invf_ref[...], (1, reps)) * pos_ref[0]
    cos = jnp.cos(ang)[None, None]
    sin = jnp.sin(ang)[None, None]

    def norm_rope(z, gain_ref):
        z3 = z.reshape(d // dh, dh, t)
        r = lax.rsqrt(jnp.mean(z3 * z3, axis=1, keepdims=True) + EPS)
        z4 = (z3 * r).reshape(d // dh, 2, half, t) * jnp.tile(gain_ref[...], (1, reps)).reshape(1, 2, half, t)
        z1 = z4[:, 0:1]
        z2 = z4[:, 1:2]
        return jnp.concatenate([z1 * cos - z2 * sin, z2 * cos + z1 * sin], axis=1).reshape(d, t)

    q_out[0] = (norm_rope(qkv[:d], qg_ref) * scale).astype(BF16)
    k_out[0] = norm_rope(qkv[d:2 * d], kg_ref).T.astype(BF16)


def _da_pre(x, mod, posr, norm_g, w_qkv, q_norm, k_norm):
    b, s, d = x.shape
    t = min(TOK_TILE, s)
    dh = DA_HEAD_DIM
    invf = (ROPE_THETA ** (-np.arange(0, dh, 2, dtype=np.float32) / dh)).astype(np.float32)
    lanes = lambda v: jnp.broadcast_to(jnp.asarray(v, F32)[:, None], (v.shape[0], LANES))
    row = lambda bi, ti: (0, 0)
    tok = lambda bi, ti: (bi, ti, 0)
    tk = min(ATT_TILE, t)
    return pl.pallas_call(
        functools.partial(_da_pre_kernel, d=d, scale=dh ** -0.5 * LOG2E),
        grid=(b, s // t),
        in_specs=[pl.BlockSpec((1, t, d), tok),
                  pl.BlockSpec((1, 6, d), lambda bi, ti: (bi, 0, 0)),
                  pl.BlockSpec((1, 1, t), lambda bi, ti: (bi, 0, ti)),
                  pl.BlockSpec((1, d), row),
                  pl.BlockSpec((3 * d, d), row),
                  pl.BlockSpec((dh, LANES), row),
                  pl.BlockSpec((dh, LANES), row),
                  pl.BlockSpec((dh // 2, LANES), row)],
        out_specs=[pl.BlockSpec((1, d, t), lambda bi, ti: (bi, 0, ti)),
                   pl.BlockSpec((1, t, d), tok),
                   pl.BlockSpec((1, t // tk, d, tk), lambda bi, ti: (bi, ti, 0, 0))],
        out_shape=[jax.ShapeDtypeStruct((b, d, s), BF16),
                   jax.ShapeDtypeStruct((b, s, d), BF16),
                   jax.ShapeDtypeStruct((b, s // tk, d, tk), BF16)],
        compiler_params=_cparams(("arbitrary", "arbitrary")),
    )(x, mod, posr, norm_g.reshape(1, d), w_qkv.T.astype(BF16), lanes(q_norm), lanes(k_norm), lanes(invf))


def _softmax_block(buf, vt_blk, g, m_ref, l_ref, acc_ref, mask):
    s = buf[0][g]
    if mask is None:
        bmax = buf[1][g]
    else:
        s = jnp.where(mask, s, NEG_INF)
        bmax = jnp.max(s, axis=0, keepdims=True)
    m_prev = m_ref[g]
    m_new = jnp.maximum(m_prev, bmax)
    alpha = jnp.exp2(m_prev - m_new)
    p = jnp.exp2(s - m_new)
    l_ref[g] = alpha * l_ref[g] + jnp.sum(p, axis=0, keepdims=True)
    acc_ref[g] = alpha * acc_ref[g] + jnp.dot(vt_blk, p.astype(BF16), preferred_element_type=F32)
    m_ref[g] = m_new


def _causal_sweep(qi, scores, softmax, sa_ref, sb_ref, diag_mask):
    def fill(buf, kb):
        for g, s in enumerate(scores(kb)):
            buf[0][g] = s
            buf[1][g] = jnp.max(s, axis=0, keepdims=True)

    fill(sa_ref, 0)

    def body(j, c):
        kb = 2 * j
        fill(sb_ref, kb + 1)
        softmax(sa_ref, kb, None)
        fill(sa_ref, kb + 2)
        softmax(sb_ref, kb + 1, None)
        return c

    lax.fori_loop(0, lax.shift_right_logical(qi, 1), body, 0)
    odd = (qi & 1) == 1

    @pl.when(jnp.logical_not(odd))
    def _():
        softmax(sa_ref, qi, diag_mask)

    @pl.when(odd)
    def _():
        fill(sb_ref, qi)
        softmax(sa_ref, qi - 1, None)
        softmax(sb_ref, qi, diag_mask)


def _init_stats(m_ref, l_ref, acc_ref):
    m_ref[...] = jnp.full(m_ref.shape, NEG_INF, F32)
    l_ref[...] = jnp.zeros(l_ref.shape, F32)
    acc_ref[...] = jnp.zeros(acc_ref.shape, F32)


def _causal_mask_t(tk, cols, tq):
    c = lax.broadcasted_iota(jnp.int32, (tk, cols), 0)
    r = lax.broadcasted_iota(jnp.int32, (tk, cols), 1)
    r = jnp.where(r >= tq, r - tq, r)
    return c <= r


def _da_attn_kernel(lq1_ref, lk1_ref, lq2_ref, lk2_ref, sub_ref, qt_ref, k_ref, vt_ref, o_ref,
                    qs_ref, m_ref, l_ref, acc_ref, sa_ref, sb_ref, ma_ref, mb_ref, *, tq, heads, lam_init):
    qi = pl.program_id(2)
    sub = lax.broadcasted_iota(jnp.int32, (LANES, tq), 0)
    for g in range(heads):
        qt = qt_ref[0, g * LANES:(g + 1) * LANES, :]
        zero = jnp.zeros_like(qt)
        qs_ref[g] = jnp.concatenate([jnp.where(sub < DA_HEAD_DIM, qt, zero),
                                     jnp.where(sub >= DA_HEAD_DIM, qt, zero)], axis=1)
    _init_stats(m_ref, l_ref, acc_ref)

    def scores(kb):
        off = pl.multiple_of(kb * tq, tq)
        return [jnp.dot(k_ref[0, pl.ds(off, tq), g * LANES:(g + 1) * LANES], qs_ref[g],
                        preferred_element_type=F32) for g in range(heads)]

    def softmax(s, kb, mask):
        for g in range(heads):
            _softmax_block(s, vt_ref[0, kb, g * LANES:(g + 1) * LANES, :], g, m_ref, l_ref, acc_ref, mask)

    _causal_sweep(qi, scores, softmax, (sa_ref, ma_ref), (sb_ref, mb_ref), _causal_mask_t(tq, 2 * tq, tq))

    lam = (jnp.exp(jnp.sum(lq1_ref[...] * lk1_ref[...], axis=1, keepdims=True))
           - jnp.exp(jnp.sum(lq2_ref[...] * lk2_ref[...], axis=1, keepdims=True)) + lam_init)
    for g in range(heads):
        ot = acc_ref[g] * (1.0 / l_ref[g])
        dd = (ot[:, :tq] - lam * ot[:, tq:]).T
        ms = jnp.mean(dd * dd, axis=-1, keepdims=True)
        o_ref[0, :, g * LANES:(g + 1) * LANES] = (
            dd * lax.rsqrt(ms + EPS) * sub_ref[...] * (1.0 - lam_init)).astype(BF16)


def _da_attn(qt, k, vt, lq1, lk1, lq2, lk2, subln, lam_init):
    b, s, d = k.shape
    heads = d // LANES
    wb = heads * LANES
    tq = vt.shape[3]
    vec = lambda bi, hi, qi: (0, 0)
    return pl.pallas_call(
        functools.partial(_da_attn_kernel, tq=tq, heads=heads, lam_init=lam_init),
        grid=(b, d // wb, s // tq),
        in_specs=[pl.BlockSpec((1, DA_HEAD_DIM), vec)] * 4 + [
            pl.BlockSpec((1, LANES), vec),
            pl.BlockSpec((1, wb, tq), lambda bi, hi, qi: (bi, hi, qi)),
            pl.BlockSpec((1, s, wb), lambda bi, hi, qi: (bi, 0, hi)),
            pl.BlockSpec((1, s // tq, wb, tq), lambda bi, hi, qi: (bi, 0, hi, 0))],
        out_specs=pl.BlockSpec((1, tq, wb), lambda bi, hi, qi: (bi, qi, hi)),
        out_shape=jax.ShapeDtypeStruct((b, s, d), BF16),
        scratch_shapes=[pltpu.VMEM((heads, LANES, 2 * tq), BF16),
                        pltpu.VMEM((heads, 1, 2 * tq), F32),
                        pltpu.VMEM((heads, 1, 2 * tq), F32),
                        pltpu.VMEM((heads, LANES, 2 * tq), F32),
                        pltpu.VMEM((heads, tq, 2 * tq), F32),
                        pltpu.VMEM((heads, tq, 2 * tq), F32),
                        pltpu.VMEM((heads, 1, 2 * tq), F32),
                        pltpu.VMEM((heads, 1, 2 * tq), F32)],
        compiler_params=_cparams(("arbitrary", "arbitrary", "arbitrary")),
    )(lq1.reshape(1, -1), lk1.reshape(1, -1), lq2.reshape(1, -1), lk2.reshape(1, -1),
      subln.reshape(1, -1), qt, k, vt)


def _mla_attn_kernel(qt_ref, k_ref, vt_ref, o_ref, m_ref, l_ref, acc_ref, sa_ref, sb_ref, ma_ref, mb_ref,
                     *, tq, heads):
    qi = pl.program_id(2)
    _init_stats(m_ref, l_ref, acc_ref)

    def scores(kb):
        off = pl.multiple_of(kb * tq, tq)
        return [jnp.dot(k_ref[0, pl.ds(off, tq), g * LANES:(g + 1) * LANES],
                        qt_ref[0, g * LANES:(g + 1) * LANES, :],
                        preferred_element_type=F32) for g in range(heads)]

    def softmax(s, kb, mask):
        for g in range(heads):
            _softmax_block(s, vt_ref[0, kb, g * MLA_V:(g + 1) * MLA_V, :], g, m_ref, l_ref, acc_ref, mask)

    _causal_sweep(qi, scores, softmax, (sa_ref, ma_ref), (sb_ref, mb_ref), _causal_mask_t(tq, tq, tq))
    for p in range(heads // 2):
        ot = jnp.concatenate([acc_ref[2 * p] * (1.0 / l_ref[2 * p]),
                              acc_ref[2 * p + 1] * (1.0 / l_ref[2 * p + 1])], axis=0)
        o_ref[0, :, p * LANES:(p + 1) * LANES] = ot.T.astype(BF16)


def _mla_attn(qt, k, vt):
    b, s, dk = k.shape
    heads = MLA_HEADS
    tq = vt.shape[3]
    return pl.pallas_call(
        functools.partial(_mla_attn_kernel, tq=tq, heads=heads),
        grid=(b, dk // (heads * LANES), s // tq),
        in_specs=[pl.BlockSpec((1, heads * LANES, tq), lambda bi, hi, qi: (bi, hi, qi)),
                  pl.BlockSpec((1, s, heads * LANES), lambda bi, hi, qi: (bi, 0, hi)),
                  pl.BlockSpec((1, s // tq, heads * MLA_V, tq), lambda bi, hi, qi: (bi, 0, hi, 0))],
        out_specs=pl.BlockSpec((1, tq, heads * MLA_V), lambda bi, hi, qi: (bi, qi, hi)),
        out_shape=jax.ShapeDtypeStruct((b, s, dk // LANES * MLA_V), BF16),
        scratch_shapes=[pltpu.VMEM((heads, 1, tq), F32),
                        pltpu.VMEM((heads, 1, tq), F32),
                        pltpu.VMEM((heads, MLA_V, tq), F32),
                        pltpu.VMEM((heads, tq, tq), F32),
                        pltpu.VMEM((heads, tq, tq), F32),
                        pltpu.VMEM((heads, 1, tq), F32),
                        pltpu.VMEM((heads, 1, tq), F32)],
        compiler_params=_cparams(("arbitrary", "arbitrary", "arbitrary")),
    )(qt, k, vt)


def _mla_pre_kernel(x_ref, mod_ref, pos_ref, g_ref, wint_ref, qag_ref, kvag_ref, kpeg_ref,
                    wuqt_ref, wukvt_ref, qng_ref, qpg_ref, kng_ref, invf_ref,
                    q_out, k_out, v_out, *, scale):
    h = _adaln_h(x_ref[0], mod_ref, g_ref, 0, 1)
    t = h.shape[0]
    reps = t // LANES
    nh, nope, rope, vd = MLA_HEADS, MLA_NOPE, MLA_ROPE, MLA_V
    lanes = lambda ref: jnp.tile(ref[...], (1, reps))

    def rms(z, axis):
        return z * lax.rsqrt(jnp.mean(z * z, axis=axis, keepdims=True) + EPS)

    lat = jnp.dot(wint_ref[...], h.T.astype(BF16), preferred_element_type=F32)
    cqn = (rms(lat[:MLA_Q_RANK], 0) * lanes(qag_ref)).astype(BF16)
    ckvn = (rms(lat[MLA_Q_RANK:MLA_Q_RANK + MLA_KV_RANK], 0) * lanes(kvag_ref)).astype(BF16)
    kpen = rms(lat[MLA_Q_RANK + MLA_KV_RANK:], 0) * lanes(kpeg_ref)
    q3 = jnp.dot(wuqt_ref[...], cqn, preferred_element_type=F32).reshape(nh, nope + rope, t)
    kv3 = jnp.dot(wukvt_ref[...], ckvn, preferred_element_type=F32).reshape(nh, nope + vd, t)

    vt = kv3[:, nope:].reshape(nh * vd, t)
    tk = v_out.shape[3]
    for j in range(v_out.shape[1]):
        v_out[0, j] = vt[:, j * tk:(j + 1) * tk].astype(BF16)

    half = rope // 2
    ang = lanes(invf_ref) * pos_ref[0]
    cos = jnp.cos(ang)[None]
    sin = jnp.sin(ang)[None]

    def rotary(z):
        z1 = z[:, :half]
        z2 = z[:, half:]
        return jnp.concatenate([z1 * cos - z2 * sin, z2 * cos + z1 * sin], axis=1)

    pad = jnp.zeros((nh, LANES - nope - rope, t), F32)
    qn = rms(q3[:, :nope], 1) * lanes(qng_ref)[None]
    qp = rotary(rms(q3[:, nope:], 1) * lanes(qpg_ref)[None])
    q_out[0] = (jnp.concatenate([qn, qp, pad], axis=1).reshape(nh * LANES, t) * scale).astype(BF16)
    kn = rms(kv3[:, :nope], 1) * lanes(kng_ref)[None]
    kp = jnp.broadcast_to(rotary(kpen[None]), (nh, rope, t))
    k_out[0] = jnp.concatenate([kn, kp, pad], axis=1).reshape(nh * LANES, t).T.astype(BF16)


def _mla_pre(x, mod, posr, norm_g, w_in, q_a_norm, kv_a_norm, w_uq, w_ukv,
             q_nope_norm, q_pe_norm, k_nope_norm, k_pe_norm):
    b, s, d = x.shape
    t = min(TOK_TILE, s)
    nh, nope, rope, vd = MLA_HEADS, MLA_NOPE, MLA_ROPE, MLA_V
    w = nh * LANES
    invf = (ROPE_THETA ** (-np.arange(0, rope, 2, dtype=np.float32) / rope)).astype(np.float32)
    lanes = lambda v: jnp.broadcast_to(jnp.asarray(v, F32)[:, None], (v.shape[0], LANES))
    row = lambda bi, ti: (0, 0)
    tok = lambda bi, ti: (bi, ti, 0)
    full = lambda a: pl.BlockSpec(a.shape, row)
    consts = [norm_g.reshape(1, d), w_in.T.astype(BF16), lanes(q_a_norm), lanes(kv_a_norm), lanes(k_pe_norm),
              w_uq.T.astype(BF16), w_ukv.T.astype(BF16), lanes(q_nope_norm), lanes(q_pe_norm),
              lanes(k_nope_norm), lanes(invf)]
    tk = min(ATT_TILE, t)
    return pl.pallas_call(
        functools.partial(_mla_pre_kernel, scale=(nope + rope) ** -0.5 * LOG2E),
        grid=(b, s // t),
        in_specs=[pl.BlockSpec((1, t, d), tok),
                  pl.BlockSpec((1, 6, d), lambda bi, ti: (bi, 0, 0)),
                  pl.BlockSpec((1, 1, t), lambda bi, ti: (bi, 0, ti))] + [full(a) for a in consts],
        out_specs=[pl.BlockSpec((1, w, t), lambda bi, ti: (bi, 0, ti)),
                   pl.BlockSpec((1, t, w), tok),
                   pl.BlockSpec((1, t // tk, nh * vd, tk), lambda bi, ti: (bi, ti, 0, 0))],
        out_shape=[jax.ShapeDtypeStruct((b, w, s), BF16),
                   jax.ShapeDtypeStruct((b, s, w), BF16),
                   jax.ShapeDtypeStruct((b, s // tk, nh * vd, tk), BF16)],
        compiler_params=_cparams(("arbitrary", "arbitrary")),
    )(x, mod, posr, *consts)


def _post_kernel(o_ref, wo_ref, x_ref, mod_ref, g_ref, wr_ref, bsel_ref,
                 x_out, h_out, route_out, cnt_out, run_ref, *, d):
    first = (pl.program_id(0) == 0) & (pl.program_id(1) == 0)

    @pl.when(first)
    def _():
        run_ref[...] = jnp.zeros_like(run_ref)

    y = jnp.dot(o_ref[0], wo_ref[...], preferred_element_type=F32)
    x = x_ref[0] + mod_ref[0, 2:3, :] * y
    x_out[0] = x
    h = _adaln_h(x, mod_ref, g_ref, 3, 4)
    t = h.shape[0]

    hh = h.astype(BF16)
    hl = (h - hh.astype(F32)).astype(BF16)
    logits = (jnp.dot(hh, wr_ref[0], preferred_element_type=F32)
              + jnp.dot(hl, wr_ref[0], preferred_element_type=F32)
              + jnp.dot(hh, wr_ref[1], preferred_element_type=F32))

    ng, ne = N_GROUPS, EXPERTS_PER_GROUP

    def route(lt):
        n = lt.shape[1]
        sub = lax.broadcasted_iota(jnp.int32, (SUBLANES, n), 0).astype(F32)

        def first_argmax(val):
            mx = jnp.max(val, axis=0, keepdims=True)
            return jnp.min(jnp.where(val == mx, sub, float(SUBLANES)), axis=0, keepdims=True)

        def pick(val, idx):
            return jnp.sum(jnp.where(sub == idx, val, 0.0), axis=0, keepdims=True)

        gmask = sub < ng
        gl = jnp.where(gmask, lt[:SUBLANES], NEG_INF)
        ge = jnp.exp(gl - jnp.max(gl, axis=0, keepdims=True))
        gprob = ge / jnp.sum(ge, axis=0, keepdims=True)
        gidx = first_argmax(jnp.where(gmask, gprob + bsel_ref[0], NEG_INF))
        ggate = pick(gprob, gidx)

        el = jnp.zeros((SUBLANES, n), F32)
        eb = jnp.zeros((SUBLANES, n), F32)
        for g in range(ng):
            chosen = gidx == float(g)
            el = jnp.where(chosen, lt[SUBLANES * (g + 1):SUBLANES * (g + 2)], el)
            eb = jnp.where(chosen, bsel_ref[g + 1], eb)
        ee = jnp.exp(el - jnp.max(el, axis=0, keepdims=True))
        eprob = ee / jnp.sum(ee, axis=0, keepdims=True)
        sel = eprob + eb
        e1 = first_argmax(sel)
        e2 = first_argmax(jnp.where(sub == e1, NEG_INF, sel))
        p1 = pick(eprob, e1)
        p2 = pick(eprob, e2)
        psum = p1 + p2
        w1 = p1 / psum * ggate
        w2 = p2 / psum * ggate
        swap = e2 < e1
        lo = jnp.where(swap, e2, e1)
        hi = jnp.where(swap, e1, e2)
        combo = gidx * float(N_PAIRS) + lo * (2.0 * ne - 1.0 - lo) * 0.5 + (hi - lo - 1.0)
        return jnp.where(swap, w2, w1), jnp.where(swap, w1, w2), combo

    lt = logits.T
    parts = [route(lt[:, j * (t // 2):(j + 1) * (t // 2)]) for j in range(2)]
    wa, wb, combo = [jnp.concatenate([p[i] for p in parts], axis=1) for i in range(3)]
    sub = lax.broadcasted_iota(jnp.int32, (SUBLANES, t), 0).astype(F32)

    row = lax.broadcasted_iota(jnp.int32, (LANES, t), 0).astype(F32)
    wrow = jnp.where(row == 0.0, wa, jnp.where(row == 1.0, wb, 0.0)).T
    _pack_token_tiles(h_out, h, wrow)

    onehot = row == combo
    oh = jnp.where(onehot, 1.0, 0.0)
    r_i = lax.broadcasted_iota(jnp.int32, (t, t), 0)
    c_i = lax.broadcasted_iota(jnp.int32, (t, t), 1)
    earlier = jnp.where(r_i < c_i, 1.0, 0.0).astype(BF16)
    before = jnp.dot(oh.astype(BF16), earlier, preferred_element_type=F32) + run_ref[...]
    rank = jnp.sum(jnp.where(onehot, before, 0.0), axis=0, keepdims=True)
    run = run_ref[...] + jnp.sum(oh, axis=1, keepdims=True)
    run_ref[...] = run
    cnt_out[...] = run
    route_out[0] = jnp.where(sub == 0.0, combo, jnp.where(sub == 1.0, rank, 0.0))


def _post(o, w_o, x, mod, norm_g, w_group, b_group, w_router, b_router):
    b, s, d = x.shape
    t = min(POST_TILE, s)
    do = o.shape[2]
    ng, ne = N_GROUPS, EXPERTS_PER_GROUP
    wr = jnp.concatenate([jnp.pad(w_group, ((0, 0), (0, SUBLANES - ng))),
                          jnp.pad(w_router, ((0, 0), (0, LANES - SUBLANES - ng * ne)))], axis=1)
    wr_hi = wr.astype(BF16)
    wr_lo = (wr - wr_hi.astype(F32)).astype(BF16)
    wr2 = jnp.stack([wr_hi, wr_lo])
    bsel = jnp.concatenate([jnp.pad(b_group, (0, SUBLANES - ng))[None, :], b_router], axis=0)[:, :, None]
    row = lambda bi, ti: (0, 0)
    tok = lambda bi, ti: (bi, ti, 0)
    return pl.pallas_call(
        functools.partial(_post_kernel, d=d),
        grid=(b, s // t),
        in_specs=[pl.BlockSpec((1, t, do), tok),
                  pl.BlockSpec((do, d), row),
                  pl.BlockSpec((1, t, d), tok),
                  pl.BlockSpec((1, 6, d), lambda bi, ti: (bi, 0, 0)),
                  pl.BlockSpec((1, d), row),
                  pl.BlockSpec((2, d, LANES), lambda bi, ti: (0, 0, 0)),
                  pl.BlockSpec((ng + 1, ne, 1), lambda bi, ti: (0, 0, 0))],
        out_specs=[pl.BlockSpec((1, t, d), tok),
                   pl.BlockSpec((1, t * SUBLANES, LANES), tok),
                   pl.BlockSpec((1, SUBLANES, t), lambda bi, ti: (bi * (s // t) + ti, 0, 0)),
                   pl.BlockSpec((LANES, 1), row)],
        out_shape=[jax.ShapeDtypeStruct((b, s, d), F32),
                   jax.ShapeDtypeStruct((b, s * SUBLANES, LANES), jnp.uint32),
                   jax.ShapeDtypeStruct((b * s // t, SUBLANES, t), F32),
                   jax.ShapeDtypeStruct((LANES, 1), F32)],
        scratch_shapes=[pltpu.VMEM((LANES, 1), F32)],
        compiler_params=_cparams(("arbitrary", "arbitrary")),
    )(o, w_o.astype(BF16), x, mod, norm_g.reshape(1, d), wr2, bsel)


IN_ROWS = SUBLANES
OUT_ROWS = SUBLANES // 2


def _tok_rows(s, n, per):
    return pl.ds(s, n, stride=per)


def _tok_span(tok, per):
    return pl.ds(pl.multiple_of(tok * per, per), per)


def _pack_words(x):
    half = x.shape[1] // 2
    bits = pltpu.bitcast(x.astype(BF16).astype(F32), jnp.uint32)
    return bits[:, :half] | (bits[:, half:] >> 16)


def _unpack_words(w):
    return pltpu.bitcast(w & jnp.uint32(0xFFFF0000), F32), pltpu.bitcast(w << 16, F32)


def _pack_token_tiles(p_out, h, wrow):
    t, d = h.shape
    word = _pack_words(h)
    nw = d // 2 // LANES
    for s in range(nw):
        p_out[0, _tok_rows(s, t, IN_ROWS), :] = word[:, s * LANES:(s + 1) * LANES]
    p_out[0, _tok_rows(nw, t, IN_ROWS), :] = pltpu.bitcast(wrow, jnp.uint32)
    for s in range(nw + 1, IN_ROWS):
        p_out[0, _tok_rows(s, t, IN_ROWS), :] = jnp.zeros((t, LANES), jnp.uint32)


def _unpack_token_tiles(hs_ref, d):
    nw = d // 2 // LANES
    tm = hs_ref.shape[0] // IN_ROWS
    pairs = [_unpack_words(hs_ref[_tok_rows(s, tm, IN_ROWS), :]) for s in range(nw)]
    x = jnp.concatenate([p[0].astype(BF16) for p in pairs] + [p[1].astype(BF16) for p in pairs], axis=1)
    wrow = pltpu.bitcast(hs_ref[_tok_rows(nw, tm, IN_ROWS), :], F32)
    return x, wrow[:, 0:1], wrow[:, 1:2]


def _row_copy(src_ref, src_row, dst_ref, dst_row, sem, per):
    return pltpu.make_async_copy(src_ref.at[_tok_span(src_row, per), :], dst_ref.at[_tok_span(dst_row, per), :], sem)


def _start_rows(rows, make_copy):
    def start(i, c):
        for u in range(ROW_UNROLL):
            make_copy(i * ROW_UNROLL + u).start(priority=u % 2)
        return c

    lax.fori_loop(0, rows // ROW_UNROLL, start, 0)


def _wait_rows(rows, make_copy):
    def wait(i, c):
        for u in range(ROW_UNROLL):
            make_copy(0).wait()
        return c

    lax.fori_loop(0, rows // ROW_UNROLL, wait, 0)


def _dispatch_kernel(dest_ref, h_ref, init_ref, hs_ref, sem, *, rows):
    del init_ref
    copy = lambda r: _row_copy(h_ref, r, hs_ref, dest_ref[r], sem, IN_ROWS)
    _start_rows(rows, copy)
    _wait_rows(rows, copy)


def _dispatch(dest, h2, n_rows):
    n = h2.shape[0] // IN_ROWS
    rows = min(ROW_TILE, n)
    return pl.pallas_call(
        functools.partial(_dispatch_kernel, rows=rows),
        grid=(n // rows,),
        in_specs=[pl.BlockSpec((rows,), lambda i: (i,), memory_space=pltpu.SMEM),
                  pl.BlockSpec((rows * IN_ROWS, LANES), lambda i: (i, 0)),
                  pl.BlockSpec(memory_space=pl.ANY)],
        out_specs=pl.BlockSpec(memory_space=pl.ANY),
        out_shape=jax.ShapeDtypeStruct((n_rows * IN_ROWS, LANES), jnp.uint32),
        scratch_shapes=[pltpu.SemaphoreType.DMA(())],
        input_output_aliases={2: 0},
        compiler_params=_cparams(("arbitrary",)),
    )(dest, h2, jnp.zeros((n_rows * IN_ROWS, LANES), jnp.uint32))


def _moe_kernel(elo_ref, ehi_ref, blk_ref, nact_ref, hs_ref, w1a_ref, w3a_ref, w2a_ref,
                w1b_ref, w3b_ref, w2b_ref, y_ref, *, d):
    del elo_ref, ehi_ref, blk_ref
    active = pl.program_id(0) < nact_ref[0]

    @pl.when(jnp.logical_not(active))
    def _():
        y_ref[...] = jnp.zeros_like(y_ref)

    @pl.when(active)
    def _():
        x, wa, wb = _unpack_token_tiles(hs_ref, d)

        def expert(w1_ref, w3_ref, wgt):
            a = jnp.dot(x, w1_ref[0], preferred_element_type=F32)
            g = jnp.dot(x, w3_ref[0], preferred_element_type=F32)
            return (a * jax.nn.sigmoid(a) * g * wgt).astype(BF16)

        y = (jnp.dot(expert(w1a_ref, w3a_ref, wa), w2a_ref[0], preferred_element_type=F32)
             + jnp.dot(expert(w1b_ref, w3b_ref, wb), w2b_ref[0], preferred_element_type=F32))
        word = _pack_words(y)
        for s in range(OUT_ROWS):
            y_ref[_tok_rows(s, y.shape[0], OUT_ROWS), :] = word[:, s * LANES:(s + 1) * LANES]


def _moe(hs, e_lo, e_hi, blk, nact, w1, w3, w2):
    n_rows = hs.shape[0] // IN_ROWS
    d = w1.shape[1]
    ff = w1.shape[2]
    tm = MOE_TILE
    n_tiles = n_rows // tm
    assert d == 2 * OUT_ROWS * LANES
    wspec = lambda shape, which: pl.BlockSpec(shape, (lambda j, lo, hi, bk, na: (lo[j], 0, 0)) if which == 0
                                              else (lambda j, lo, hi, bk, na: (hi[j], 0, 0)))
    grid_spec = pltpu.PrefetchScalarGridSpec(
        num_scalar_prefetch=4,
        grid=(n_tiles,),
        in_specs=[pl.BlockSpec((tm * IN_ROWS, LANES), lambda j, lo, hi, bk, na: (bk[j], 0)),
                  wspec((1, d, ff), 0), wspec((1, d, ff), 0), wspec((1, ff, d), 0),
                  wspec((1, d, ff), 1), wspec((1, d, ff), 1), wspec((1, ff, d), 1)],
        out_specs=pl.BlockSpec((tm * OUT_ROWS, LANES), lambda j, lo, hi, bk, na: (j, 0)),
    )
    return pl.pallas_call(
        functools.partial(_moe_kernel, d=d),
        grid_spec=grid_spec,
        out_shape=jax.ShapeDtypeStruct((n_rows * OUT_ROWS, LANES), jnp.uint32),
        compiler_params=_cparams(("arbitrary",)),
    )(e_lo, e_hi, blk, nact, hs, w1, w3, w2, w1, w3, w2)


def _combine_kernel(dest_ref, next_ref, x_ref, gate_ref, y_ref, o_ref, buf_ref, sem, *, rows):
    i = pl.program_id(0)
    slot = i & 1

    def gather(d_ref, sl):
        return lambda r: _row_copy(y_ref, d_ref[r], buf_ref.at[sl], r, sem.at[sl], OUT_ROWS)

    @pl.when(i == 0)
    def _():
        _start_rows(rows, gather(dest_ref, 0))

    @pl.when(i + 1 < pl.num_programs(0))
    def _():
        _start_rows(rows, gather(next_ref, 1 - slot))

    _wait_rows(rows, gather(dest_ref, slot))
    half = OUT_ROWS * LANES
    for s in range(OUT_ROWS):
        hi, lo = _unpack_words(buf_ref[slot, _tok_rows(s, rows, OUT_ROWS), :])
        for off, val in ((s * LANES, hi), (half + s * LANES, lo)):
            cols = slice(off, off + LANES)
            o_ref[:, cols] = x_ref[:, cols] + gate_ref[0, :, cols] * val


def _combine(dest, x2, gate, y, seq):
    n, d = x2.shape
    rows = min(ROW_TILE, seq)
    per_seq = seq // rows
    steps = n // rows
    return pl.pallas_call(
        functools.partial(_combine_kernel, rows=rows),
        grid=(steps,),
        in_specs=[pl.BlockSpec((rows,), lambda i: (i,), memory_space=pltpu.SMEM),
                  pl.BlockSpec((rows,), lambda i: (jnp.minimum(i + 1, steps - 1),), memory_space=pltpu.SMEM),
                  pl.BlockSpec((rows, d), lambda i: (i, 0)),
                  pl.BlockSpec((1, 1, d), lambda i: (i // per_seq, 0, 0)),
                  pl.BlockSpec(memory_space=pl.ANY)],
        out_specs=pl.BlockSpec((rows, d), lambda i: (i, 0)),
        out_shape=jax.ShapeDtypeStruct((n, d), F32),
        scratch_shapes=[pltpu.VMEM((2, rows * OUT_ROWS, LANES), jnp.uint32), pltpu.SemaphoreType.DMA((2,))],
        compiler_params=_cparams(("arbitrary",)),
    )(dest, dest, x2, gate, y)


def _moe_layer(x_new, h2, route, counts, gate_f, w1, w3, w2):
    b, s, d = x_new.shape
    n = b * s
    tm = MOE_TILE
    n_tiles = n // tm + N_COMBOS
    combo = route[:, 0, :].reshape(n).astype(jnp.int32)
    rank = route[:, 1, :].reshape(n).astype(jnp.int32)
    cnt = counts[:N_COMBOS, 0].astype(jnp.int32)
    tiles_per = (cnt + tm - 1) // tm
    tile_end = jnp.cumsum(tiles_per)
    row_off = (tile_end - tiles_per) * tm
    ids = jnp.arange(N_COMBOS, dtype=jnp.int32)

    def lookup(table, idx):
        return jnp.sum(jnp.where(idx[:, None] == ids[None, :], table[None, :], 0), axis=1)

    dest = lookup(row_off, combo) + rank
    nact = tile_end[-1:]
    blk = jnp.minimum(jnp.arange(n_tiles, dtype=jnp.int32), nact[0] - 1)
    tile_combo = jnp.sum((tile_end[None, :] <= blk[:, None]).astype(jnp.int32), axis=1)
    e_lo = lookup(jnp.asarray(_COMBO_LO), tile_combo)
    e_hi = lookup(jnp.asarray(_COMBO_HI), tile_combo)

    hs = _dispatch(dest, h2.reshape(n * SUBLANES, LANES), n_tiles * tm)
    ne = w1.shape[0] * w1.shape[1]
    y = _moe(hs, e_lo, e_hi, blk, nact.astype(jnp.int32),
             w1.reshape(ne, d, -1).astype(BF16), w3.reshape(ne, d, -1).astype(BF16),
             w2.reshape(ne, -1, d).astype(BF16))
    return _combine(dest, x_new.reshape(n, d), gate_f, y, s).reshape(b, s, d)


def kernel(x, c, positions, ada_w, ada_b, norm_mix, norm_ffn, da_w_qkv, da_q_norm, da_k_norm, da_lambda_q1, da_lambda_k1, da_lambda_q2, da_lambda_k2, da_subln, da_w_o, mla_w_in, mla_q_a_norm, mla_kv_a_norm, mla_w_uq, mla_w_ukv, mla_q_nope_norm, mla_q_pe_norm, mla_k_nope_norm, mla_k_pe_norm, mla_w_o, moe_w_group, moe_b_group, moe_w_router, moe_b_router, moe_w1, moe_w3, moe_w2):
    b, s, d = x.shape
    depth = ada_w.shape[0]
    mod_all = _modulation(c, ada_w, ada_b).reshape(depth, b, 6, d)
    posr = positions.astype(F32).reshape(b, 1, s)
    for i in range(depth):
        mod = mod_all[i]
        j = i // 2
        if i % 2 == 0:
            lam_init = 0.8 - 0.6 * math.exp(-0.3 * i)
            q, k, v = _da_pre(x, mod, posr, norm_mix[i], da_w_qkv[j], da_q_norm[j], da_k_norm[j])
            o = _da_attn(q, k, v, da_lambda_q1[j], da_lambda_k1[j], da_lambda_q2[j], da_lambda_k2[j],
                         da_subln[j], lam_init)
            w_o = da_w_o[j]
        else:
            q, k, v = _mla_pre(x, mod, posr, norm_mix[i], mla_w_in[j], mla_q_a_norm[j], mla_kv_a_norm[j],
                               mla_w_uq[j], mla_w_ukv[j], mla_q_nope_norm[j], mla_q_pe_norm[j],
                               mla_k_nope_norm[j], mla_k_pe_norm[j])
            o = _mla_attn(q, k, v)
            w_o = mla_w_o[j]
        x_new, h2, route, counts = _post(o, w_o, x, mod, norm_ffn[i], moe_w_group[i], moe_b_group[i],
                                         moe_w_router[i], moe_b_router[i])
        x = _moe_layer(x_new, h2, route, counts, mod[:, 5:6, :], moe_w1[i], moe_w3[i], moe_w2[i])
    return x
```

```python
import functools
import math

import numpy as np
import jax
import jax.numpy as jnp
from jax import lax
from jax.experimental import pallas as pl
from jax.experimental.pallas import tpu as pltpu

F32 = jnp.float32
BF16 = jnp.bfloat16

ROPE_THETA = 10000.0
EPS = 1e-6
NEG_INF = -1e30
LOG2E = math.log2(math.e)
DA_HEAD_DIM = 64
MLA_HEADS = 16
MLA_NOPE = 64
MLA_ROPE = 32
MLA_V = 64
MLA_Q_RANK = 384
MLA_KV_RANK = 256
N_GROUPS = 4
EXPERTS_PER_GROUP = 8

LANES = 128
VMEM_LIMIT = 56 * 1024 * 1024
TOK_TILE = 512
POST_TILE = 1024
ATT_TILE = 256
MOE_TILE = 256
ROW_TILE = 1024
ROW_UNROLL = 8
SUBLANES = 8

N_PAIRS = EXPERTS_PER_GROUP * (EXPERTS_PER_GROUP - 1) // 2
N_COMBOS = N_GROUPS * N_PAIRS


def _combo_tables():
    lo_t, hi_t = [], []
    for g in range(N_GROUPS):
        for lo in range(EXPERTS_PER_GROUP):
            for hi in range(lo + 1, EXPERTS_PER_GROUP):
                lo_t.append(g * EXPERTS_PER_GROUP + lo)
                hi_t.append(g * EXPERTS_PER_GROUP + hi)
    return np.asarray(lo_t, np.int32), np.asarray(hi_t, np.int32)


_COMBO_LO, _COMBO_HI = _combo_tables()


def _cparams(sem):
    return pltpu.CompilerParams(dimension_semantics=sem, vmem_limit_bytes=VMEM_LIMIT)


def _adaln_h(x, mod_ref, g_ref, shift_row, scale_row):
    ms = jnp.mean(x * x, axis=-1, keepdims=True)
    h = x * lax.rsqrt(ms + EPS) * g_ref[...]
    return h * (1.0 + mod_ref[0, scale_row:scale_row + 1, :]) + mod_ref[0, shift_row:shift_row + 1, :]


def _mod_kernel(c_ref, w_ref, b_ref, o_ref):
    c = c_ref[...]
    cond = (c * jax.nn.sigmoid(c)).astype(BF16)
    o_ref[0] = jnp.dot(cond, w_ref[0].astype(BF16), preferred_element_type=F32) + b_ref[0]


def _modulation(c, ada_w, ada_b):
    depth, d, n6 = ada_w.shape
    b = c.shape[0]
    tn = 1536
    return pl.pallas_call(
        _mod_kernel,
        grid=(depth, n6 // tn),
        in_specs=[pl.BlockSpec((b, d), lambda i, j: (0, 0)),
                  pl.BlockSpec((1, d, tn), lambda i, j: (i, 0, j)),
                  pl.BlockSpec((1, 1, tn), lambda i, j: (i, 0, j))],
        out_specs=pl.BlockSpec((1, b, tn), lambda i, j: (i, 0, j)),
        out_shape=jax.ShapeDtypeStruct((depth, b, n6), F32),
        compiler_params=_cparams(("arbitrary", "arbitrary")),
    )(c, ada_w, ada_b.reshape(depth, 1, n6))


def _da_pre_kernel(x_ref, mod_ref, pos_ref, g_ref, wt_ref, qg_ref, kg_ref, invf_ref,
                   q_out, k_out, v_out, *, d, scale):
    h = _adaln_h(x_ref[0], mod_ref, g_ref, 0, 1)
    t = h.shape[0]
    qkv = jnp.dot(wt_ref[...], h.T.astype(BF16), preferred_element_type=F32)
    vt = qkv[2 * d:]
    tk = v_out.shape[3]
    for j in range(v_out.shape[1]):
        v_out[0, j] = vt[:, j * tk:(j + 1) * tk].astype(BF16)

    dh = DA_HEAD_DIM
    half = dh // 2
    reps = t // LANES
    ang = jnp.tile(invf_ref[...], (1, reps)) * pos_ref[0]
    cos = jnp.cos(ang)[None, None]
    sin = jnp.sin(ang)[None, None]

    def norm_rope(z, gain_ref):
        z3 = z.reshape(d // dh, dh, t)
        r = lax.rsqrt(jnp.mean(z3 * z3, axis=1, keepdims=True) + EPS)
        z4 = (z3 * r).reshape(d // dh, 2, half, t) * jnp.tile(gain_ref[...], (1, reps)).reshape(1, 2, half, t)
        z1 = z4[:, 0:1]
        z2 = z4[:, 1:2]
        return jnp.concatenate([z1 * cos - z2 * sin, z2 * cos + z1 * sin], axis=1).reshape(d, t)

    q_out[0] = (norm_rope(qkv[:d], qg_ref) * scale).astype(BF16)
    k_out[0] = norm_rope(qkv[d:2 * d], kg_ref).T.astype(BF16)


def _da_pre(x, mod, posr, norm_g, w_qkv, q_norm, k_norm):
    b, s, d = x.shape
    t = min(TOK_TILE, s)
    dh = DA_HEAD_DIM
    invf = (ROPE_THETA ** (-np.arange(0, dh, 2, dtype=np.float32) / dh)).astype(np.float32)
    lanes = lambda v: jnp.broadcast_to(jnp.asarray(v, F32)[:, None], (v.shape[0], LANES))
    row = lambda bi, ti: (0, 0)
    tok = lambda bi, ti: (bi, ti, 0)
    tk = min(ATT_TILE, t)
    return pl.pallas_call(
        functools.partial(_da_pre_kernel, d=d, scale=dh ** -0.5 * LOG2E),
        grid=(b, s // t),
        in_specs=[pl.BlockSpec((1, t, d), tok),
                  pl.BlockSpec((1, 6, d), lambda bi, ti: (bi, 0, 0)),
                  pl.BlockSpec((1, 1, t), lambda bi, ti: (bi, 0, ti)),
                  pl.BlockSpec((1, d), row),
                  pl.BlockSpec((3 * d, d), row),
                  pl.BlockSpec((dh, LANES), row),
                  pl.BlockSpec((dh, LANES), row),
                  pl.BlockSpec((dh // 2, LANES), row)],
        out_specs=[pl.BlockSpec((1, d, t), lambda bi, ti: (bi, 0, ti)),
                   pl.BlockSpec((1, t, d), tok),
                   pl.BlockSpec((1, t // tk, d, tk), lambda bi, ti: (bi, ti, 0, 0))],
        out_shape=[jax.ShapeDtypeStruct((b, d, s), BF16),
                   jax.ShapeDtypeStruct((b, s, d), BF16),
                   jax.ShapeDtypeStruct((b, s // tk, d, tk), BF16)],
        compiler_params=_cparams(("arbitrary", "arbitrary")),
    )(x, mod, posr, norm_g.reshape(1, d), w_qkv.T.astype(BF16), lanes(q_norm), lanes(k_norm), lanes(invf))


def _softmax_block(buf, vt_blk, g, m_ref, l_ref, acc_ref, mask):
    s = buf[0][g]
    if mask is None:
        bmax = buf[1][g]
    else:
        s = jnp.where(mask, s, NEG_INF)
        bmax = jnp.max(s, axis=0, keepdims=True)
    m_prev = m_ref[g]
    m_new = jnp.maximum(m_prev, bmax)
    alpha = jnp.exp2(m_prev - m_new)
    p = jnp.exp2(s - m_new)
    l_ref[g] = alpha * l_ref[g] + jnp.sum(p, axis=0, keepdims=True)
    acc_ref[g] = alpha * acc_ref[g] + jnp.dot(vt_blk, p.astype(BF16), preferred_element_type=F32)
    m_ref[g] = m_new


def _causal_sweep(qi, scores, softmax, sa_ref, sb_ref, diag_mask):
    def fill(buf, kb):
        for g, s in enumerate(scores(kb)):
            buf[0][g] = s
            buf[1][g] = jnp.max(s, axis=0, keepdims=True)

    fill(sa_ref, 0)

    def body(j, c):
        kb = 2 * j
        fill(sb_ref, kb + 1)
        softmax(sa_ref, kb, None)
        fill(sa_ref, kb + 2)
        softmax(sb_ref, kb + 1, None)
        return c

    lax.fori_loop(0, lax.shift_right_logical(qi, 1), body, 0)
    odd = (qi & 1) == 1

    @pl.when(jnp.logical_not(odd))
    def _():
        softmax(sa_ref, qi, diag_mask)

    @pl.when(odd)
    def _():
        fill(sb_ref, qi)
        softmax(sa_ref, qi - 1, None)
        softmax(sb_ref, qi, diag_mask)


def _init_stats(m_ref, l_ref, acc_ref):
    m_ref[...] = jnp.full(m_ref.shape, NEG_INF, F32)
    l_ref[...] = jnp.zeros(l_ref.shape, F32)
    acc_ref[...] = jnp.zeros(acc_ref.shape, F32)


def _causal_mask_t(tk, cols, tq):
    c = lax.broadcasted_iota(jnp.int32, (tk, cols), 0)
    r = lax.broadcasted_iota(jnp.int32, (tk, cols), 1)
    r = jnp.where(r >= tq, r - tq, r)
    return c <= r


def _da_attn_kernel(lq1_ref, lk1_ref, lq2_ref, lk2_ref, sub_ref, qt_ref, k_ref, vt_ref, o_ref,
                    qs_ref, m_ref, l_ref, acc_ref, sa_ref, sb_ref, ma_ref, mb_ref, *, tq, heads, lam_init):
    qi = pl.program_id(2)
    sub = lax.broadcasted_iota(jnp.int32, (LANES, tq), 0)
    for g in range(heads):
        qt = qt_ref[0, g * LANES:(g + 1) * LANES, :]
        zero = jnp.zeros_like(qt)
        qs_ref[g] = jnp.concatenate([jnp.where(sub < DA_HEAD_DIM, qt, zero),
                                     jnp.where(sub >= DA_HEAD_DIM, qt, zero)], axis=1)
    _init_stats(m_ref, l_ref, acc_ref)

    def scores(kb):
        off = pl.multiple_of(kb * tq, tq)
        return [jnp.dot(k_ref[0, pl.ds(off, tq), g * LANES:(g + 1) * LANES], qs_ref[g],
                        preferred_element_type=F32) for g in range(heads)]

    def softmax(s, kb, mask):
        for g in range(heads):
            _softmax_block(s, vt_ref[0, kb, g * LANES:(g + 1) * LANES, :], g, m_ref, l_ref, acc_ref, mask)

    _causal_sweep(qi, scores, softmax, (sa_ref, ma_ref), (sb_ref, mb_ref), _causal_mask_t(tq, 2 * tq, tq))

    lam = (jnp.exp(jnp.sum(lq1_ref[...] * lk1_ref[...], axis=1, keepdims=True))
           - jnp.exp(jnp.sum(lq2_ref[...] * lk2_ref[...], axis=1, keepdims=True)) + lam_init)
    for g in range(heads):
        ot = acc_ref[g] * (1.0 / l_ref[g])
        dd = (ot[:, :tq] - lam * ot[:, tq:]).T
        ms = jnp.mean(dd * dd, axis=-1, keepdims=True)
        o_ref[0, :, g * LANES:(g + 1) * LANES] = (
            dd * lax.rsqrt(ms + EPS) * sub_ref[...] * (1.0 - lam_init)).astype(BF16)


def _da_attn(qt, k, vt, lq1, lk1, lq2, lk2, subln, lam_init):
    b, s, d = k.shape
    heads = d // LANES
    wb = heads * LANES
    tq = vt.shape[3]
    vec = lambda bi, hi, qi: (0, 0)
    return pl.pallas_call(
        functools.partial(_da_attn_kernel, tq=tq, heads=heads, lam_init=lam_init),
        grid=(b, d // wb, s // tq),
        in_specs=[pl.BlockSpec((1, DA_HEAD_DIM), vec)] * 4 + [
            pl.BlockSpec((1, LANES), vec),
            pl.BlockSpec((1, wb, tq), lambda bi, hi, qi: (bi, hi, qi)),
            pl.BlockSpec((1, s, wb), lambda bi, hi, qi: (bi, 0, hi)),
            pl.BlockSpec((1, s // tq, wb, tq), lambda bi, hi, qi: (bi, 0, hi, 0))],
        out_specs=pl.BlockSpec((1, tq, wb), lambda bi, hi, qi: (bi, qi, hi)),
        out_shape=jax.ShapeDtypeStruct((b, s, d), BF16),
        scratch_shapes=[pltpu.VMEM((heads, LANES, 2 * tq), BF16),
                        pltpu.VMEM((heads, 1, 2 * tq), F32),
                        pltpu.VMEM((heads, 1, 2 * tq), F32),
                        pltpu.VMEM((heads, LANES, 2 * tq), F32),
                        pltpu.VMEM((heads, tq, 2 * tq), F32),
                        pltpu.VMEM((heads, tq, 2 * tq), F32),
                        pltpu.VMEM((heads, 1, 2 * tq), F32),
                        pltpu.VMEM((heads, 1, 2 * tq), F32)],
        compiler_params=_cparams(("arbitrary", "arbitrary", "arbitrary")),
    )(lq1.reshape(1, -1), lk1.reshape(1, -1), lq2.reshape(1, -1), lk2.reshape(1, -1),
      subln.reshape(1, -1), qt, k, vt)


def _mla_attn_kernel(qt_ref, k_ref, vt_ref, o_ref, m_ref, l_ref, acc_ref, sa_ref, sb_ref, ma_ref, mb_ref,
                     *, tq, heads):
    qi = pl.program_id(2)
    _init_stats(m_ref, l_ref, acc_ref)

    def scores(kb):
        off = pl.multiple_of(kb * tq, tq)
        return [jnp.dot(k_ref[0, pl.ds(off, tq), g * LANES:(g + 1) * LANES],
                        qt_ref[0, g * LANES:(g + 1) * LANES, :],
                        preferred_element_type=F32) for g in range(heads)]

    def softmax(s, kb, mask):
        for g in range(heads):
            _softmax_block(s, vt_ref[0, kb, g * MLA_V:(g + 1) * MLA_V, :], g, m_ref, l_ref, acc_ref, mask)

    _causal_sweep(qi, scores, softmax, (sa_ref, ma_ref), (sb_ref, mb_ref), _causal_mask_t(tq, tq, tq))
    for p in range(heads // 2):
        ot = jnp.concatenate([acc_ref[2 * p] * (1.0 / l_ref[2 * p]),
                              acc_ref[2 * p + 1] * (1.0 / l_ref[2 * p + 1])], axis=0)
        o_ref[0, :, p * LANES:(p + 1) * LANES] = ot.T.astype(BF16)


def _mla_attn(qt, k, vt):
    b, s, dk = k.shape
    heads = MLA_HEADS
    tq = vt.shape[3]
    return pl.pallas_call(
        functools.partial(_mla_attn_kernel, tq=tq, heads=heads),
        grid=(b, dk // (heads * LANES), s // tq),
        in_specs=[pl.BlockSpec((1, heads * LANES, tq), lambda bi, hi, qi: (bi, hi, qi)),
                  pl.BlockSpec((1, s, heads * LANES), lambda bi, hi, qi: (bi, 0, hi)),
                  pl.BlockSpec((1, s // tq, heads * MLA_V, tq), lambda bi, hi, qi: (bi, 0, hi, 0))],
        out_specs=pl.BlockSpec((1, tq, heads * MLA_V), lambda bi, hi, qi: (bi, qi, hi)),
        out_shape=jax.ShapeDtypeStruct((b, s, dk // LANES * MLA_V), BF16),
        scratch_shapes=[pltpu.VMEM((heads, 1, tq), F32),
                        pltpu.VMEM((heads, 1, tq), F32),
                        pltpu.VMEM((heads, MLA_V, tq), F32),
                        pltpu.VMEM((heads, tq, tq), F32),
                        pltpu.VMEM((heads, tq, tq), F32),
                        pltpu.VMEM((heads, 1, tq), F32),
                        pltpu.VMEM((heads, 1, tq), F32)],
        compiler_params=_cparams(("arbitrary", "arbitrary", "arbitrary")),
    )(qt, k, vt)


def _mla_pre_kernel(x_ref, mod_ref, pos_ref, g_ref, wint_ref, qag_ref, kvag_ref, kpeg_ref,
                    wuqt_ref, wukvt_ref, qng_ref, qpg_ref, kng_ref, invf_ref,
                    q_out, k_out, v_out, *, scale):
    h = _adaln_h(x_ref[0], mod_ref, g_ref, 0, 1)
    t = h.shape[0]
    reps = t // LANES
    nh, nope, rope, vd = MLA_HEADS, MLA_NOPE, MLA_ROPE, MLA_V
    lanes = lambda ref: jnp.tile(ref[...], (1, reps))

    def rms(z, axis):
        return z * lax.rsqrt(jnp.mean(z * z, axis=axis, keepdims=True) + EPS)

    lat = jnp.dot(wint_ref[...], h.T.astype(BF16), preferred_element_type=F32)
    cqn = (rms(lat[:MLA_Q_RANK], 0) * lanes(qag_ref)).astype(BF16)
    ckvn = (rms(lat[MLA_Q_RANK:MLA_Q_RANK + MLA_KV_RANK], 0) * lanes(kvag_ref)).astype(BF16)
    kpen = rms(lat[MLA_Q_RANK + MLA_KV_RANK:], 0) * lanes(kpeg_ref)
    q3 = jnp.dot(wuqt_ref[...], cqn, preferred_element_type=F32).reshape(nh, nope + rope, t)
    kv3 = jnp.dot(wukvt_ref[...], ckvn, preferred_element_type=F32).reshape(nh, nope + vd, t)

    vt = kv3[:, nope:].reshape(nh * vd, t)
    tk = v_out.shape[3]
    for j in range(v_out.shape[1]):
        v_out[0, j] = vt[:, j * tk:(j + 1) * tk].astype(BF16)

    half = rope // 2
    ang = lanes(invf_ref) * pos_ref[0]
    cos = jnp.cos(ang)[None]
    sin = jnp.sin(ang)[None]

    def rotary(z):
        z1 = z[:, :half]
        z2 = z[:, half:]
        return jnp.concatenate([z1 * cos - z2 * sin, z2 * cos + z1 * sin], axis=1)

    pad = jnp.zeros((nh, LANES - nope - rope, t), F32)
    qn = rms(q3[:, :nope], 1) * lanes(qng_ref)[None]
    qp = rotary(rms(q3[:, nope:], 1) * lanes(qpg_ref)[None])
    q_out[0] = (jnp.concatenate([qn, qp, pad], axis=1).reshape(nh * LANES, t) * scale).astype(BF16)
    kn = rms(kv3[:, :nope], 1) * lanes(kng_ref)[None]
    kp = jnp.broadcast_to(rotary(kpen[None]), (nh, rope, t))
    k_out[0] = jnp.concatenate([kn, kp, pad], axis=1).reshape(nh * LANES, t).T.astype(BF16)


def _mla_pre(x, mod, posr, norm_g, w_in, q_a_norm, kv_a_norm, w_uq, w_ukv,
             q_nope_norm, q_pe_norm, k_nope_norm, k_pe_norm):
    b, s, d = x.shape
    t = min(TOK_TILE, s)
    nh, nope, rope, vd = MLA_HEADS, MLA_NOPE, MLA_ROPE, MLA_V
    w = nh * LANES
    invf = (ROPE_THETA ** (-np.arange(0, rope, 2, dtype=np.float32) / rope)).astype(np.float32)
    lanes = lambda v: jnp.broadcast_to(jnp.asarray(v, F32)[:, None], (v.shape[0], LANES))
    row = lambda bi, ti: (0, 0)
    tok = lambda bi, ti: (bi, ti, 0)
    full = lambda a: pl.BlockSpec(a.shape, row)
    consts = [norm_g.reshape(1, d), w_in.T.astype(BF16), lanes(q_a_norm), lanes(kv_a_norm), lanes(k_pe_norm),
              w_uq.T.astype(BF16), w_ukv.T.astype(BF16), lanes(q_nope_norm), lanes(q_pe_norm),
              lanes(k_nope_norm), lanes(invf)]
    tk = min(ATT_TILE, t)
    return pl.pallas_call(
        functools.partial(_mla_pre_kernel, scale=(nope + rope) ** -0.5 * LOG2E),
        grid=(b, s // t),
        in_specs=[pl.BlockSpec((1, t, d), tok),
                  pl.BlockSpec((1, 6, d), lambda bi, ti: (bi, 0, 0)),
                  pl.BlockSpec((1, 1, t), lambda bi, ti: (bi, 0, ti))] + [full(a) for a in consts],
        out_specs=[pl.BlockSpec((1, w, t), lambda bi, ti: (bi, 0, ti)),
                   pl.BlockSpec((1, t, w), tok),
                   pl.BlockSpec((1, t // tk, nh * vd, tk), lambda bi, ti: (bi, ti, 0, 0))],
        out_shape=[jax.ShapeDtypeStruct((b, w, s), BF16),
                   jax.ShapeDtypeStruct((b, s, w), BF16),
                   jax.ShapeDtypeStruct((b, s // tk, nh * vd, tk), BF16)],
        compiler_params=_cparams(("arbitrary", "arbitrary")),
    )(x, mod, posr, *consts)


def _post_kernel(o_ref, wo_ref, x_ref, mod_ref, g_ref, wr_ref, bsel_ref,
                 x_out, h_out, route_out, cnt_out, run_ref, *, d):
    first = (pl.program_id(0) == 0) & (pl.program_id(1) == 0)

    @pl.when(first)
    def _():
        run_ref[...] = jnp.zeros_like(run_ref)

    y = jnp.dot(o_ref[0], wo_ref[...], preferred_element_type=F32)
    x = x_ref[0] + mod_ref[0, 2:3, :] * y
    x_out[0] = x
    h = _adaln_h(x, mod_ref, g_ref, 3, 4)
    t = h.shape[0]

    hh = h.astype(BF16)
    hl = (h - hh.astype(F32)).astype(BF16)
    logits = (jnp.dot(hh, wr_ref[0], preferred_element_type=F32)
              + jnp.dot(hl, wr_ref[0], preferred_element_type=F32)
              + jnp.dot(hh, wr_ref[1], preferred_element_type=F32))

    ng, ne = N_GROUPS, EXPERTS_PER_GROUP

    def route(lt):
        n = lt.shape[1]
        sub = lax.broadcasted_iota(jnp.int32, (SUBLANES, n), 0).astype(F32)

        def first_argmax(val):
            mx = jnp.max(val, axis=0, keepdims=True)
            return jnp.min(jnp.where(val == mx, sub, float(SUBLANES)), axis=0, keepdims=True)

        def pick(val, idx):
            return jnp.sum(jnp.where(sub == idx, val, 0.0), axis=0, keepdims=True)

        gmask = sub < ng
        gl = jnp.where(gmask, lt[:SUBLANES], NEG_INF)
        ge = jnp.exp(gl - jnp.max(gl, axis=0, keepdims=True))
        gprob = ge / jnp.sum(ge, axis=0, keepdims=True)
        gidx = first_argmax(jnp.where(gmask, gprob + bsel_ref[0], NEG_INF))
        ggate = pick(gprob, gidx)

        el = jnp.zeros((SUBLANES, n), F32)
        eb = jnp.zeros((SUBLANES, n), F32)
        for g in range(ng):
            chosen = gidx == float(g)
            el = jnp.where(chosen, lt[SUBLANES * (g + 1):SUBLANES * (g + 2)], el)
            eb = jnp.where(chosen, bsel_ref[g + 1], eb)
        ee = jnp.exp(el - jnp.max(el, axis=0, keepdims=True))
        eprob = ee / jnp.sum(ee, axis=0, keepdims=True)
        sel = eprob + eb
        e1 = first_argmax(sel)
        e2 = first_argmax(jnp.where(sub == e1, NEG_INF, sel))
        p1 = pick(eprob, e1)
        p2 = pick(eprob, e2)
        psum = p1 + p2
        w1 = p1 / psum * ggate
        w2 = p2 / psum * ggate
        swap = e2 < e1
        lo = jnp.where(swap, e2, e1)
        hi = jnp.where(swap, e1, e2)
        combo = gidx * float(N_PAIRS) + lo * (2.0 * ne - 1.0 - lo) * 0.5 + (hi - lo - 1.0)
        return jnp.where(swap, w2, w1), jnp.where(swap, w1, w2), combo

    lt = logits.T
    parts = [route(lt[:, j * (t // 2):(j + 1) * (t // 2)]) for j in range(2)]
    wa, wb, combo = [jnp.concatenate([p[i] for p in parts], axis=1) for i in range(3)]
    sub = lax.broadcasted_iota(jnp.int32, (SUBLANES, t), 0).astype(F32)

    row = lax.broadcasted_iota(jnp.int32, (LANES, t), 0).astype(F32)
    wrow = jnp.where(row == 0.0, wa, jnp.where(row == 1.0, wb, 0.0)).T
    _pack_token_tiles(h_out, h, wrow)

    onehot = row == combo
    oh = jnp.where(onehot, 1.0, 0.0)
    r_i = lax.broadcasted_iota(jnp.int32, (t, t), 0)
    c_i = lax.broadcasted_iota(jnp.int32, (t, t), 1)
    earlier = jnp.where(r_i < c_i, 1.0, 0.0).astype(BF16)
    before = jnp.dot(oh.astype(BF16), earlier, preferred_element_type=F32) + run_ref[...]
    rank = jnp.sum(jnp.where(onehot, before, 0.0), axis=0, keepdims=True)
    run = run_ref[...] + jnp.sum(oh, axis=1, keepdims=True)
    run_ref[...] = run
    cnt_out[...] = run
    route_out[0] = jnp.where(sub == 0.0, combo, jnp.where(sub == 1.0, rank, 0.0))


def _post(o, w_o, x, mod, norm_g, w_group, b_group, w_router, b_router):
    b, s, d = x.shape
    t = min(POST_TILE, s)
    do = o.shape[2]
    ng, ne = N_GROUPS, EXPERTS_PER_GROUP
    wr = jnp.concatenate([jnp.pad(w_group, ((0, 0), (0, SUBLANES - ng))),
                          jnp.pad(w_router, ((0, 0), (0, LANES - SUBLANES - ng * ne)))], axis=1)
    wr_hi = wr.astype(BF16)
    wr_lo = (wr - wr_hi.astype(F32)).astype(BF16)
    wr2 = jnp.stack([wr_hi, wr_lo])
    bsel = jnp.concatenate([jnp.pad(b_group, (0, SUBLANES - ng))[None, :], b_router], axis=0)[:, :, None]
    row = lambda bi, ti: (0, 0)
    tok = lambda bi, ti: (bi, ti, 0)
    return pl.pallas_call(
        functools.partial(_post_kernel, d=d),
        grid=(b, s // t),
        in_specs=[pl.BlockSpec((1, t, do), tok),
                  pl.BlockSpec((do, d), row),
                  pl.BlockSpec((1, t, d), tok),
                  pl.BlockSpec((1, 6, d), lambda bi, ti: (bi, 0, 0)),
                  pl.BlockSpec((1, d), row),
                  pl.BlockSpec((2, d, LANES), lambda bi, ti: (0, 0, 0)),
                  pl.BlockSpec((ng + 1, ne, 1), lambda bi, ti: (0, 0, 0))],
        out_specs=[pl.BlockSpec((1, t, d), tok),
                   pl.BlockSpec((1, t * SUBLANES, LANES), tok),
                   pl.BlockSpec((1, SUBLANES, t), lambda bi, ti: (bi * (s // t) + ti, 0, 0)),
                   pl.BlockSpec((LANES, 1), row)],
        out_shape=[jax.ShapeDtypeStruct((b, s, d), F32),
                   jax.ShapeDtypeStruct((b, s * SUBLANES, LANES), jnp.uint32),
                   jax.ShapeDtypeStruct((b * s // t, SUBLANES, t), F32),
                   jax.ShapeDtypeStruct((LANES, 1), F32)],
        scratch_shapes=[pltpu.VMEM((LANES, 1), F32)],
        compiler_params=_cparams(("arbitrary", "arbitrary")),
    )(o, w_o.astype(BF16), x, mod, norm_g.reshape(1, d), wr2, bsel)


IN_ROWS = SUBLANES
OUT_ROWS = SUBLANES // 2


def _tok_rows(s, n, per):
    return pl.ds(s, n, stride=per)


def _tok_span(tok, per):
    return pl.ds(pl.multiple_of(tok * per, per), per)


def _pack_words(x):
    half = x.shape[1] // 2
    bits = pltpu.bitcast(x.astype(BF16).astype(F32), jnp.uint32)
    return bits[:, :half] | (bits[:, half:] >> 16)


def _unpack_words(w):
    return pltpu.bitcast(w & jnp.uint32(0xFFFF0000), F32), pltpu.bitcast(w << 16, F32)


def _pack_token_tiles(p_out, h, wrow):
    t, d = h.shape
    word = _pack_words(h)
    nw = d // 2 // LANES
    for s in range(nw):
        p_out[0, _tok_rows(s, t, IN_ROWS), :] = word[:, s * LANES:(s + 1) * LANES]
    p_out[0, _tok_rows(nw, t, IN_ROWS), :] = pltpu.bitcast(wrow, jnp.uint32)
    for s in range(nw + 1, IN_ROWS):
        p_out[0, _tok_rows(s, t, IN_ROWS), :] = jnp.zeros((t, LANES), jnp.uint32)


def _unpack_token_tiles(hs_ref, d):
    nw = d // 2 // LANES
    tm = hs_ref.shape[0] // IN_ROWS
    pairs = [_unpack_words(hs_ref[_tok_rows(s, tm, IN_ROWS), :]) for s in range(nw)]
    x = jnp.concatenate([p[0].astype(BF16) for p in pairs] + [p[1].astype(BF16) for p in pairs], axis=1)
    wrow = pltpu.bitcast(hs_ref[_tok_rows(nw, tm, IN_ROWS), :], F32)
    return x, wrow[:, 0:1], wrow[:, 1:2]


def _row_copy(src_ref, src_row, dst_ref, dst_row, sem, per):
    return pltpu.make_async_copy(src_ref.at[_tok_span(src_row, per), :], dst_ref.at[_tok_span(dst_row, per), :], sem)


def _start_rows(rows, make_copy):
    def start(i, c):
        for u in range(ROW_UNROLL):
            make_copy(i * ROW_UNROLL + u).start(priority=u % 2)
        return c

    lax.fori_loop(0, rows // ROW_UNROLL, start, 0)


def _wait_rows(rows, make_copy):
    def wait(i, c):
        for u in range(ROW_UNROLL):
            make_copy(0).wait()
        return c

    lax.fori_loop(0, rows // ROW_UNROLL, wait, 0)


def _dispatch_kernel(dest_ref, h_ref, init_ref, hs_ref, sem, *, rows):
    del init_ref
    copy = lambda r: _row_copy(h_ref, r, hs_ref, dest_ref[r], sem, IN_ROWS)
    _start_rows(rows, copy)
    _wait_rows(rows, copy)


def _dispatch(dest, h2, n_rows):
    n = h2.shape[0] // IN_ROWS
    rows = min(ROW_TILE, n)
    return pl.pallas_call(
        functools.partial(_dispatch_kernel, rows=rows),
        grid=(n // rows,),
        in_specs=[pl.BlockSpec((rows,), lambda i: (i,), memory_space=pltpu.SMEM),
                  pl.BlockSpec((rows * IN_ROWS, LANES), lambda i: (i, 0)),
                  pl.BlockSpec(memory_space=pl.ANY)],
        out_specs=pl.BlockSpec(memory_space=pl.ANY),
        out_shape=jax.ShapeDtypeStruct((n_rows * IN_ROWS, LANES), jnp.uint32),
        scratch_shapes=[pltpu.SemaphoreType.DMA(())],
        input_output_aliases={2: 0},
        compiler_params=_cparams(("arbitrary",)),
    )(dest, h2, jnp.zeros((n_rows * IN_ROWS, LANES), jnp.uint32))


def _moe_kernel(elo_ref, ehi_ref, blk_ref, nact_ref, hs_ref, w1a_ref, w3a_ref, w2a_ref,
                w1b_ref, w3b_ref, w2b_ref, y_ref, *, d):
    del elo_ref, ehi_ref, blk_ref
    active = pl.program_id(0) < nact_ref[0]

    @pl.when(jnp.logical_not(active))
    def _():
        y_ref[...] = jnp.zeros_like(y_ref)

    @pl.when(active)
    def _():
        x, wa, wb = _unpack_token_tiles(hs_ref, d)

        def expert(w1_ref, w3_ref, wgt):
            a = jnp.dot(x, w1_ref[0], preferred_element_type=F32)
            g = jnp.dot(x, w3_ref[0], preferred_element_type=F32)
            return (a * jax.nn.sigmoid(a) * g * wgt).astype(BF16)

        y = (jnp.dot(expert(w1a_ref, w3a_ref, wa), w2a_ref[0], preferred_element_type=F32)
             + jnp.dot(expert(w1b_ref, w3b_ref, wb), w2b_ref[0], preferred_element_type=F32))
        word = _pack_words(y)
        for s in range(OUT_ROWS):
            y_ref[_tok_rows(s, y.shape[0], OUT_ROWS), :] = word[:, s * LANES:(s + 1) * LANES]


def _moe(hs, e_lo, e_hi, blk, nact, w1, w3, w2):
    n_rows = hs.shape[0] // IN_ROWS
    d = w1.shape[1]
    ff = w1.shape[2]
    tm = MOE_TILE
    n_tiles = n_rows // tm
    assert d == 2 * OUT_ROWS * LANES
    wspec = lambda shape, which: pl.BlockSpec(shape, (lambda j, lo, hi, bk, na: (lo[j], 0, 0)) if which == 0
                                              else (lambda j, lo, hi, bk, na: (hi[j], 0, 0)))
    grid_spec = pltpu.PrefetchScalarGridSpec(
        num_scalar_prefetch=4,
        grid=(n_tiles,),
        in_specs=[pl.BlockSpec((tm * IN_ROWS, LANES), lambda j, lo, hi, bk, na: (bk[j], 0)),
                  wspec((1, d, ff), 0), wspec((1, d, ff), 0), wspec((1, ff, d), 0),
                  wspec((1, d, ff), 1), wspec((1, d, ff), 1), wspec((1, ff, d), 1)],
        out_specs=pl.BlockSpec((tm * OUT_ROWS, LANES), lambda j, lo, hi, bk, na: (j, 0)),
    )
    return pl.pallas_call(
        functools.partial(_moe_kernel, d=d),
        grid_spec=grid_spec,
        out_shape=jax.ShapeDtypeStruct((n_rows * OUT_ROWS, LANES), jnp.uint32),
        compiler_params=_cparams(("arbitrary",)),
    )(e_lo, e_hi, blk, nact, hs, w1, w3, w2, w1, w3, w2)


def _combine_kernel(dest_ref, next_ref, x_ref, gate_ref, y_ref, o_ref, buf_ref, sem, *, rows):
    i = pl.program_id(0)
    slot = i & 1

    def gather(d_ref, sl):
        return lambda r: _row_copy(y_ref, d_ref[r], buf_ref.at[sl], r, sem.at[sl], OUT_ROWS)

    @pl.when(i == 0)
    def _():
        _start_rows(rows, gather(dest_ref, 0))

    @pl.when(i + 1 < pl.num_programs(0))
    def _():
        _start_rows(rows, gather(next_ref, 1 - slot))

    _wait_rows(rows, gather(dest_ref, slot))
    half = OUT_ROWS * LANES
    for s in range(OUT_ROWS):
        hi, lo = _unpack_words(buf_ref[slot, _tok_rows(s, rows, OUT_ROWS), :])
        for off, val in ((s * LANES, hi), (half + s * LANES, lo)):
            cols = slice(off, off + LANES)
            o_ref[:, cols] = x_ref[:, cols] + gate_ref[0, :, cols] * val


def _combine(dest, x2, gate, y, seq):
    n, d = x2.shape
    rows = min(ROW_TILE, seq)
    per_seq = seq // rows
    steps = n // rows
    return pl.pallas_call(
        functools.partial(_combine_kernel, rows=rows),
        grid=(steps,),
        in_specs=[pl.BlockSpec((rows,), lambda i: (i,), memory_space=pltpu.SMEM),
                  pl.BlockSpec((rows,), lambda i: (jnp.minimum(i + 1, steps - 1),), memory_space=pltpu.SMEM),
                  pl.BlockSpec((rows, d), lambda i: (i, 0)),
                  pl.BlockSpec((1, 1, d), lambda i: (i // per_seq, 0, 0)),
                  pl.BlockSpec(memory_space=pl.ANY)],
        out_specs=pl.BlockSpec((rows, d), lambda i: (i, 0)),
        out_shape=jax.ShapeDtypeStruct((n, d), F32),
        scratch_shapes=[pltpu.VMEM((2, rows * OUT_ROWS, LANES), jnp.uint32), pltpu.SemaphoreType.DMA((2,))],
        compiler_params=_cparams(("arbitrary",)),
    )(dest, dest, x2, gate, y)


def _moe_layer(x_new, h2, route, counts, gate_f, w1, w3, w2):
    b, s, d = x_new.shape
    n = b * s
    tm = MOE_TILE
    n_tiles = n // tm + N_COMBOS
    combo = route[:, 0, :].reshape(n).astype(jnp.int32)
    rank = route[:, 1, :].reshape(n).astype(jnp.int32)
    cnt = counts[:N_COMBOS, 0].astype(jnp.int32)
    tiles_per = (cnt + tm - 1) // tm
    tile_end = jnp.cumsum(tiles_per)
    row_off = (tile_end - tiles_per) * tm
    ids = jnp.arange(N_COMBOS, dtype=jnp.int32)

    def lookup(table, idx):
        return jnp.sum(jnp.where(idx[:, None] == ids[None, :], table[None, :], 0), axis=1)

    dest = lookup(row_off, combo) + rank
    nact = tile_end[-1:]
    blk = jnp.minimum(jnp.arange(n_tiles, dtype=jnp.int32), nact[0] - 1)
    tile_combo = jnp.sum((tile_end[None, :] <= blk[:, None]).astype(jnp.int32), axis=1)
    e_lo = lookup(jnp.asarray(_COMBO_LO), tile_combo)
    e_hi = lookup(jnp.asarray(_COMBO_HI), tile_combo)

    hs = _dispatch(dest, h2.reshape(n * SUBLANES, LANES), n_tiles * tm)
    ne = w1.shape[0] * w1.shape[1]
    y = _moe(hs, e_lo, e_hi, blk, nact.astype(jnp.int32),
             w1.reshape(ne, d, -1).astype(BF16), w3.reshape(ne, d, -1).astype(BF16),
             w2.reshape(ne, -1, d).astype(BF16))
    return _combine(dest, x_new.reshape(n, d), gate_f, y, s).reshape(b, s, d)


def kernel(x, c, positions, ada_w, ada_b, norm_mix, norm_ffn, da_w_qkv, da_q_norm, da_k_norm, da_lambda_q1, da_lambda_k1, da_lambda_q2, da_lambda_k2, da_subln, da_w_o, mla_w_in, mla_q_a_norm, mla_kv_a_norm, mla_w_uq, mla_w_ukv, mla_q_nope_norm, mla_q_pe_norm, mla_k_nope_norm, mla_k_pe_norm, mla_w_o, moe_w_group, moe_b_group, moe_w_router, moe_b_router, moe_w1, moe_w3, moe_w2):
    b, s, d = x.shape
    depth = ada_w.shape[0]
    mod_all = _modulation(c, ada_w, ada_b).reshape(depth, b, 6, d)
    posr = positions.astype(F32).reshape(b, 1, s)
    for i in range(depth):
        mod = mod_all[i]
        j = i // 2
        if i % 2 == 0:
            lam_init = 0.8 - 0.6 * math.exp(-0.3 * i)
            q, k, v = _da_pre(x, mod, posr, norm_mix[i], da_w_qkv[j], da_q_norm[j], da_k_norm[j])
            o = _da_attn(q, k, v, da_lambda_q1[j], da_lambda_k1[j], da_lambda_q2[j], da_lambda_k2[j],
                         da_subln[j], lam_init)
            w_o = da_w_o[j]
        else:
            q, k, v = _mla_pre(x, mod, posr, norm_mix[i], mla_w_in[j], mla_q_a_norm[j], mla_kv_a_norm[j],
                               mla_w_uq[j], mla_w_ukv[j], mla_q_nope_norm[j], mla_q_pe_norm[j],
                               mla_k_nope_norm[j], mla_k_pe_norm[j])
            o = _mla_attn(q, k, v)
            w_o = mla_w_o[j]
        x_new, h2, route, counts = _post(o, w_o, x, mod, norm_ffn[i], moe_w_group[i], moe_b_group[i],
                                         moe_w_router[i], moe_b_router[i])
        x = _moe_layer(x_new, h2, route, counts, mod[:, 5:6, :], moe_w1[i], moe_w3[i], moe_w2[i])
    return x
```

```python
import functools
import math

import numpy as np
import jax
import jax.numpy as jnp
from jax import lax
from jax.experimental import pallas as pl
from jax.experimental.pallas import tpu as pltpu

F32 = jnp.float32
BF16 = jnp.bfloat16

ROPE_THETA = 10000.0
EPS = 1e-6
NEG_INF = -1e30
LOG2E = math.log2(math.e)
DA_HEAD_DIM = 64
MLA_HEADS = 16
MLA_NOPE = 64
MLA_ROPE = 32
MLA_V = 64
MLA_Q_RANK = 384
MLA_KV_RANK = 256
N_GROUPS = 4
EXPERTS_PER_GROUP = 8
EXPERT_FF = 256

LANES = 128
VMEM_LIMIT = 56 * 1024 * 1024
TOK_TILE = 1024
ATT_TILE = 256
MOE_TILE = 256
ROW_TILE = 1024
ROW_UNROLL = 8
SUBLANES = 8

N_PAIRS = EXPERTS_PER_GROUP * (EXPERTS_PER_GROUP - 1) // 2
N_COMBOS = N_GROUPS * N_PAIRS


def _combo_tables():
    lo_t, hi_t = [], []
    for g in range(N_GROUPS):
        for lo in range(EXPERTS_PER_GROUP):
            for hi in range(lo + 1, EXPERTS_PER_GROUP):
                lo_t.append(g * EXPERTS_PER_GROUP + lo)
                hi_t.append(g * EXPERTS_PER_GROUP + hi)
    return np.asarray(lo_t, np.int32), np.asarray(hi_t, np.int32)


_COMBO_LO, _COMBO_HI = _combo_tables()


def _cparams(sem):
    return pltpu.CompilerParams(dimension_semantics=sem, vmem_limit_bytes=VMEM_LIMIT)


def _adaln_h(x, mod_ref, g_ref, shift_row, scale_row):
    ms = jnp.mean(x * x, axis=-1, keepdims=True)
    h = x * lax.rsqrt(ms + EPS) * g_ref[...]
    return h * (1.0 + mod_ref[0, scale_row:scale_row + 1, :]) + mod_ref[0, shift_row:shift_row + 1, :]


def _mod_kernel(c_ref, w_ref, b_ref, o_ref):
    c = c_ref[...]
    cond = (c * jax.nn.sigmoid(c)).astype(BF16)
    o_ref[0] = jnp.dot(cond, w_ref[0].astype(BF16), preferred_element_type=F32) + b_ref[0]


def _modulation(c, ada_w, ada_b):
    depth, d, n6 = ada_w.shape
    b = c.shape[0]
    tn = 1536
    return pl.pallas_call(
        _mod_kernel,
        grid=(depth, n6 // tn),
        in_specs=[pl.BlockSpec((b, d), lambda i, j: (0, 0)),
                  pl.BlockSpec((1, d, tn), lambda i, j: (i, 0, j)),
                  pl.BlockSpec((1, 1, tn), lambda i, j: (i, 0, j))],
        out_specs=pl.BlockSpec((1, b, tn), lambda i, j: (i, 0, j)),
        out_shape=jax.ShapeDtypeStruct((depth, b, n6), F32),
        compiler_params=_cparams(("arbitrary", "arbitrary")),
    )(c, ada_w, ada_b.reshape(depth, 1, n6))


def _da_pre_kernel(x_ref, mod_ref, pos_ref, g_ref, wt_ref, qg_ref, kg_ref, invf_ref,
                   q_out, k_out, v_out, *, d, scale):
    h = _adaln_h(x_ref[0], mod_ref, g_ref, 0, 1)
    t = h.shape[0]
    qkv = jnp.dot(wt_ref[...], h.T.astype(BF16), preferred_element_type=F32)
    vt = qkv[2 * d:]
    tk = v_out.shape[3]
    for j in range(v_out.shape[1]):
        v_out[0, j] = vt[:, j * tk:(j + 1) * tk].astype(BF16)

    dh = DA_HEAD_DIM
    half = dh // 2
    reps = t // LANES
    ang = jnp.tile(invf_ref[...], (1, reps)) * pos_ref[0]
    cos = jnp.cos(ang)[None, None]
    sin = jnp.sin(ang)[None, None]

    def norm_rope(z, gain_ref):
        z3 = z.reshape(d // dh, dh, t)
        r = lax.rsqrt(jnp.mean(z3 * z3, axis=1, keepdims=True) + EPS)
        z4 = (z3 * r).reshape(d // dh, 2, half, t) * jnp.tile(gain_ref[...], (1, reps)).reshape(1, 2, half, t)
        z1 = z4[:, 0:1]
        z2 = z4[:, 1:2]
        return jnp.concatenate([z1 * cos - z2 * sin, z2 * cos + z1 * sin], axis=1).reshape(d, t)

    q_out[0] = (norm_rope(qkv[:d], qg_ref) * scale).astype(BF16)
    k_out[0] = norm_rope(qkv[d:2 * d], kg_ref).T.astype(BF16)


def _da_pre(x, mod, posr, norm_g, w_qkv, q_norm, k_norm):
    b, s, d = x.shape
    t = min(TOK_TILE, s)
    dh = DA_HEAD_DIM
    invf = (ROPE_THETA ** (-np.arange(0, dh, 2, dtype=np.float32) / dh)).astype(np.float32)
    lanes = lambda v: jnp.broadcast_to(jnp.asarray(v, F32)[:, None], (v.shape[0], LANES))
    row = lambda bi, ti: (0, 0)
    tok = lambda bi, ti: (bi, ti, 0)
    tk = min(ATT_TILE, t)
    return pl.pallas_call(
        functools.partial(_da_pre_kernel, d=d, scale=dh ** -0.5 * LOG2E),
        grid=(b, s // t),
        in_specs=[pl.BlockSpec((1, t, d), tok),
                  pl.BlockSpec((1, 6, d), lambda bi, ti: (bi, 0, 0)),
                  pl.BlockSpec((1, 1, t), lambda bi, ti: (bi, 0, ti)),
                  pl.BlockSpec((1, d), row),
                  pl.BlockSpec((3 * d, d), row),
                  pl.BlockSpec((dh, LANES), row),
                  pl.BlockSpec((dh, LANES), row),
                  pl.BlockSpec((dh // 2, LANES), row)],
        out_specs=[pl.BlockSpec((1, d, t), lambda bi, ti: (bi, 0, ti)),
                   pl.BlockSpec((1, t, d), tok),
                   pl.BlockSpec((1, t // tk, d, tk), lambda bi, ti: (bi, ti, 0, 0))],
        out_shape=[jax.ShapeDtypeStruct((b, d, s), BF16),
                   jax.ShapeDtypeStruct((b, s, d), BF16),
                   jax.ShapeDtypeStruct((b, s // tk, d, tk), BF16)],
        compiler_params=_cparams(("arbitrary", "arbitrary")),
    )(x, mod, posr, norm_g.reshape(1, d), w_qkv.T.astype(BF16), lanes(q_norm), lanes(k_norm), lanes(invf))


def _softmax_block(buf, vt_blk, g, m_ref, l_ref, acc_ref, mask):
    s = buf[0][g]
    if mask is None:
        bmax = buf[1][g]
    else:
        s = jnp.where(mask, s, NEG_INF)
        bmax = jnp.max(s, axis=0, keepdims=True)
    m_prev = m_ref[g]
    m_new = jnp.maximum(m_prev, bmax)
    alpha = jnp.exp2(m_prev - m_new)
    p = jnp.exp2(s - m_new)
    l_ref[g] = alpha * l_ref[g] + jnp.sum(p, axis=0, keepdims=True)
    acc_ref[g] = alpha * acc_ref[g] + jnp.dot(vt_blk, p.astype(BF16), preferred_element_type=F32)
    m_ref[g] = m_new


def _causal_sweep(qi, scores, softmax, sa_ref, sb_ref, diag_mask):
    def fill(buf, kb):
        for g, s in enumerate(scores(kb)):
            buf[0][g] = s
            buf[1][g] = jnp.max(s, axis=0, keepdims=True)

    fill(sa_ref, 0)

    def body(j, c):
        kb = 2 * j
        fill(sb_ref, kb + 1)
        softmax(sa_ref, kb, None)
        fill(sa_ref, kb + 2)
        softmax(sb_ref, kb + 1, None)
        return c

    lax.fori_loop(0, lax.shift_right_logical(qi, 1), body, 0)
    odd = (qi & 1) == 1

    @pl.when(jnp.logical_not(odd))
    def _():
        softmax(sa_ref, qi, diag_mask)

    @pl.when(odd)
    def _():
        fill(sb_ref, qi)
        softmax(sa_ref, qi - 1, None)
        softmax(sb_ref, qi, diag_mask)


def _init_stats(m_ref, l_ref, acc_ref):
    m_ref[...] = jnp.full(m_ref.shape, NEG_INF, F32)
    l_ref[...] = jnp.zeros(l_ref.shape, F32)
    acc_ref[...] = jnp.zeros(acc_ref.shape, F32)


def _causal_mask_t(tk, cols, tq):
    c = lax.broadcasted_iota(jnp.int32, (tk, cols), 0)
    r = lax.broadcasted_iota(jnp.int32, (tk, cols), 1)
    r = jnp.where(r >= tq, r - tq, r)
    return c <= r


def _da_attn_kernel(lq1_ref, lk1_ref, lq2_ref, lk2_ref, sub_ref, qt_ref, k_ref, vt_ref, o_ref,
                    qs_ref, m_ref, l_ref, acc_ref, sa_ref, sb_ref, ma_ref, mb_ref, *, tq, heads, lam_init):
    qi = pl.program_id(2)
    sub = lax.broadcasted_iota(jnp.int32, (LANES, tq), 0)
    for g in range(heads):
        qt = qt_ref[0, g * LANES:(g + 1) * LANES, :]
        zero = jnp.zeros_like(qt)
        qs_ref[g] = jnp.concatenate([jnp.where(sub < DA_HEAD_DIM, qt, zero),
                                     jnp.where(sub >= DA_HEAD_DIM, qt, zero)], axis=1)
    _init_stats(m_ref, l_ref, acc_ref)

    def scores(kb):
        off = pl.multiple_of(kb * tq, tq)
        return [jnp.dot(k_ref[0, pl.ds(off, tq), g * LANES:(g + 1) * LANES], qs_ref[g],
                        preferred_element_type=F32) for g in range(heads)]

    def softmax(s, kb, mask):
        for g in range(heads):
            _softmax_block(s, vt_ref[0, kb, g * LANES:(g + 1) * LANES, :], g, m_ref, l_ref, acc_ref, mask)

    _causal_sweep(qi, scores, softmax, (sa_ref, ma_ref), (sb_ref, mb_ref), _causal_mask_t(tq, 2 * tq, tq))

    lam = (jnp.exp(jnp.sum(lq1_ref[...] * lk1_ref[...], axis=1, keepdims=True))
           - jnp.exp(jnp.sum(lq2_ref[...] * lk2_ref[...], axis=1, keepdims=True)) + lam_init)
    for g in range(heads):
        ot = acc_ref[g] * (1.0 / l_ref[g])
        dd = (ot[:, :tq] - lam * ot[:, tq:]).T
        ms = jnp.mean(dd * dd, axis=-1, keepdims=True)
        o_ref[0, :, g * LANES:(g + 1) * LANES] = (
            dd * lax.rsqrt(ms + EPS) * sub_ref[...] * (1.0 - lam_init)).astype(BF16)


def _da_attn(qt, k, vt, lq1, lk1, lq2, lk2, subln, lam_init):
    b, s, d = k.shape
    heads = 8
    wb = heads * LANES
    tq = vt.shape[3]
    vec = lambda bi, hi, qi: (0, 0)
    return pl.pallas_call(
        functools.partial(_da_attn_kernel, tq=tq, heads=heads, lam_init=lam_init),
        grid=(b, d // wb, s // tq),
        in_specs=[pl.BlockSpec((1, DA_HEAD_DIM), vec)] * 4 + [
            pl.BlockSpec((1, LANES), vec),
            pl.BlockSpec((1, wb, tq), lambda bi, hi, qi: (bi, hi, qi)),
            pl.BlockSpec((1, s, wb), lambda bi, hi, qi: (bi, 0, hi)),
            pl.BlockSpec((1, s // tq, wb, tq), lambda bi, hi, qi: (bi, 0, hi, 0))],
        out_specs=pl.BlockSpec((1, tq, wb), lambda bi, hi, qi: (bi, qi, hi)),
        out_shape=jax.ShapeDtypeStruct((b, s, d), BF16),
        scratch_shapes=[pltpu.VMEM((heads, LANES, 2 * tq), BF16),
                        pltpu.VMEM((heads, 1, 2 * tq), F32),
                        pltpu.VMEM((heads, 1, 2 * tq), F32),
                        pltpu.VMEM((heads, LANES, 2 * tq), F32),
                        pltpu.VMEM((heads, tq, 2 * tq), F32),
                        pltpu.VMEM((heads, tq, 2 * tq), F32),
                        pltpu.VMEM((heads, 1, 2 * tq), F32),
                        pltpu.VMEM((heads, 1, 2 * tq), F32)],
        compiler_params=_cparams(("arbitrary", "arbitrary", "arbitrary")),
    )(lq1.reshape(1, -1), lk1.reshape(1, -1), lq2.reshape(1, -1), lk2.reshape(1, -1),
      subln.reshape(1, -1), qt, k, vt)


def _mla_attn_kernel(qt_ref, k_ref, vt_ref, o_ref, m_ref, l_ref, acc_ref, sa_ref, sb_ref, ma_ref, mb_ref,
                     *, tq, heads):
    qi = pl.program_id(2)
    _init_stats(m_ref, l_ref, acc_ref)

    def scores(kb):
        off = pl.multiple_of(kb * tq, tq)
        return [jnp.dot(k_ref[0, pl.ds(off, tq), g * LANES:(g + 1) * LANES],
                        qt_ref[0, g * LANES:(g + 1) * LANES, :],
                        preferred_element_type=F32) for g in range(heads)]

    def softmax(s, kb, mask):
        for g in range(heads):
            _softmax_block(s, vt_ref[0, kb, g * MLA_V:(g + 1) * MLA_V, :], g, m_ref, l_ref, acc_ref, mask)

    _causal_sweep(qi, scores, softmax, (sa_ref, ma_ref), (sb_ref, mb_ref), _causal_mask_t(tq, tq, tq))
    for p in range(heads // 2):
        ot = jnp.concatenate([acc_ref[2 * p] * (1.0 / l_ref[2 * p]),
                              acc_ref[2 * p + 1] * (1.0 / l_ref[2 * p + 1])], axis=0)
        o_ref[0, :, p * LANES:(p + 1) * LANES] = ot.T.astype(BF16)


def _mla_attn(qt, k, vt):
    b, s, dk = k.shape
    heads = 16
    tq = vt.shape[3]
    return pl.pallas_call(
        functools.partial(_mla_attn_kernel, tq=tq, heads=heads),
        grid=(b, dk // (heads * LANES), s // tq),
        in_specs=[pl.BlockSpec((1, heads * LANES, tq), lambda bi, hi, qi: (bi, hi, qi)),
                  pl.BlockSpec((1, s, heads * LANES), lambda bi, hi, qi: (bi, 0, hi)),
                  pl.BlockSpec((1, s // tq, heads * MLA_V, tq), lambda bi, hi, qi: (bi, 0, hi, 0))],
        out_specs=pl.BlockSpec((1, tq, heads * MLA_V), lambda bi, hi, qi: (bi, qi, hi)),
        out_shape=jax.ShapeDtypeStruct((b, s, dk // LANES * MLA_V), BF16),
        scratch_shapes=[pltpu.VMEM((heads, 1, tq), F32),
                        pltpu.VMEM((heads, 1, tq), F32),
                        pltpu.VMEM((heads, MLA_V, tq), F32),
                        pltpu.VMEM((heads, tq, tq), F32),
                        pltpu.VMEM((heads, tq, tq), F32),
                        pltpu.VMEM((heads, 1, tq), F32),
                        pltpu.VMEM((heads, 1, tq), F32)],
        compiler_params=_cparams(("arbitrary", "arbitrary", "arbitrary")),
    )(qt, k, vt)


def _mla_pre_kernel(x_ref, mod_ref, pos_ref, g_ref, wint_ref, qag_ref, kvag_ref, kpeg_ref,
                    wuqt_ref, wukvt_ref, qng_ref, qpg_ref, kng_ref, invf_ref,
                    q_out, k_out, v_out, *, scale):
    h = _adaln_h(x_ref[0], mod_ref, g_ref, 0, 1)
    t = h.shape[0]
    reps = t // LANES
    nh, nope, rope, vd = MLA_HEADS, MLA_NOPE, MLA_ROPE, MLA_V
    lanes = lambda ref: jnp.tile(ref[...], (1, reps))

    def rms(z, axis):
        return z * lax.rsqrt(jnp.mean(z * z, axis=axis, keepdims=True) + EPS)

    lat = jnp.dot(wint_ref[...], h.T.astype(BF16), preferred_element_type=F32)
    cqn = (rms(lat[:MLA_Q_RANK], 0) * lanes(qag_ref)).astype(BF16)
    ckvn = (rms(lat[MLA_Q_RANK:MLA_Q_RANK + MLA_KV_RANK], 0) * lanes(kvag_ref)).astype(BF16)
    kpen = rms(lat[MLA_Q_RANK + MLA_KV_RANK:], 0) * lanes(kpeg_ref)
    q3 = jnp.dot(wuqt_ref[...], cqn, preferred_element_type=F32).reshape(nh, nope + rope, t)
    kv3 = jnp.dot(wukvt_ref[...], ckvn, preferred_element_type=F32).reshape(nh, nope + vd, t)

    vt = kv3[:, nope:].reshape(nh * vd, t)
    tk = v_out.shape[3]
    for j in range(v_out.shape[1]):
        v_out[0, j] = vt[:, j * tk:(j + 1) * tk].astype(BF16)

    half = rope // 2
    ang = lanes(invf_ref) * pos_ref[0]
    cos = jnp.cos(ang)[None]
    sin = jnp.sin(ang)[None]

    def rotary(z):
        z1 = z[:, :half]
        z2 = z[:, half:]
        return jnp.concatenate([z1 * cos - z2 * sin, z2 * cos + z1 * sin], axis=1)

    pad = jnp.zeros((nh, LANES - nope - rope, t), F32)
    qn = rms(q3[:, :nope], 1) * lanes(qng_ref)[None]
    qp = rotary(rms(q3[:, nope:], 1) * lanes(qpg_ref)[None])
    q_out[0] = (jnp.concatenate([qn, qp, pad], axis=1).reshape(nh * LANES, t) * scale).astype(BF16)
    kn = rms(kv3[:, :nope], 1) * lanes(kng_ref)[None]
    kp = jnp.broadcast_to(rotary(kpen[None]), (nh, rope, t))
    k_out[0] = jnp.concatenate([kn, kp, pad], axis=1).reshape(nh * LANES, t).T.astype(BF16)


def _mla_pre(x, mod, posr, norm_g, w_in, q_a_norm, kv_a_norm, w_uq, w_ukv,
             q_nope_norm, q_pe_norm, k_nope_norm, k_pe_norm):
    b, s, d = x.shape
    t = min(TOK_TILE, s)
    nh, nope, rope, vd = MLA_HEADS, MLA_NOPE, MLA_ROPE, MLA_V
    w = nh * LANES
    invf = (ROPE_THETA ** (-np.arange(0, rope, 2, dtype=np.float32) / rope)).astype(np.float32)
    lanes = lambda v: jnp.broadcast_to(jnp.asarray(v, F32)[:, None], (v.shape[0], LANES))
    row = lambda bi, ti: (0, 0)
    tok = lambda bi, ti: (bi, ti, 0)
    full = lambda a: pl.BlockSpec(a.shape, row)
    consts = [norm_g.reshape(1, d), w_in.T.astype(BF16), lanes(q_a_norm), lanes(kv_a_norm), lanes(k_pe_norm),
              w_uq.T.astype(BF16), w_ukv.T.astype(BF16), lanes(q_nope_norm), lanes(q_pe_norm),
              lanes(k_nope_norm), lanes(invf)]
    tk = min(ATT_TILE, t)
    return pl.pallas_call(
        functools.partial(_mla_pre_kernel, scale=(nope + rope) ** -0.5 * LOG2E),
        grid=(b, s // t),
        in_specs=[pl.BlockSpec((1, t, d), tok),
                  pl.BlockSpec((1, 6, d), lambda bi, ti: (bi, 0, 0)),
                  pl.BlockSpec((1, 1, t), lambda bi, ti: (bi, 0, ti))] + [full(a) for a in consts],
        out_specs=[pl.BlockSpec((1, w, t), lambda bi, ti: (bi, 0, ti)),
                   pl.BlockSpec((1, t, w), tok),
                   pl.BlockSpec((1, t // tk, nh * vd, tk), lambda bi, ti: (bi, ti, 0, 0))],
        out_shape=[jax.ShapeDtypeStruct((b, w, s), BF16),
                   jax.ShapeDtypeStruct((b, s, w), BF16),
                   jax.ShapeDtypeStruct((b, s // tk, nh * vd, tk), BF16)],
        compiler_params=_cparams(("arbitrary", "arbitrary")),
    )(x, mod, posr, *consts)


def _post_kernel(o_ref, wo_ref, x_ref, mod_ref, g_ref, wr_ref, bsel_ref,
                 x_out, h_out, route_out, cnt_out, run_ref, *, d):
    first = (pl.program_id(0) == 0) & (pl.program_id(1) == 0)

    @pl.when(first)
    def _():
        run_ref[...] = jnp.zeros_like(run_ref)

    y = jnp.dot(o_ref[0], wo_ref[...], preferred_element_type=F32)
    x = x_ref[0] + mod_ref[0, 2:3, :] * y
    x_out[0] = x
    h = _adaln_h(x, mod_ref, g_ref, 3, 4)
    t = h.shape[0]

    hh = h.astype(BF16)
    hl = (h - hh.astype(F32)).astype(BF16)
    logits = (jnp.dot(hh, wr_ref[0], preferred_element_type=F32)
              + jnp.dot(hl, wr_ref[0], preferred_element_type=F32)
              + jnp.dot(hh, wr_ref[1], preferred_element_type=F32))

    ng, ne = N_GROUPS, EXPERTS_PER_GROUP

    def route(lt):
        n = lt.shape[1]
        sub = lax.broadcasted_iota(jnp.int32, (SUBLANES, n), 0).astype(F32)

        def first_argmax(val):
            mx = jnp.max(val, axis=0, keepdims=True)
            return jnp.min(jnp.where(val == mx, sub, float(SUBLANES)), axis=0, keepdims=True)

        def pick(val, idx):
            return jnp.sum(jnp.where(sub == idx, val, 0.0), axis=0, keepdims=True)

        gmask = sub < ng
        gl = jnp.where(gmask, lt[:SUBLANES], NEG_INF)
        ge = jnp.exp(gl - jnp.max(gl, axis=0, keepdims=True))
        gprob = ge / jnp.sum(ge, axis=0, keepdims=True)
        gidx = first_argmax(jnp.where(gmask, gprob + bsel_ref[0], NEG_INF))
        ggate = pick(gprob, gidx)

        el = jnp.zeros((SUBLANES, n), F32)
        eb = jnp.zeros((SUBLANES, n), F32)
        for g in range(ng):
            chosen = gidx == float(g)
            el = jnp.where(chosen, lt[SUBLANES * (g + 1):SUBLANES * (g + 2)], el)
            eb = jnp.where(chosen, bsel_ref[g + 1], eb)
        ee = jnp.exp(el - jnp.max(el, axis=0, keepdims=True))
        eprob = ee / jnp.sum(ee, axis=0, keepdims=True)
        sel = eprob + eb
        e1 = first_argmax(sel)
        e2 = first_argmax(jnp.where(sub == e1, NEG_INF, sel))
        p1 = pick(eprob, e1)
        p2 = pick(eprob, e2)
        psum = p1 + p2
        w1 = p1 / psum * ggate
        w2 = p2 / psum * ggate
        swap = e2 < e1
        lo = jnp.where(swap, e2, e1)
        hi = jnp.where(swap, e1, e2)
        combo = gidx * float(N_PAIRS) + lo * (2.0 * ne - 1.0 - lo) * 0.5 + (hi - lo - 1.0)
        return jnp.where(swap, w2, w1), jnp.where(swap, w1, w2), combo

    lt = logits.T
    parts = [route(lt[:, j * (t // 2):(j + 1) * (t // 2)]) for j in range(2)]
    wa, wb, combo = [jnp.concatenate([p[i] for p in parts], axis=1) for i in range(3)]
    sub = lax.broadcasted_iota(jnp.int32, (SUBLANES, t), 0).astype(F32)

    row = lax.broadcasted_iota(jnp.int32, (LANES, t), 0).astype(F32)
    wrow = jnp.where(row == 0.0, wa, jnp.where(row == 1.0, wb, 0.0)).T
    _pack_token_tiles(h_out, h, wrow)

    onehot = row == combo
    oh = jnp.where(onehot, 1.0, 0.0)
    r_i = lax.broadcasted_iota(jnp.int32, (t, t), 0)
    c_i = lax.broadcasted_iota(jnp.int32, (t, t), 1)
    earlier = jnp.where(r_i < c_i, 1.0, 0.0).astype(BF16)
    before = jnp.dot(oh.astype(BF16), earlier, preferred_element_type=F32) + run_ref[...]
    rank = jnp.sum(jnp.where(onehot, before, 0.0), axis=0, keepdims=True)
    run = run_ref[...] + jnp.sum(oh, axis=1, keepdims=True)
    run_ref[...] = run
    cnt_out[...] = run
    route_out[0] = jnp.where(sub == 0.0, combo, jnp.where(sub == 1.0, rank, 0.0))


def _post(o, w_o, x, mod, norm_g, w_group, b_group, w_router, b_router):
    b, s, d = x.shape
    t = min(TOK_TILE, s)
    do = o.shape[2]
    ng, ne = N_GROUPS, EXPERTS_PER_GROUP
    wr = jnp.concatenate([jnp.pad(w_group, ((0, 0), (0, SUBLANES - ng))),
                          jnp.pad(w_router, ((0, 0), (0, LANES - SUBLANES - ng * ne)))], axis=1)
    wr_hi = wr.astype(BF16)
    wr_lo = (wr - wr_hi.astype(F32)).astype(BF16)
    wr2 = jnp.stack([wr_hi, wr_lo])
    bsel = jnp.concatenate([jnp.pad(b_group, (0, SUBLANES - ng))[None, :], b_router], axis=0)[:, :, None]
    row = lambda bi, ti: (0, 0)
    tok = lambda bi, ti: (bi, ti, 0)
    return pl.pallas_call(
        functools.partial(_post_kernel, d=d),
        grid=(b, s // t),
        in_specs=[pl.BlockSpec((1, t, do), tok),
                  pl.BlockSpec((do, d), row),
                  pl.BlockSpec((1, t, d), tok),
                  pl.BlockSpec((1, 6, d), lambda bi, ti: (bi, 0, 0)),
                  pl.BlockSpec((1, d), row),
                  pl.BlockSpec((2, d, LANES), lambda bi, ti: (0, 0, 0)),
                  pl.BlockSpec((ng + 1, ne, 1), lambda bi, ti: (0, 0, 0))],
        out_specs=[pl.BlockSpec((1, t, d), tok),
                   pl.BlockSpec((1, t * SUBLANES, LANES), tok),
                   pl.BlockSpec((1, SUBLANES, t), lambda bi, ti: (bi * (s // t) + ti, 0, 0)),
                   pl.BlockSpec((LANES, 1), row)],
        out_shape=[jax.ShapeDtypeStruct((b, s, d), F32),
                   jax.ShapeDtypeStruct((b, s * SUBLANES, LANES), jnp.uint32),
                   jax.ShapeDtypeStruct((b * s // t, SUBLANES, t), F32),
                   jax.ShapeDtypeStruct((LANES, 1), F32)],
        scratch_shapes=[pltpu.VMEM((LANES, 1), F32)],
        compiler_params=_cparams(("arbitrary", "arbitrary")),
    )(o, w_o.astype(BF16), x, mod, norm_g.reshape(1, d), wr2, bsel)


IN_ROWS = SUBLANES
OUT_ROWS = SUBLANES // 2


def _tok_rows(s, n, per):
    return pl.ds(s, n, stride=per)


def _tok_span(tok, per):
    return pl.ds(pl.multiple_of(tok * per, per), per)


def _pack_words(x):
    half = x.shape[1] // 2
    bits = pltpu.bitcast(x.astype(BF16).astype(F32), jnp.uint32)
    return bits[:, :half] | (bits[:, half:] >> 16)


def _unpack_words(w):
    return pltpu.bitcast(w & jnp.uint32(0xFFFF0000), F32), pltpu.bitcast(w << 16, F32)


def _pack_token_tiles(p_out, h, wrow):
    t, d = h.shape
    word = _pack_words(h)
    nw = d // 2 // LANES
    for s in range(nw):
        p_out[0, _tok_rows(s, t, IN_ROWS), :] = word[:, s * LANES:(s + 1) * LANES]
    p_out[0, _tok_rows(nw, t, IN_ROWS), :] = pltpu.bitcast(wrow, jnp.uint32)
    for s in range(nw + 1, IN_ROWS):
        p_out[0, _tok_rows(s, t, IN_ROWS), :] = jnp.zeros((t, LANES), jnp.uint32)


def _unpack_token_tiles(hs_ref, d):
    nw = d // 2 // LANES
    tm = hs_ref.shape[0] // IN_ROWS
    pairs = [_unpack_words(hs_ref[_tok_rows(s, tm, IN_ROWS), :]) for s in range(nw)]
    x = jnp.concatenate([p[0].astype(BF16) for p in pairs] + [p[1].astype(BF16) for p in pairs], axis=1)
    wrow = pltpu.bitcast(hs_ref[_tok_rows(nw, tm, IN_ROWS), :], F32)
    return x, wrow[:, 0:1], wrow[:, 1:2]


def _row_copy(src_ref, src_row, dst_ref, dst_row, sem, per):
    return pltpu.make_async_copy(src_ref.at[_tok_span(src_row, per), :], dst_ref.at[_tok_span(dst_row, per), :], sem)


def _start_rows(rows, make_copy):
    def start(i, c):
        for u in range(ROW_UNROLL):
            make_copy(i * ROW_UNROLL + u).start(priority=u % 2)
        return c

    lax.fori_loop(0, rows // ROW_UNROLL, start, 0)


def _wait_rows(rows, make_copy):
    def wait(i, c):
        for u in range(ROW_UNROLL):
            make_copy(0).wait()
        return c

    lax.fori_loop(0, rows // ROW_UNROLL, wait, 0)


def _dispatch_kernel(dest_ref, h_ref, init_ref, hs_ref, sem, *, rows):
    del init_ref
    copy = lambda r: _row_copy(h_ref, r, hs_ref, dest_ref[r], sem, IN_ROWS)
    _start_rows(rows, copy)
    _wait_rows(rows, copy)


def _dispatch(dest, h2, n_rows):
    n = h2.shape[0] // IN_ROWS
    rows = min(ROW_TILE, n)
    return pl.pallas_call(
        functools.partial(_dispatch_kernel, rows=rows),
        grid=(n // rows,),
        in_specs=[pl.BlockSpec((rows,), lambda i: (i,), memory_space=pltpu.SMEM),
                  pl.BlockSpec((rows * IN_ROWS, LANES), lambda i: (i, 0)),
                  pl.BlockSpec(memory_space=pl.ANY)],
        out_specs=pl.BlockSpec(memory_space=pl.ANY),
        out_shape=jax.ShapeDtypeStruct((n_rows * IN_ROWS, LANES), jnp.uint32),
        scratch_shapes=[pltpu.SemaphoreType.DMA(())],
        input_output_aliases={2: 0},
        compiler_params=_cparams(("arbitrary",)),
    )(dest, h2, jnp.zeros((n_rows * IN_ROWS, LANES), jnp.uint32))


def _moe_kernel(elo_ref, ehi_ref, blk_ref, nact_ref, hs_ref, w1a_ref, w3a_ref, w2a_ref,
                w1b_ref, w3b_ref, w2b_ref, y_ref, *, d):
    del elo_ref, ehi_ref, blk_ref
    active = pl.program_id(0) < nact_ref[0]

    @pl.when(jnp.logical_not(active))
    def _():
        y_ref[...] = jnp.zeros_like(y_ref)

    @pl.when(active)
    def _():
        x, wa, wb = _unpack_token_tiles(hs_ref, d)

        def expert(w1_ref, w3_ref, wgt):
            a = jnp.dot(x, w1_ref[0], preferred_element_type=F32)
            g = jnp.dot(x, w3_ref[0], preferred_element_type=F32)
            return (a * jax.nn.sigmoid(a) * g * wgt).astype(BF16)

        y = (jnp.dot(expert(w1a_ref, w3a_ref, wa), w2a_ref[0], preferred_element_type=F32)
             + jnp.dot(expert(w1b_ref, w3b_ref, wb), w2b_ref[0], preferred_element_type=F32))
        word = _pack_words(y)
        for s in range(OUT_ROWS):
            y_ref[_tok_rows(s, y.shape[0], OUT_ROWS), :] = word[:, s * LANES:(s + 1) * LANES]


def _moe(hs, e_lo, e_hi, blk, nact, w1, w3, w2):
    n_rows = hs.shape[0] // IN_ROWS
    d = w1.shape[1]
    ff = w1.shape[2]
    tm = MOE_TILE
    n_tiles = n_rows // tm
    assert d == 2 * OUT_ROWS * LANES
    wspec = lambda shape, which: pl.BlockSpec(shape, (lambda j, lo, hi, bk, na: (lo[j], 0, 0)) if which == 0
                                              else (lambda j, lo, hi, bk, na: (hi[j], 0, 0)))
    grid_spec = pltpu.PrefetchScalarGridSpec(
        num_scalar_prefetch=4,
        grid=(n_tiles,),
        in_specs=[pl.BlockSpec((tm * IN_ROWS, LANES), lambda j, lo, hi, bk, na: (bk[j], 0)),
                  wspec((1, d, ff), 0), wspec((1, d, ff), 0), wspec((1, ff, d), 0),
                  wspec((1, d, ff), 1), wspec((1, d, ff), 1), wspec((1, ff, d), 1)],
        out_specs=pl.BlockSpec((tm * OUT_ROWS, LANES), lambda j, lo, hi, bk, na: (j, 0)),
    )
    return pl.pallas_call(
        functools.partial(_moe_kernel, d=d),
        grid_spec=grid_spec,
        out_shape=jax.ShapeDtypeStruct((n_rows * OUT_ROWS, LANES), jnp.uint32),
        compiler_params=_cparams(("arbitrary",)),
    )(e_lo, e_hi, blk, nact, hs, w1, w3, w2, w1, w3, w2)


def _combine_kernel(dest_ref, next_ref, x_ref, gate_ref, y_ref, o_ref, buf_ref, sem, *, rows):
    i = pl.program_id(0)
    slot = i & 1

    def gather(d_ref, sl):
        return lambda r: _row_copy(y_ref, d_ref[r], buf_ref.at[sl], r, sem.at[sl], OUT_ROWS)

    @pl.when(i == 0)
    def _():
        _start_rows(rows, gather(dest_ref, 0))

    @pl.when(i + 1 < pl.num_programs(0))
    def _():
        _start_rows(rows, gather(next_ref, 1 - slot))

    _wait_rows(rows, gather(dest_ref, slot))
    half = OUT_ROWS * LANES
    for s in range(OUT_ROWS):
        hi, lo = _unpack_words(buf_ref[slot, _tok_rows(s, rows, OUT_ROWS), :])
        for off, val in ((s * LANES, hi), (half + s * LANES, lo)):
            cols = slice(off, off + LANES)
            o_ref[:, cols] = x_ref[:, cols] + gate_ref[0, :, cols] * val


def _combine(dest, x2, gate, y, seq):
    n, d = x2.shape
    rows = min(ROW_TILE, seq)
    per_seq = seq // rows
    steps = n // rows
    return pl.pallas_call(
        functools.partial(_combine_kernel, rows=rows),
        grid=(steps,),
        in_specs=[pl.BlockSpec((rows,), lambda i: (i,), memory_space=pltpu.SMEM),
                  pl.BlockSpec((rows,), lambda i: (jnp.minimum(i + 1, steps - 1),), memory_space=pltpu.SMEM),
                  pl.BlockSpec((rows, d), lambda i: (i, 0)),
                  pl.BlockSpec((1, 1, d), lambda i: (i // per_seq, 0, 0)),
                  pl.BlockSpec(memory_space=pl.ANY)],
        out_specs=pl.BlockSpec((rows, d), lambda i: (i, 0)),
        out_shape=jax.ShapeDtypeStruct((n, d), F32),
        scratch_shapes=[pltpu.VMEM((2, rows * OUT_ROWS, LANES), jnp.uint32), pltpu.SemaphoreType.DMA((2,))],
        compiler_params=_cparams(("arbitrary",)),
    )(dest, dest, x2, gate, y)


def _moe_layer(x_new, h2, route, counts, gate_f, w1, w3, w2):
    b, s, d = x_new.shape
    n = b * s
    tm = MOE_TILE
    n_tiles = n // tm + N_COMBOS
    combo = route[:, 0, :].reshape(n).astype(jnp.int32)
    rank = route[:, 1, :].reshape(n).astype(jnp.int32)
    cnt = counts[:N_COMBOS, 0].astype(jnp.int32)
    tiles_per = (cnt + tm - 1) // tm
    tile_end = jnp.cumsum(tiles_per)
    row_off = (tile_end - tiles_per) * tm
    ids = jnp.arange(N_COMBOS, dtype=jnp.int32)

    def lookup(table, idx):
        return jnp.sum(jnp.where(idx[:, None] == ids[None, :], table[None, :], 0), axis=1)

    dest = lookup(row_off, combo) + rank
    nact = tile_end[-1:]
    blk = jnp.minimum(jnp.arange(n_tiles, dtype=jnp.int32), nact[0] - 1)
    tile_combo = jnp.sum((tile_end[None, :] <= blk[:, None]).astype(jnp.int32), axis=1)
    e_lo = lookup(jnp.asarray(_COMBO_LO), tile_combo)
    e_hi = lookup(jnp.asarray(_COMBO_HI), tile_combo)

    hs = _dispatch(dest, h2.reshape(n * SUBLANES, LANES), n_tiles * tm)
    ne = w1.shape[0] * w1.shape[1]
    y = _moe(hs, e_lo, e_hi, blk, nact.astype(jnp.int32),
             w1.reshape(ne, d, -1).astype(BF16), w3.reshape(ne, d, -1).astype(BF16),
             w2.reshape(ne, -1, d).astype(BF16))
    return _combine(dest, x_new.reshape(n, d), gate_f, y, s).reshape(b, s, d)


def kernel(x, c, positions, ada_w, ada_b, norm_mix, norm_ffn, da_w_qkv, da_q_norm, da_k_norm, da_lambda_q1, da_lambda_k1, da_lambda_q2, da_lambda_k2, da_subln, da_w_o, mla_w_in, mla_q_a_norm, mla_kv_a_norm, mla_w_uq, mla_w_ukv, mla_q_nope_norm, mla_q_pe_norm, mla_k_nope_norm, mla_k_pe_norm, mla_w_o, moe_w_group, moe_b_group, moe_w_router, moe_b_router, moe_w1, moe_w3, moe_w2):
    b, s, d = x.shape
    depth = ada_w.shape[0]
    mod_all = _modulation(c, ada_w, ada_b).reshape(depth, b, 6, d)
    posr = positions.astype(F32).reshape(b, 1, s)
    for i in range(depth):
        mod = mod_all[i]
        j = i // 2
        if i % 2 == 0:
            lam_init = 0.8 - 0.6 * math.exp(-0.3 * i)
            q, k, v = _da_pre(x, mod, posr, norm_mix[i], da_w_qkv[j], da_q_norm[j], da_k_norm[j])
            o = _da_attn(q, k, v, da_lambda_q1[j], da_lambda_k1[j], da_lambda_q2[j], da_lambda_k2[j],
                         da_subln[j], lam_init)
            w_o = da_w_o[j]
        else:
            q, k, v = _mla_pre(x, mod, posr, norm_mix[i], mla_w_in[j], mla_q_a_norm[j], mla_kv_a_norm[j],
                               mla_w_uq[j], mla_w_ukv[j], mla_q_nope_norm[j], mla_q_pe_norm[j],
                               mla_k_nope_norm[j], mla_k_pe_norm[j])
            o = _mla_attn(q, k, v)
            w_o = mla_w_o[j]
        x_new, h2, route, counts = _post(o, w_o, x, mod, norm_ffn[i], moe_w_group[i], moe_b_group[i],
                                         moe_w_router[i], moe_b_router[i])
        x = _moe_layer(x_new, h2, route, counts, mod[:, 5:6, :], moe_w1[i], moe_w3[i], moe_w2[i])
    return x
```

```python
import functools
import math

import numpy as np
import jax
import jax.numpy as jnp
from jax import lax
from jax.experimental import pallas as pl
from jax.experimental.pallas import tpu as pltpu

F32 = jnp.float32
BF16 = jnp.bfloat16

ROPE_THETA = 10000.0
EPS = 1e-6
NEG_INF = -1e30
LOG2E = math.log2(math.e)
DA_HEAD_DIM = 64
MLA_HEADS = 16
MLA_NOPE = 64
MLA_ROPE = 32
MLA_V = 64
MLA_VP = 80
MLA_Q_RANK = 384
MLA_KV_RANK = 256
N_GROUPS = 4
EXPERTS_PER_GROUP = 8
EXPERT_FF = 256

LANES = 128
VMEM_LIMIT = 56 * 1024 * 1024
TOK_TILE = 1024
ATT_TILE = 256
MOE_TILE = 256
ROW_TILE = 1024
ROW_UNROLL = 8
SUBLANES = 8

N_PAIRS = EXPERTS_PER_GROUP * (EXPERTS_PER_GROUP - 1) // 2
N_COMBOS = N_GROUPS * N_PAIRS


def _combo_tables():
    lo_t, hi_t = [], []
    for g in range(N_GROUPS):
        for lo in range(EXPERTS_PER_GROUP):
            for hi in range(lo + 1, EXPERTS_PER_GROUP):
                lo_t.append(g * EXPERTS_PER_GROUP + lo)
                hi_t.append(g * EXPERTS_PER_GROUP + hi)
    return np.asarray(lo_t, np.int32), np.asarray(hi_t, np.int32)


_COMBO_LO, _COMBO_HI = _combo_tables()


def _cparams(sem):
    return pltpu.CompilerParams(dimension_semantics=sem, vmem_limit_bytes=VMEM_LIMIT)


def _adaln_h(x, mod_ref, g_ref, shift_row, scale_row):
    ms = jnp.mean(x * x, axis=-1, keepdims=True)
    h = x * lax.rsqrt(ms + EPS) * g_ref[...]
    return h * (1.0 + mod_ref[0, scale_row:scale_row + 1, :]) + mod_ref[0, shift_row:shift_row + 1, :]


def _mod_kernel(c_ref, w_ref, b_ref, o_ref):
    c = c_ref[...]
    cond = (c * jax.nn.sigmoid(c)).astype(BF16)
    o_ref[0] = jnp.dot(cond, w_ref[0].astype(BF16), preferred_element_type=F32) + b_ref[0]


def _modulation(c, ada_w, ada_b):
    depth, d, n6 = ada_w.shape
    b = c.shape[0]
    tn = 1536
    return pl.pallas_call(
        _mod_kernel,
        grid=(depth, n6 // tn),
        in_specs=[pl.BlockSpec((b, d), lambda i, j: (0, 0)),
                  pl.BlockSpec((1, d, tn), lambda i, j: (i, 0, j)),
                  pl.BlockSpec((1, 1, tn), lambda i, j: (i, 0, j))],
        out_specs=pl.BlockSpec((1, b, tn), lambda i, j: (i, 0, j)),
        out_shape=jax.ShapeDtypeStruct((depth, b, n6), F32),
        compiler_params=_cparams(("arbitrary", "arbitrary")),
    )(c, ada_w, ada_b.reshape(depth, 1, n6))


def _da_pre_kernel(x_ref, mod_ref, pos_ref, g_ref, wt_ref, qg_ref, kg_ref, invf_ref,
                   q_out, k_out, v_out, *, d, scale):
    h = _adaln_h(x_ref[0], mod_ref, g_ref, 0, 1)
    t = h.shape[0]
    qkv = jnp.dot(wt_ref[...], h.T.astype(BF16), preferred_element_type=F32)
    vt = qkv[2 * d:]
    tk = v_out.shape[3]
    for j in range(v_out.shape[1]):
        v_out[0, j] = vt[:, j * tk:(j + 1) * tk].astype(BF16)

    dh = DA_HEAD_DIM
    half = dh // 2
    reps = t // LANES
    ang = jnp.tile(invf_ref[...], (1, reps)) * pos_ref[0]
    cos = jnp.cos(ang)[None, None]
    sin = jnp.sin(ang)[None, None]

    def norm_rope(z, gain_ref):
        z3 = z.reshape(d // dh, dh, t)
        r = lax.rsqrt(jnp.mean(z3 * z3, axis=1, keepdims=True) + EPS)
        z4 = (z3 * r).reshape(d // dh, 2, half, t) * jnp.tile(gain_ref[...], (1, reps)).reshape(1, 2, half, t)
        z1 = z4[:, 0:1]
        z2 = z4[:, 1:2]
        return jnp.concatenate([z1 * cos - z2 * sin, z2 * cos + z1 * sin], axis=1).reshape(d, t)

    q_out[0] = (norm_rope(qkv[:d], qg_ref) * scale).astype(BF16)
    k_out[0] = norm_rope(qkv[d:2 * d], kg_ref).T.astype(BF16)


def _da_pre(x, mod, posr, norm_g, w_qkv, q_norm, k_norm):
    b, s, d = x.shape
    t = min(TOK_TILE, s)
    dh = DA_HEAD_DIM
    invf = (ROPE_THETA ** (-np.arange(0, dh, 2, dtype=np.float32) / dh)).astype(np.float32)
    lanes = lambda v: jnp.broadcast_to(jnp.asarray(v, F32)[:, None], (v.shape[0], LANES))
    row = lambda bi, ti: (0, 0)
    tok = lambda bi, ti: (bi, ti, 0)
    tk = min(ATT_TILE, t)
    return pl.pallas_call(
        functools.partial(_da_pre_kernel, d=d, scale=dh ** -0.5 * LOG2E),
        grid=(b, s // t),
        in_specs=[pl.BlockSpec((1, t, d), tok),
                  pl.BlockSpec((1, 6, d), lambda bi, ti: (bi, 0, 0)),
                  pl.BlockSpec((1, 1, t), lambda bi, ti: (bi, 0, ti)),
                  pl.BlockSpec((1, d), row),
                  pl.BlockSpec((3 * d, d), row),
                  pl.BlockSpec((dh, LANES), row),
                  pl.BlockSpec((dh, LANES), row),
                  pl.BlockSpec((dh // 2, LANES), row)],
        out_specs=[pl.BlockSpec((1, d, t), lambda bi, ti: (bi, 0, ti)),
                   pl.BlockSpec((1, t, d), tok),
                   pl.BlockSpec((1, t // tk, d, tk), lambda bi, ti: (bi, ti, 0, 0))],
        out_shape=[jax.ShapeDtypeStruct((b, d, s), BF16),
                   jax.ShapeDtypeStruct((b, s, d), BF16),
                   jax.ShapeDtypeStruct((b, s // tk, d, tk), BF16)],
        compiler_params=_cparams(("arbitrary", "arbitrary")),
    )(x, mod, posr, norm_g.reshape(1, d), w_qkv.T.astype(BF16), lanes(q_norm), lanes(k_norm), lanes(invf))


def _softmax_block(buf, vt_blk, g, m_ref, l_ref, acc_ref, mask):
    s = buf[0][g]
    if mask is None:
        bmax = buf[1][g]
    else:
        s = jnp.where(mask, s, NEG_INF)
        bmax = jnp.max(s, axis=0, keepdims=True)
    m_prev = m_ref[g]
    m_new = jnp.maximum(m_prev, bmax)
    alpha = jnp.exp2(m_prev - m_new)
    p = jnp.exp2(s - m_new)
    if l_ref is not None:
        l_ref[g] = alpha * l_ref[g] + jnp.sum(p, axis=0, keepdims=True)
    acc_ref[g] = alpha * acc_ref[g] + jnp.dot(vt_blk, p.astype(BF16), preferred_element_type=F32)
    m_ref[g] = m_new


def _causal_sweep(qi, scores, softmax, sa_ref, sb_ref, diag_mask):
    def fill(buf, kb):
        for g, s in enumerate(scores(kb)):
            buf[0][g] = s
            buf[1][g] = jnp.max(s, axis=0, keepdims=True)

    fill(sa_ref, 0)

    def body(j, c):
        kb = 2 * j
        fill(sb_ref, kb + 1)
        softmax(sa_ref, kb, None)
        fill(sa_ref, kb + 2)
        softmax(sb_ref, kb + 1, None)
        return c

    lax.fori_loop(0, lax.shift_right_logical(qi, 1), body, 0)
    odd = (qi & 1) == 1

    @pl.when(jnp.logical_not(odd))
    def _():
        softmax(sa_ref, qi, diag_mask)

    @pl.when(odd)
    def _():
        fill(sb_ref, qi)
        softmax(sa_ref, qi - 1, None)
        softmax(sb_ref, qi, diag_mask)


def _init_stats(m_ref, l_ref, acc_ref):
    if l_ref is not None:
        l_ref[...] = jnp.zeros(l_ref.shape, F32)
    m_ref[...] = jnp.full(m_ref.shape, NEG_INF, F32)
    acc_ref[...] = jnp.zeros(acc_ref.shape, F32)


def _causal_mask_t(tk, cols, tq):
    c = lax.broadcasted_iota(jnp.int32, (tk, cols), 0)
    r = lax.broadcasted_iota(jnp.int32, (tk, cols), 1)
    r = jnp.where(r >= tq, r - tq, r)
    return c <= r


def _da_attn_kernel(lq1_ref, lk1_ref, lq2_ref, lk2_ref, sub_ref, qt_ref, k_ref, vt_ref, o_ref,
                    qs_ref, m_ref, l_ref, acc_ref, sa_ref, sb_ref, ma_ref, mb_ref, *, tq, heads, lam_init):
    qi = pl.program_id(2)
    sub = lax.broadcasted_iota(jnp.int32, (LANES, tq), 0)
    for g in range(heads):
        qt = qt_ref[0, g * LANES:(g + 1) * LANES, :]
        zero = jnp.zeros_like(qt)
        qs_ref[g] = jnp.concatenate([jnp.where(sub < DA_HEAD_DIM, qt, zero),
                                     jnp.where(sub >= DA_HEAD_DIM, qt, zero)], axis=1)
    _init_stats(m_ref, l_ref, acc_ref)

    def scores(kb):
        off = pl.multiple_of(kb * tq, tq)
        return [jnp.dot(k_ref[0, pl.ds(off, tq), g * LANES:(g + 1) * LANES], qs_ref[g],
                        preferred_element_type=F32) for g in range(heads)]

    def softmax(s, kb, mask):
        for g in range(heads):
            _softmax_block(s, vt_ref[0, kb, g * LANES:(g + 1) * LANES, :], g, m_ref, l_ref, acc_ref, mask)

    _causal_sweep(qi, scores, softmax, (sa_ref, ma_ref), (sb_ref, mb_ref), _causal_mask_t(tq, 2 * tq, tq))

    lam = (jnp.exp(jnp.sum(lq1_ref[...] * lk1_ref[...], axis=1, keepdims=True))
           - jnp.exp(jnp.sum(lq2_ref[...] * lk2_ref[...], axis=1, keepdims=True)) + lam_init)
    for g in range(heads):
        ot = acc_ref[g] * (1.0 / l_ref[g])
        dd = (ot[:, :tq] - lam * ot[:, tq:]).T
        ms = jnp.mean(dd * dd, axis=-1, keepdims=True)
        o_ref[0, :, g * LANES:(g + 1) * LANES] = (
            dd * lax.rsqrt(ms + EPS) * sub_ref[...] * (1.0 - lam_init)).astype(BF16)


def _da_attn(qt, k, vt, lq1, lk1, lq2, lk2, subln, lam_init):
    b, s, d = k.shape
    heads = 8
    wb = heads * LANES
    tq = vt.shape[3]
    vec = lambda bi, hi, qi: (0, 0)
    return pl.pallas_call(
        functools.partial(_da_attn_kernel, tq=tq, heads=heads, lam_init=lam_init),
        grid=(b, d // wb, s // tq),
        in_specs=[pl.BlockSpec((1, DA_HEAD_DIM), vec)] * 4 + [
            pl.BlockSpec((1, LANES), vec),
            pl.BlockSpec((1, wb, tq), lambda bi, hi, qi: (bi, hi, qi)),
            pl.BlockSpec((1, s, wb), lambda bi, hi, qi: (bi, 0, hi)),
            pl.BlockSpec((1, s // tq, wb, tq), lambda bi, hi, qi: (bi, 0, hi, 0))],
        out_specs=pl.BlockSpec((1, tq, wb), lambda bi, hi, qi: (bi, qi, hi)),
        out_shape=jax.ShapeDtypeStruct((b, s, d), BF16),
        scratch_shapes=[pltpu.VMEM((heads, LANES, 2 * tq), BF16),
                        pltpu.VMEM((heads, 1, 2 * tq), F32),
                        pltpu.VMEM((heads, 1, 2 * tq), F32),
                        pltpu.VMEM((heads, LANES, 2 * tq), F32),
                        pltpu.VMEM((heads, tq, 2 * tq), F32),
                        pltpu.VMEM((heads, tq, 2 * tq), F32),
                        pltpu.VMEM((heads, 1, 2 * tq), F32),
                        pltpu.VMEM((heads, 1, 2 * tq), F32)],
        compiler_params=_cparams(("arbitrary", "arbitrary", "arbitrary")),
    )(lq1.reshape(1, -1), lk1.reshape(1, -1), lq2.reshape(1, -1), lk2.reshape(1, -1),
      subln.reshape(1, -1), qt, k, vt)


def _mla_attn_kernel(qt_ref, k_ref, vt_ref, o_ref, m_ref, acc_ref, sa_ref, sb_ref, ma_ref, mb_ref,
                     *, tq, heads):
    qi = pl.program_id(2)
    _init_stats(m_ref, None, acc_ref)

    def scores(kb):
        off = pl.multiple_of(kb * tq, tq)
        return [jnp.dot(k_ref[0, pl.ds(off, tq), g * LANES:(g + 1) * LANES],
                        qt_ref[0, g * LANES:(g + 1) * LANES, :],
                        preferred_element_type=F32) for g in range(heads)]

    def softmax(s, kb, mask):
        for g in range(heads):
            _softmax_block(s, vt_ref[0, kb, g * MLA_VP:(g + 1) * MLA_VP, :], g, m_ref, None, acc_ref, mask)

    _causal_sweep(qi, scores, softmax, (sa_ref, ma_ref), (sb_ref, mb_ref), _causal_mask_t(tq, tq, tq))
    for p in range(heads // 2):
        ot = jnp.concatenate([acc_ref[g, :MLA_V] * (1.0 / acc_ref[g, MLA_V:MLA_V + 1])
                              for g in (2 * p, 2 * p + 1)], axis=0)
        o_ref[0, :, p * LANES:(p + 1) * LANES] = ot.T.astype(BF16)


def _mla_attn(qt, k, vt):
    b, s, dk = k.shape
    heads = 16
    tq = vt.shape[3]
    return pl.pallas_call(
        functools.partial(_mla_attn_kernel, tq=tq, heads=heads),
        grid=(b, dk // (heads * LANES), s // tq),
        in_specs=[pl.BlockSpec((1, heads * LANES, tq), lambda bi, hi, qi: (bi, hi, qi)),
                  pl.BlockSpec((1, s, heads * LANES), lambda bi, hi, qi: (bi, 0, hi)),
                  pl.BlockSpec((1, s // tq, heads * MLA_VP, tq), lambda bi, hi, qi: (bi, 0, hi, 0))],
        out_specs=pl.BlockSpec((1, tq, heads * MLA_V), lambda bi, hi, qi: (bi, qi, hi)),
        out_shape=jax.ShapeDtypeStruct((b, s, dk // LANES * MLA_V), BF16),
        scratch_shapes=[pltpu.VMEM((heads, 1, tq), F32),
                        pltpu.VMEM((heads, MLA_VP, tq), F32),
                        pltpu.VMEM((heads, tq, tq), F32),
                        pltpu.VMEM((heads, tq, tq), F32),
                        pltpu.VMEM((heads, 1, tq), F32),
                        pltpu.VMEM((heads, 1, tq), F32)],
        compiler_params=_cparams(("arbitrary", "arbitrary", "arbitrary")),
    )(qt, k, vt)


def _mla_pre_kernel(x_ref, mod_ref, pos_ref, g_ref, wint_ref, qag_ref, kvag_ref, kpeg_ref,
                    wuqt_ref, wukvt_ref, qng_ref, qpg_ref, kng_ref, invf_ref,
                    q_out, k_out, v_out, *, scale):
    h = _adaln_h(x_ref[0], mod_ref, g_ref, 0, 1)
    t = h.shape[0]
    reps = t // LANES
    nh, nope, rope, vd = MLA_HEADS, MLA_NOPE, MLA_ROPE, MLA_V
    lanes = lambda ref: jnp.tile(ref[...], (1, reps))

    def rms(z, axis):
        return z * lax.rsqrt(jnp.mean(z * z, axis=axis, keepdims=True) + EPS)

    lat = jnp.dot(wint_ref[...], h.T.astype(BF16), preferred_element_type=F32)
    cqn = (rms(lat[:MLA_Q_RANK], 0) * lanes(qag_ref)).astype(BF16)
    ckvn = (rms(lat[MLA_Q_RANK:MLA_Q_RANK + MLA_KV_RANK], 0) * lanes(kvag_ref)).astype(BF16)
    kpen = rms(lat[MLA_Q_RANK + MLA_KV_RANK:], 0) * lanes(kpeg_ref)
    q3 = jnp.dot(wuqt_ref[...], cqn, preferred_element_type=F32).reshape(nh, nope + rope, t)
    kv3 = jnp.dot(wukvt_ref[...], ckvn, preferred_element_type=F32).reshape(nh, nope + vd, t)

    vt = jnp.concatenate([kv3[:, nope:], jnp.ones((nh, MLA_VP - vd, t), F32)], axis=1).reshape(nh * MLA_VP, t)
    tk = v_out.shape[3]
    for j in range(v_out.shape[1]):
        v_out[0, j] = vt[:, j * tk:(j + 1) * tk].astype(BF16)

    half = rope // 2
    ang = lanes(invf_ref) * pos_ref[0]
    cos = jnp.cos(ang)[None]
    sin = jnp.sin(ang)[None]

    def rotary(z):
        z1 = z[:, :half]
        z2 = z[:, half:]
        return jnp.concatenate([z1 * cos - z2 * sin, z2 * cos + z1 * sin], axis=1)

    pad = jnp.zeros((nh, LANES - nope - rope, t), F32)
    qn = rms(q3[:, :nope], 1) * lanes(qng_ref)[None]
    qp = rotary(rms(q3[:, nope:], 1) * lanes(qpg_ref)[None])
    q_out[0] = (jnp.concatenate([qn, qp, pad], axis=1).reshape(nh * LANES, t) * scale).astype(BF16)
    kn = rms(kv3[:, :nope], 1) * lanes(kng_ref)[None]
    kp = jnp.broadcast_to(rotary(kpen[None]), (nh, rope, t))
    k_out[0] = jnp.concatenate([kn, kp, pad], axis=1).reshape(nh * LANES, t).T.astype(BF16)


def _mla_pre(x, mod, posr, norm_g, w_in, q_a_norm, kv_a_norm, w_uq, w_ukv,
             q_nope_norm, q_pe_norm, k_nope_norm, k_pe_norm):
    b, s, d = x.shape
    t = min(TOK_TILE, s)
    nh, nope, rope, vd = MLA_HEADS, MLA_NOPE, MLA_ROPE, MLA_V
    w = nh * LANES
    invf = (ROPE_THETA ** (-np.arange(0, rope, 2, dtype=np.float32) / rope)).astype(np.float32)
    lanes = lambda v: jnp.broadcast_to(jnp.asarray(v, F32)[:, None], (v.shape[0], LANES))
    row = lambda bi, ti: (0, 0)
    tok = lambda bi, ti: (bi, ti, 0)
    full = lambda a: pl.BlockSpec(a.shape, row)
    consts = [norm_g.reshape(1, d), w_in.T.astype(BF16), lanes(q_a_norm), lanes(kv_a_norm), lanes(k_pe_norm),
              w_uq.T.astype(BF16), w_ukv.T.astype(BF16), lanes(q_nope_norm), lanes(q_pe_norm),
              lanes(k_nope_norm), lanes(invf)]
    tk = min(ATT_TILE, t)
    return pl.pallas_call(
        functools.partial(_mla_pre_kernel, scale=(nope + rope) ** -0.5 * LOG2E),
        grid=(b, s // t),
        in_specs=[pl.BlockSpec((1, t, d), tok),
                  pl.BlockSpec((1, 6, d), lambda bi, ti: (bi, 0, 0)),
                  pl.BlockSpec((1, 1, t), lambda bi, ti: (bi, 0, ti))] + [full(a) for a in consts],
        out_specs=[pl.BlockSpec((1, w, t), lambda bi, ti: (bi, 0, ti)),
                   pl.BlockSpec((1, t, w), tok),
                   pl.BlockSpec((1, t // tk, nh * MLA_VP, tk), lambda bi, ti: (bi, ti, 0, 0))],
        out_shape=[jax.ShapeDtypeStruct((b, w, s), BF16),
                   jax.ShapeDtypeStruct((b, s, w), BF16),
                   jax.ShapeDtypeStruct((b, s // tk, nh * MLA_VP, tk), BF16)],
        compiler_params=_cparams(("arbitrary", "arbitrary")),
    )(x, mod, posr, *consts)


def _post_kernel(o_ref, wo_ref, x_ref, mod_ref, g_ref, wr_ref, bsel_ref,
                 x_out, h_out, route_out, cnt_out, run_ref, *, d):
    first = (pl.program_id(0) == 0) & (pl.program_id(1) == 0)

    @pl.when(first)
    def _():
        run_ref[...] = jnp.zeros_like(run_ref)

    y = jnp.dot(o_ref[0], wo_ref[...], preferred_element_type=F32)
    x = x_ref[0] + mod_ref[0, 2:3, :] * y
    x_out[0] = x
    h = _adaln_h(x, mod_ref, g_ref, 3, 4)
    t = h.shape[0]

    hh = h.astype(BF16)
    hl = (h - hh.astype(F32)).astype(BF16)
    logits = (jnp.dot(hh, wr_ref[0], preferred_element_type=F32)
              + jnp.dot(hl, wr_ref[0], preferred_element_type=F32)
              + jnp.dot(hh, wr_ref[1], preferred_element_type=F32))

    ng, ne = N_GROUPS, EXPERTS_PER_GROUP

    def route(lt):
        n = lt.shape[1]
        sub = lax.broadcasted_iota(jnp.int32, (SUBLANES, n), 0).astype(F32)

        def first_argmax(val):
            mx = jnp.max(val, axis=0, keepdims=True)
            return jnp.min(jnp.where(val == mx, sub, float(SUBLANES)), axis=0, keepdims=True)

        def pick(val, idx):
            return jnp.sum(jnp.where(sub == idx, val, 0.0), axis=0, keepdims=True)

        gmask = sub < ng
        gl = jnp.where(gmask, lt[:SUBLANES], NEG_INF)
        ge = jnp.exp(gl - jnp.max(gl, axis=0, keepdims=True))
        gprob = ge / jnp.sum(ge, axis=0, keepdims=True)
        gidx = first_argmax(jnp.where(gmask, gprob + bsel_ref[0], NEG_INF))
        ggate = pick(gprob, gidx)

        el = jnp.zeros((SUBLANES, n), F32)
        eb = jnp.zeros((SUBLANES, n), F32)
        for g in range(ng):
            chosen = gidx == float(g)
            el = jnp.where(chosen, lt[SUBLANES * (g + 1):SUBLANES * (g + 2)], el)
            eb = jnp.where(chosen, bsel_ref[g + 1], eb)
        ee = jnp.exp(el - jnp.max(el, axis=0, keepdims=True))
        eprob = ee / jnp.sum(ee, axis=0, keepdims=True)
        sel = eprob + eb
        e1 = first_argmax(sel)
        e2 = first_argmax(jnp.where(sub == e1, NEG_INF, sel))
        p1 = pick(eprob, e1)
        p2 = pick(eprob, e2)
        psum = p1 + p2
        w1 = p1 / psum * ggate
        w2 = p2 / psum * ggate
        swap = e2 < e1
        lo = jnp.where(swap, e2, e1)
        hi = jnp.where(swap, e1, e2)
        combo = gidx * float(N_PAIRS) + lo * (2.0 * ne - 1.0 - lo) * 0.5 + (hi - lo - 1.0)
        return jnp.where(swap, w2, w1), jnp.where(swap, w1, w2), combo

    lt = logits.T
    parts = [route(lt[:, j * (t // 2):(j + 1) * (t // 2)]) for j in range(2)]
    wa, wb, combo = [jnp.concatenate([p[i] for p in parts], axis=1) for i in range(3)]
    sub = lax.broadcasted_iota(jnp.int32, (SUBLANES, t), 0).astype(F32)

    row = lax.broadcasted_iota(jnp.int32, (LANES, t), 0).astype(F32)
    wrow = jnp.where(row == 0.0, wa, jnp.where(row == 1.0, wb, 0.0)).T
    _pack_token_tiles(h_out, h, wrow)

    onehot = row == combo
    oh = jnp.where(onehot, 1.0, 0.0)
    r_i = lax.broadcasted_iota(jnp.int32, (t, t), 0)
    c_i = lax.broadcasted_iota(jnp.int32, (t, t), 1)
    earlier = jnp.where(r_i < c_i, 1.0, 0.0).astype(BF16)
    before = jnp.dot(oh.astype(BF16), earlier, preferred_element_type=F32) + run_ref[...]
    rank = jnp.sum(jnp.where(onehot, before, 0.0), axis=0, keepdims=True)
    run = run_ref[...] + jnp.sum(oh, axis=1, keepdims=True)
    run_ref[...] = run
    cnt_out[...] = run
    route_out[0] = jnp.where(sub == 0.0, combo, jnp.where(sub == 1.0, rank, 0.0))


def _post(o, w_o, x, mod, norm_g, w_group, b_group, w_router, b_router):
    b, s, d = x.shape
    t = min(TOK_TILE, s)
    do = o.shape[2]
    ng, ne = N_GROUPS, EXPERTS_PER_GROUP
    wr = jnp.concatenate([jnp.pad(w_group, ((0, 0), (0, SUBLANES - ng))),
                          jnp.pad(w_router, ((0, 0), (0, LANES - SUBLANES - ng * ne)))], axis=1)
    wr_hi = wr.astype(BF16)
    wr_lo = (wr - wr_hi.astype(F32)).astype(BF16)
    wr2 = jnp.stack([wr_hi, wr_lo])
    bsel = jnp.concatenate([jnp.pad(b_group, (0, SUBLANES - ng))[None, :], b_router], axis=0)[:, :, None]
    row = lambda bi, ti: (0, 0)
    tok = lambda bi, ti: (bi, ti, 0)
    return pl.pallas_call(
        functools.partial(_post_kernel, d=d),
        grid=(b, s // t),
        in_specs=[pl.BlockSpec((1, t, do), tok),
                  pl.BlockSpec((do, d), row),
                  pl.BlockSpec((1, t, d), tok),
                  pl.BlockSpec((1, 6, d), lambda bi, ti: (bi, 0, 0)),
                  pl.BlockSpec((1, d), row),
                  pl.BlockSpec((2, d, LANES), lambda bi, ti: (0, 0, 0)),
                  pl.BlockSpec((ng + 1, ne, 1), lambda bi, ti: (0, 0, 0))],
        out_specs=[pl.BlockSpec((1, t, d), tok),
                   pl.BlockSpec((1, t * SUBLANES, LANES), tok),
                   pl.BlockSpec((1, SUBLANES, t), lambda bi, ti: (bi * (s // t) + ti, 0, 0)),
                   pl.BlockSpec((LANES, 1), row)],
        out_shape=[jax.ShapeDtypeStruct((b, s, d), F32),
                   jax.ShapeDtypeStruct((b, s * SUBLANES, LANES), jnp.uint32),
                   jax.ShapeDtypeStruct((b * s // t, SUBLANES, t), F32),
                   jax.ShapeDtypeStruct((LANES, 1), F32)],
        scratch_shapes=[pltpu.VMEM((LANES, 1), F32)],
        compiler_params=_cparams(("arbitrary", "arbitrary")),
    )(o, w_o.astype(BF16), x, mod, norm_g.reshape(1, d), wr2, bsel)


IN_ROWS = SUBLANES
OUT_ROWS = SUBLANES // 2


def _tok_rows(s, n, per):
    return pl.ds(s, n, stride=per)


def _tok_span(tok, per):
    return pl.ds(pl.multiple_of(tok * per, per), per)


def _pack_words(x):
    half = x.shape[1] // 2
    bits = pltpu.bitcast(x.astype(BF16).astype(F32), jnp.uint32)
    return bits[:, :half] | (bits[:, half:] >> 16)


def _unpack_words(w):
    return pltpu.bitcast(w & jnp.uint32(0xFFFF0000), F32), pltpu.bitcast(w << 16, F32)


def _pack_token_tiles(p_out, h, wrow):
    t, d = h.shape
    word = _pack_words(h)
    nw = d // 2 // LANES
    for s in range(nw):
        p_out[0, _tok_rows(s, t, IN_ROWS), :] = word[:, s * LANES:(s + 1) * LANES]
    p_out[0, _tok_rows(nw, t, IN_ROWS), :] = pltpu.bitcast(wrow, jnp.uint32)
    for s in range(nw + 1, IN_ROWS):
        p_out[0, _tok_rows(s, t, IN_ROWS), :] = jnp.zeros((t, LANES), jnp.uint32)


def _unpack_token_tiles(hs_ref, d):
    nw = d // 2 // LANES
    tm = hs_ref.shape[0] // IN_ROWS
    pairs = [_unpack_words(hs_ref[_tok_rows(s, tm, IN_ROWS), :]) for s in range(nw)]
    x = jnp.concatenate([p[0].astype(BF16) for p in pairs] + [p[1].astype(BF16) for p in pairs], axis=1)
    wrow = pltpu.bitcast(hs_ref[_tok_rows(nw, tm, IN_ROWS), :], F32)
    return x, wrow[:, 0:1], wrow[:, 1:2]


def _row_copy(src_ref, src_row, dst_ref, dst_row, sem, per):
    return pltpu.make_async_copy(src_ref.at[_tok_span(src_row, per), :], dst_ref.at[_tok_span(dst_row, per), :], sem)


def _start_rows(rows, make_copy):
    def start(i, c):
        for u in range(ROW_UNROLL):
            make_copy(i * ROW_UNROLL + u).start(priority=u % 2)
        return c

    lax.fori_loop(0, rows // ROW_UNROLL, start, 0)


def _wait_rows(rows, make_copy):
    def wait(i, c):
        for u in range(ROW_UNROLL):
            make_copy(0).wait()
        return c

    lax.fori_loop(0, rows // ROW_UNROLL, wait, 0)


def _dispatch_kernel(dest_ref, h_ref, init_ref, hs_ref, sem, *, rows):
    del init_ref
    copy = lambda r: _row_copy(h_ref, r, hs_ref, dest_ref[r], sem, IN_ROWS)
    _start_rows(rows, copy)
    _wait_rows(rows, copy)


def _dispatch(dest, h2, n_rows):
    n = h2.shape[0] // IN_ROWS
    rows = min(ROW_TILE, n)
    return pl.pallas_call(
        functools.partial(_dispatch_kernel, rows=rows),
        grid=(n // rows,),
        in_specs=[pl.BlockSpec((rows,), lambda i: (i,), memory_space=pltpu.SMEM),
                  pl.BlockSpec((rows * IN_ROWS, LANES), lambda i: (i, 0)),
                  pl.BlockSpec(memory_space=pl.ANY)],
        out_specs=pl.BlockSpec(memory_space=pl.ANY),
        out_shape=jax.ShapeDtypeStruct((n_rows * IN_ROWS, LANES), jnp.uint32),
        scratch_shapes=[pltpu.SemaphoreType.DMA(())],
        input_output_aliases={2: 0},
        compiler_params=_cparams(("arbitrary",)),
    )(dest, h2, jnp.zeros((n_rows * IN_ROWS, LANES), jnp.uint32))


def _moe_kernel(elo_ref, ehi_ref, blk_ref, nact_ref, hs_ref, w1a_ref, w3a_ref, w2a_ref,
                w1b_ref, w3b_ref, w2b_ref, y_ref, *, d):
    del elo_ref, ehi_ref, blk_ref
    active = pl.program_id(0) < nact_ref[0]

    @pl.when(jnp.logical_not(active))
    def _():
        y_ref[...] = jnp.zeros_like(y_ref)

    @pl.when(active)
    def _():
        x, wa, wb = _unpack_token_tiles(hs_ref, d)

        def expert(w1_ref, w3_ref, wgt):
            a = jnp.dot(x, w1_ref[0], preferred_element_type=F32)
            g = jnp.dot(x, w3_ref[0], preferred_element_type=F32)
            return (a * jax.nn.sigmoid(a) * g * wgt).astype(BF16)

        y = (jnp.dot(expert(w1a_ref, w3a_ref, wa), w2a_ref[0], preferred_element_type=F32)
             + jnp.dot(expert(w1b_ref, w3b_ref, wb), w2b_ref[0], preferred_element_type=F32))
        word = _pack_words(y)
        for s in range(OUT_ROWS):
            y_ref[_tok_rows(s, y.shape[0], OUT_ROWS), :] = word[:, s * LANES:(s + 1) * LANES]


def _moe(hs, e_lo, e_hi, blk, nact, w1, w3, w2):
    n_rows = hs.shape[0] // IN_ROWS
    d = w1.shape[1]
    ff = w1.shape[2]
    tm = MOE_TILE
    n_tiles = n_rows // tm
    assert d == 2 * OUT_ROWS * LANES
    wspec = lambda shape, which: pl.BlockSpec(shape, (lambda j, lo, hi, bk, na: (lo[j], 0, 0)) if which == 0
                                              else (lambda j, lo, hi, bk, na: (hi[j], 0, 0)))
    grid_spec = pltpu.PrefetchScalarGridSpec(
        num_scalar_prefetch=4,
        grid=(n_tiles,),
        in_specs=[pl.BlockSpec((tm * IN_ROWS, LANES), lambda j, lo, hi, bk, na: (bk[j], 0)),
                  wspec((1, d, ff), 0), wspec((1, d, ff), 0), wspec((1, ff, d), 0),
                  wspec((1, d, ff), 1), wspec((1, d, ff), 1), wspec((1, ff, d), 1)],
        out_specs=pl.BlockSpec((tm * OUT_ROWS, LANES), lambda j, lo, hi, bk, na: (j, 0)),
    )
    return pl.pallas_call(
        functools.partial(_moe_kernel, d=d),
        grid_spec=grid_spec,
        out_shape=jax.ShapeDtypeStruct((n_rows * OUT_ROWS, LANES), jnp.uint32),
        compiler_params=_cparams(("arbitrary",)),
    )(e_lo, e_hi, blk, nact, hs, w1, w3, w2, w1, w3, w2)


def _combine_kernel(dest_ref, next_ref, x_ref, gate_ref, y_ref, o_ref, buf_ref, sem, *, rows):
    i = pl.program_id(0)
    slot = i & 1

    def gather(d_ref, sl):
        return lambda r: _row_copy(y_ref, d_ref[r], buf_ref.at[sl], r, sem.at[sl], OUT_ROWS)

    @pl.when(i == 0)
    def _():
        _start_rows(rows, gather(dest_ref, 0))

    @pl.when(i + 1 < pl.num_programs(0))
    def _():
        _start_rows(rows, gather(next_ref, 1 - slot))

    _wait_rows(rows, gather(dest_ref, slot))
    half = OUT_ROWS * LANES
    for s in range(OUT_ROWS):
        hi, lo = _unpack_words(buf_ref[slot, _tok_rows(s, rows, OUT_ROWS), :])
        for off, val in ((s * LANES, hi), (half + s * LANES, lo)):
            cols = slice(off, off + LANES)
            o_ref[:, cols] = x_ref[:, cols] + gate_ref[0, :, cols] * val


def _combine(dest, x2, gate, y, seq):
    n, d = x2.shape
    rows = min(ROW_TILE, seq)
    per_seq = seq // rows
    steps = n // rows
    return pl.pallas_call(
        functools.partial(_combine_kernel, rows=rows),
        grid=(steps,),
        in_specs=[pl.BlockSpec((rows,), lambda i: (i,), memory_space=pltpu.SMEM),
                  pl.BlockSpec((rows,), lambda i: (jnp.minimum(i + 1, steps - 1),), memory_space=pltpu.SMEM),
                  pl.BlockSpec((rows, d), lambda i: (i, 0)),
                  pl.BlockSpec((1, 1, d), lambda i: (i // per_seq, 0, 0)),
                  pl.BlockSpec(memory_space=pl.ANY)],
        out_specs=pl.BlockSpec((rows, d), lambda i: (i, 0)),
        out_shape=jax.ShapeDtypeStruct((n, d), F32),
        scratch_shapes=[pltpu.VMEM((2, rows * OUT_ROWS, LANES), jnp.uint32), pltpu.SemaphoreType.DMA((2,))],
        compiler_params=_cparams(("arbitrary",)),
    )(dest, dest, x2, gate, y)


def _moe_layer(x_new, h2, route, counts, gate_f, w1, w3, w2):
    b, s, d = x_new.shape
    n = b * s
    tm = MOE_TILE
    n_tiles = n // tm + N_COMBOS
    combo = route[:, 0, :].reshape(n).astype(jnp.int32)
    rank = route[:, 1, :].reshape(n).astype(jnp.int32)
    cnt = counts[:N_COMBOS, 0].astype(jnp.int32)
    tiles_per = (cnt + tm - 1) // tm
    tile_end = jnp.cumsum(tiles_per)
    row_off = (tile_end - tiles_per) * tm
    ids = jnp.arange(N_COMBOS, dtype=jnp.int32)

    def lookup(table, idx):
        return jnp.sum(jnp.where(idx[:, None] == ids[None, :], table[None, :], 0), axis=1)

    dest = lookup(row_off, combo) + rank
    nact = tile_end[-1:]
    blk = jnp.minimum(jnp.arange(n_tiles, dtype=jnp.int32), nact[0] - 1)
    tile_combo = jnp.sum((tile_end[None, :] <= blk[:, None]).astype(jnp.int32), axis=1)
    e_lo = lookup(jnp.asarray(_COMBO_LO), tile_combo)
    e_hi = lookup(jnp.asarray(_COMBO_HI), tile_combo)

    hs = _dispatch(dest, h2.reshape(n * SUBLANES, LANES), n_tiles * tm)
    ne = w1.shape[0] * w1.shape[1]
    y = _moe(hs, e_lo, e_hi, blk, nact.astype(jnp.int32),
             w1.reshape(ne, d, -1).astype(BF16), w3.reshape(ne, d, -1).astype(BF16),
             w2.reshape(ne, -1, d).astype(BF16))
    return _combine(dest, x_new.reshape(n, d), gate_f, y, s).reshape(b, s, d)


def kernel(x, c, positions, ada_w, ada_b, norm_mix, norm_ffn, da_w_qkv, da_q_norm, da_k_norm, da_lambda_q1, da_lambda_k1, da_lambda_q2, da_lambda_k2, da_subln, da_w_o, mla_w_in, mla_q_a_norm, mla_kv_a_norm, mla_w_uq, mla_w_ukv, mla_q_nope_norm, mla_q_pe_norm, mla_k_nope_norm, mla_k_pe_norm, mla_w_o, moe_w_group, moe_b_group, moe_w_router, moe_b_router, moe_w1, moe_w3, moe_w2):
    b, s, d = x.shape
    depth = ada_w.shape[0]
    mod_all = _modulation(c, ada_w, ada_b).reshape(depth, b, 6, d)
    posr = positions.astype(F32).reshape(b, 1, s)
    for i in range(depth):
        mod = mod_all[i]
        j = i // 2
        if i % 2 == 0:
            lam_init = 0.8 - 0.6 * math.exp(-0.3 * i)
            q, k, v = _da_pre(x, mod, posr, norm_mix[i], da_w_qkv[j], da_q_norm[j], da_k_norm[j])
            o = _da_attn(q, k, v, da_lambda_q1[j], da_lambda_k1[j], da_lambda_q2[j], da_lambda_k2[j],
                         da_subln[j], lam_init)
            w_o = da_w_o[j]
        else:
            q, k, v = _mla_pre(x, mod, posr, norm_mix[i], mla_w_in[j], mla_q_a_norm[j], mla_kv_a_norm[j],
                               mla_w_uq[j], mla_w_ukv[j], mla_q_nope_norm[j], mla_q_pe_norm[j],
                               mla_k_nope_norm[j], mla_k_pe_norm[j])
            o = _mla_attn(q, k, v)
            w_o = mla_w_o[j]
        x_new, h2, route, counts = _post(o, w_o, x, mod, norm_ffn[i], moe_w_group[i], moe_b_group[i],
                                         moe_w_router[i], moe_b_router[i])
        x = _moe_layer(x_new, h2, route, counts, mod[:, 5:6, :], moe_w1[i], moe_w3[i], moe_w2[i])
    return x
```
